```python
import jax
import jax.numpy as jnp
from jax import lax
import numpy as np

D_MODEL = 1024
BATCH = 32
SEQ = 256
DEPTH = 2
DEC_BATCH = 8
DEC_SEQ = 2048
PAST_LEN = 512

GRID_W = 64
N_EVEN = (DEPTH + 1) // 2
N_ODD = DEPTH // 2
D_FF = 2816
N_MOD = 9
CHUNK = 128
Q_BLOCK = 128
ROPE_BASE = 10000.0
H_A = 4
DK_A = 64
DV_A = 128
H_B = 8
HS_B = 64
LORA_W = 64
LORA_A = 64
LORA_G = 128
H_C = 4
DH_C = 128
HQ_D = 8
HKV_D = 2
HD_D = 64

WA = H_A * DV_A
WB = H_B * HS_B
WC = H_C * DH_C
WD = HQ_D * HD_D
P_A = 2 * H_A * DK_A + 2 * WA
P_B = 3 * WB + LORA_W + LORA_A + LORA_G
P_C = 3 * WC + 4 * H_C + WC
P_D = WD + 2 * HKV_D * HD_D

kernel_name = 'hybrid_bidir_diffusion_step'


def rms_norm(x, g, eps=1e-6):
    xf = x.astype(jnp.float32)
    y = xf * lax.rsqrt(jnp.mean(xf * xf, axis=-1, keepdims=True) + eps)
    return (y * g.astype(jnp.float32)).astype(x.dtype)


def head_layer_norm(y, eps):
    y = y.astype(jnp.float32)
    mu = jnp.mean(y, axis=-1, keepdims=True)
    var = jnp.mean(jnp.square(y - mu), axis=-1, keepdims=True)
    return (y - mu) * lax.rsqrt(var + eps)


def to_heads(p, h):
    b, l, w = p.shape
    return p.reshape(b, l, h, w // h).transpose(0, 2, 1, 3)


def from_heads(o):
    b, h, l, d = o.shape
    return o.transpose(0, 2, 1, 3).reshape(b, l, h * d)


def tflip(t):
    return jnp.flip(t, axis=2)


def center_shift(p):
    z = jnp.zeros_like(p[:, :1])
    prev = jnp.concatenate([z, p[:, :-1]], axis=1)
    nxt = jnp.concatenate([p[:, 1:], z], axis=1)
    return 0.5 * (prev + nxt)


def axial_rope(x):
    n, d = x.shape[-2], x.shape[-1]
    rows = n // GRID_W
    row = jnp.repeat(jnp.arange(rows), GRID_W).astype(jnp.float32)
    col = (jnp.arange(rows * GRID_W) % GRID_W).astype(jnp.float32)
    da = d // 2
    nf = da // 2
    inv = ROPE_BASE ** (-jnp.arange(nf, dtype=jnp.float32) / nf)

    def rot(xa, pos):
        ang = pos[:, None] * inv[None, :]
        cos = jnp.cos(ang).astype(x.dtype)
        sin = jnp.sin(ang).astype(x.dtype)
        x1, x2 = xa[..., :nf], xa[..., nf:]
        return jnp.concatenate([x1 * cos - x2 * sin, x1 * sin + x2 * cos], axis=-1)

    return jnp.concatenate([rot(x[..., :da], row), rot(x[..., da:], col)], axis=-1)


def to_chunks(x):
    b, h, l = x.shape[:3]
    x = x.reshape(b, h, l // CHUNK, CHUNK, *x.shape[3:])
    return jnp.moveaxis(x, 2, 0)


def from_chunks(x):
    x = jnp.moveaxis(x, 0, 2)
    b, h, nc, cl = x.shape[:4]
    return x.reshape(b, h, nc * cl, *x.shape[4:])


def retention_scan(q, k, v, log_g, s0):
    idx = jnp.arange(CHUNK, dtype=jnp.float32)
    diff = idx[:, None] - idx[None, :]
    causal = diff >= 0
    lg = log_g[:, None, None]
    d_intra = jnp.where(causal, jnp.exp(lg * jnp.where(causal, diff, 0.0)), 0.0)
    q_dec = jnp.exp(log_g[:, None] * (idx + 1.0))[:, :, None]
    k_dec = jnp.exp(log_g[:, None] * (CHUNK - 1.0 - idx))[:, :, None]
    c_dec = jnp.exp(log_g * CHUNK)[:, None, None]

    def step(s, inp):
        qc, kc, vc = inp
        att = jnp.einsum('bhid,bhjd->bhij', qc, kc) * d_intra
        o = jnp.einsum('bhij,bhje->bhie', att, vc) + jnp.einsum('bhid,bhde->bhie', qc * q_dec, s)
        s = s * c_dec + jnp.einsum('bhjd,bhje->bhde', kc * k_dec, vc)
        return s, o

    s_fin, o = lax.scan(step, s0, (to_chunks(q), to_chunks(k), to_chunks(v)))
    return from_chunks(o), s_fin


def mlstm_scan(q, k, v, ig, lf, c0, n0, m0):
    idx = jnp.arange(CHUNK)
    causal = idx[:, None] >= idx[None, :]

    def step(carry, inp):
        cm, nv, m = carry
        qc, kc, vc, igc, lfc = inp
        b = jnp.cumsum(lfc, axis=-1)
        a = b + m[..., None]
        dlog = jnp.where(causal, b[..., :, None] - b[..., None, :] + igc[..., None, :], -jnp.inf)
        mt = jnp.maximum(a, jnp.max(dlog, axis=-1))
        w = jnp.exp(dlog - mt[..., None])
        s = jnp.einsum('bhid,bhjd->bhij', qc, kc) * w
        inter = jnp.exp(a - mt)
        num = jnp.einsum('bhij,bhje->bhie', s, vc) + inter[..., None] * jnp.einsum('bhid,bhde->bhie', qc, cm)
        den = jnp.sum(s, axis=-1) + inter * jnp.einsum('bhid,bhd->bhi', qc, nv)
        h = num / jnp.maximum(jnp.abs(den), jnp.exp(-mt))[..., None]
        bl = b[..., -1]
        wl = bl[..., None] - b + igc
        m_new = jnp.maximum(bl + m, jnp.max(wl, axis=-1))
        wk = jnp.exp(wl - m_new[..., None])[..., None] * kc
        dec = jnp.exp(bl + m - m_new)
        cm = dec[..., None, None] * cm + jnp.einsum('bhjd,bhje->bhde', wk, vc)
        nv = dec[..., None] * nv + jnp.sum(wk, axis=2)
        return (cm, nv, m_new), h

    seqs = tuple(to_chunks(t) for t in (q, k, v, ig, lf))
    (c_f, n_f, m_f), h = lax.scan(step, (c0, n0, m0), seqs)
    return from_chunks(h), c_f, n_f, m_f


def rwkv_scan(r, w, k, v, a, b, s0):
    def step(s, inp):
        rt, wt, kt, vt, at, bt = inp
        sa = jnp.einsum('bhij,bhj->bhi', s, at)
        s = s * wt[..., None, :] + sa[..., :, None] * bt[..., None, :] + vt[..., :, None] * kt[..., None, :]
        return s, jnp.einsum('bhij,bhj->bhi', s, rt)

    seqs = tuple(jnp.moveaxis(t, 2, 0) for t in (r, w, k, v, a, b))
    s_fin, y = lax.scan(step, s0, seqs)
    return jnp.moveaxis(y, 0, 2), s_fin


def block_attention(q, k, v):
    b, hq, l, hd = q.shape
    hkv = k.shape[1]
    g = hq // hkv
    nb = l // Q_BLOCK
    qb = jnp.moveaxis(q.reshape(b, hkv, g, nb, Q_BLOCK, hd), 3, 0)
    scale = hd ** -0.5

    def one(qblk):
        s = jnp.einsum('bkgqd,bksd->bkgqs', qblk, k).astype(jnp.float32) * scale
        pr = jax.nn.softmax(s, axis=-1).astype(v.dtype)
        return jnp.einsum('bkgqs,bksd->bkgqd', pr, v)

    o = lax.map(one, qb)
    return jnp.moveaxis(o, 0, 3).reshape(b, hq, l, hd)


def retention_mixer(p, log_decay, gn_w, s0, latent):
    f32 = jnp.float32
    qk = H_A * DK_A
    q = to_heads(p[..., :qk], H_A)
    k = to_heads(p[..., qk:2 * qk], H_A)
    v = to_heads(p[..., 2 * qk:2 * qk + WA], H_A).astype(f32)
    g = p[..., 2 * qk + WA:]
    if latent:
        q = axial_rope(q)
        k = axial_rope(k)
    q = q.astype(f32)
    k = k.astype(f32) * DK_A ** -0.5
    s0 = s0.astype(f32)
    log_decay = log_decay.astype(f32)
    o_f, s_f = retention_scan(q, k, v, log_decay[0], s0[:, 0])
    o_b, s_b = retention_scan(tflip(q), tflip(k), tflip(v), log_decay[1], s0[:, 1])
    o = from_heads(head_layer_norm(o_f + tflip(o_b), 1e-5)) * gn_w
    out = jax.nn.silu(g) * o.astype(p.dtype)
    return out, jnp.stack([s_f, s_b], axis=1)


def rwkv_mixer(p, mu, w0, w2, a0, a2, g2, k_k, k_a, r_k, ln_w, ln_b, s0):
    f32 = jnp.float32
    p = p + mu * (center_shift(p) - p)
    r, k, v = p[..., :WB], p[..., WB:2 * WB], p[..., 2 * WB:3 * WB]
    o = 3 * WB
    wd = jnp.tanh(p[..., o:o + LORA_W])
    ad = p[..., o + LORA_W:o + LORA_W + LORA_A]
    gd = p[..., o + LORA_W + LORA_A:]
    g = jax.nn.sigmoid(gd) @ g2
    kk = to_heads(k * k_k, H_B).astype(f32)
    kk = kk / jnp.maximum(jnp.sqrt(jnp.sum(kk * kk, axis=-1, keepdims=True)), 1e-12)
    r_h = to_heads(r, H_B).astype(f32)
    k_h = to_heads(k, H_B).astype(f32)
    v_h = to_heads(v, H_B).astype(f32)
    ys, ss = [], []
    for d in range(2):
        w_log = -jax.nn.softplus(-(w0[d] + wd @ w2[d])) - 0.5
        decay = to_heads(jnp.exp(-jnp.exp(w_log.astype(f32))), H_B)
        a = jax.nn.sigmoid(a0[d] + ad @ a2[d])
        kd = to_heads(k * (1 + (a - 1) * k_a), H_B).astype(f32)
        a_h = to_heads(a, H_B).astype(f32)
        seqs = (r_h, decay, kd, v_h, -kk, kk * a_h)
        if d == 1:
            seqs = tuple(tflip(t) for t in seqs)
        y, s = rwkv_scan(*seqs, s0[:, d].astype(f32))
        ys.append(y if d == 0 else tflip(y))
        ss.append(s)
    y = from_heads(head_layer_norm(ys[0] + ys[1], 64e-5)) * ln_w + ln_b
    bonus = from_heads(jnp.sum(r_h * k_h * r_k[:, None, :].astype(f32), axis=-1, keepdims=True) * v_h)
    out = (y + bonus).astype(p.dtype) * g
    return out, jnp.stack(ss, axis=1)


def mlstm_mixer(p, i_bias, f_bias, norm_w, c0, n0, m0):
    f32 = jnp.float32
    bsz, seq = p.shape[0], p.shape[1]
    q = to_heads(p[..., :WC], H_C).astype(f32)
    k = to_heads(p[..., WC:2 * WC], H_C).astype(f32) * DH_C ** -0.5
    v = to_heads(p[..., 2 * WC:3 * WC], H_C).astype(f32)
    o = 3 * WC
    ig = p[..., o:o + 2 * H_C].reshape(bsz, seq, 2, H_C) + i_bias
    fg = p[..., o + 2 * H_C:o + 4 * H_C].reshape(bsz, seq, 2, H_C) + f_bias
    og = p[..., o + 4 * H_C:]
    ig = jnp.transpose(ig, (2, 0, 3, 1)).astype(f32)
    lf = jax.nn.log_sigmoid(jnp.transpose(fg, (2, 0, 3, 1)).astype(f32))
    c0 = c0.astype(f32)
    n0 = n0.astype(f32)
    m0 = m0.astype(f32)
    h_f, cf, nf, mf = mlstm_scan(q, k, v, ig[0], lf[0], c0[:, 0], n0[:, 0], m0[:, 0])
    h_b, cb, nb, mb = mlstm_scan(tflip(q), tflip(k), tflip(v), tflip(ig[1]), tflip(lf[1]),
                                 c0[:, 1], n0[:, 1], m0[:, 1])
    h = from_heads(head_layer_norm(h_f + tflip(h_b), 1e-5)) * norm_w
    out = jax.nn.sigmoid(og) * h.astype(p.dtype)
    return out, jnp.stack([cf, cb], axis=1), jnp.stack([nf, nb], axis=1), jnp.stack([mf, mb], axis=1)


def attention_mixer(p, qk_gain, ctx_k, ctx_v, latent):
    kvw = HKV_D * HD_D
    q = to_heads(p[..., :WD], HQ_D)
    k = to_heads(p[..., WD:WD + kvw], HKV_D)
    v = to_heads(p[..., WD + kvw:], HKV_D)
    q = rms_norm(q, qk_gain[0])
    k = rms_norm(k, qk_gain[1])
    if latent:
        q = axial_rope(q)
        k_lat = axial_rope(k)
        keys = jnp.concatenate([ctx_k.astype(k.dtype), k_lat], axis=2)
        vals = jnp.concatenate([ctx_v.astype(v.dtype), v], axis=2)
    else:
        keys, vals = k, v
    o = block_attention(q, keys, vals)
    return from_heads(o), k, v


def modulation(cond, ada_w, ada_b):
    m = jax.nn.silu(cond) @ ada_w + ada_b
    return jnp.split(m[:, None, :], N_MOD, axis=-1)


def swiglu(h, w1, w2):
    gate, up = jnp.split(h @ w1, 2, axis=-1)
    return (jax.nn.silu(gate) * up) @ w2


def setup_inputs(seed: int = 0) -> dict:
    key = jax.random.key(seed)
    ks = iter(jax.random.split(key, 48))
    f32 = jnp.float32

    def nrm(shape, scale=1.0):
        return jax.random.normal(next(ks), shape, f32) * scale

    d = D_MODEL
    ret_base = jnp.log1p(-(2.0 ** (-5.0 - jnp.arange(H_A, dtype=f32))))
    w0_base = jnp.repeat(jnp.linspace(-6.0, -0.5, HS_B)[None, :], H_B, axis=0).reshape(-1)
    return {
        'x_prompt': nrm((BATCH, SEQ, d)),
        'x_sample': nrm((DEC_BATCH, DEC_SEQ, d)),
        'state_ret': nrm((DEC_BATCH, N_EVEN, 2, H_A, DK_A, DV_A), 0.5),
        'state_rwkv': nrm((DEC_BATCH, N_EVEN, 2, H_B, HS_B, HS_B), 0.3),
        'state_mlstm_c': nrm((DEC_BATCH, N_ODD, 2, H_C, DH_C, DH_C), 0.3),
        'state_mlstm_n': nrm((DEC_BATCH, N_ODD, 2, H_C, DH_C), 0.3),
        'state_mlstm_m': nrm((DEC_BATCH, N_ODD, 2, H_C), 0.5),
        'cache_k': nrm((DEC_BATCH, N_ODD, HKV_D, PAST_LEN, HD_D)),
        'cache_v': nrm((DEC_BATCH, N_ODD, HKV_D, PAST_LEN, HD_D)),
        'c': nrm((DEC_BATCH, d)),
        'c_ctx': nrm((d,)),
        'ada_w': nrm((DEPTH, d, N_MOD * d), 0.5 * d ** -0.5),
        'ada_b': nrm((DEPTH, N_MOD * d), 0.02),
        'norm_g': 1.0 + nrm((DEPTH, 3, d), 0.02),
        'ffn_w1': nrm((DEPTH, 2, d, 2 * D_FF), d ** -0.5),
        'ffn_w2': nrm((DEPTH, 2, D_FF, d), D_FF ** -0.5),
        'w_in_even': nrm((N_EVEN, d, P_A + P_B), d ** -0.5),
        'w_out_even': nrm((N_EVEN, WA + WB, d), (WA + WB) ** -0.5),
        'ret_log_decay': ret_base[None, None, :] * (1.0 + nrm((N_EVEN, 2, H_A), 0.05)),
        'ret_gn_w': 1.0 + nrm((N_EVEN, WA), 0.02),
        'rwkv_mu': 0.5 + nrm((N_EVEN, P_B), 0.1),
        'rwkv_w0': w0_base[None, None, :] + nrm((N_EVEN, 2, WB), 0.1),
        'rwkv_w2': nrm((N_EVEN, 2, LORA_W, WB), 0.1 * LORA_W ** -0.5),
        'rwkv_a0': nrm((N_EVEN, 2, WB), 0.1),
        'rwkv_a2': nrm((N_EVEN, 2, LORA_A, WB), LORA_A ** -0.5),
        'rwkv_g2': nrm((N_EVEN, LORA_G, WB), LORA_G ** -0.5),
        'rwkv_k_k': 0.85 + nrm((N_EVEN, WB), 0.02),
        'rwkv_k_a': 1.0 + nrm((N_EVEN, WB), 0.02),
        'rwkv_r_k': nrm((N_EVEN, H_B, HS_B), 0.1),
        'rwkv_ln_w': 1.0 + nrm((N_EVEN, WB), 0.02),
        'rwkv_ln_b': nrm((N_EVEN, WB), 0.02),
        'w_in_odd': nrm((N_ODD, d, P_C + P_D), d ** -0.5),
        'w_out_odd': nrm((N_ODD, WC + WD, d), (WC + WD) ** -0.5),
        'mlstm_i_bias': nrm((N_ODD, 2, H_C), 0.1),
        'mlstm_f_bias': jnp.linspace(3.0, 6.0, H_C)[None, None, :] + nrm((N_ODD, 2, H_C), 0.1),
        'mlstm_norm_w': 1.0 + nrm((N_ODD, WC), 0.02),
        'attn_qk_norm': 1.0 + nrm((N_ODD, 2, HD_D), 0.02),
        'final_norm': 1.0 + nrm((d,), 0.02),
    }


def reference(x_prompt, x_sample, state_ret, state_rwkv, state_mlstm_c, state_mlstm_n, state_mlstm_m,
              cache_k, cache_v, c, c_ctx, ada_w, ada_b, norm_g, ffn_w1, ffn_w2, w_in_even, w_out_even,
              ret_log_decay, ret_gn_w, rwkv_mu, rwkv_w0, rwkv_w2, rwkv_a0, rwkv_a2, rwkv_g2, rwkv_k_k,
              rwkv_k_a, rwkv_r_k, rwkv_ln_w, rwkv_ln_b, w_in_odd, w_out_odd, mlstm_i_bias, mlstm_f_bias,
              mlstm_norm_w, attn_qk_norm, final_norm):
    f32 = jnp.float32

    def layer(x, cond, l, latent, st):
        sh1, sc1, gt1, sh2, sc2, gt2, sh3, sc3, gt3 = modulation(cond, ada_w[l], ada_b[l])
        h = rms_norm(x, norm_g[l, 0]) * (1 + sc1) + sh1
        x = x + 0.5 * gt1 * swiglu(h, ffn_w1[l, 0], ffn_w2[l, 0])
        h = rms_norm(x, norm_g[l, 1]) * (1 + sc2) + sh2
        if l % 2 == 0:
            e = l // 2
            p = h @ w_in_even[e]
            out_a, s_ret = retention_mixer(p[..., :P_A], ret_log_decay[e], ret_gn_w[e], st[0], latent)
            out_b, s_rwkv = rwkv_mixer(p[..., P_A:], rwkv_mu[e], rwkv_w0[e], rwkv_w2[e], rwkv_a0[e],
                                       rwkv_a2[e], rwkv_g2[e], rwkv_k_k[e], rwkv_k_a[e], rwkv_r_k[e],
                                       rwkv_ln_w[e], rwkv_ln_b[e], st[1])
            mix = jnp.concatenate([out_a, out_b], axis=-1) @ w_out_even[e]
            new = (s_ret, s_rwkv)
        else:
            o = l // 2
            p = h @ w_in_odd[o]
            out_c, s_c, s_n, s_m = mlstm_mixer(p[..., :P_C], mlstm_i_bias[o], mlstm_f_bias[o],
                                               mlstm_norm_w[o], st[0], st[1], st[2])
            out_d, k_ctx, v_ctx = attention_mixer(p[..., P_C:], attn_qk_norm[o], st[3], st[4], latent)
            mix = jnp.concatenate([out_c, out_d], axis=-1) @ w_out_odd[o]
            new = (s_c, s_n, s_m, k_ctx, v_ctx)
        x = x + gt2 * mix
        h = rms_norm(x, norm_g[l, 2]) * (1 + sc3) + sh3
        x = x + 0.5 * gt3 * swiglu(h, ffn_w1[l, 1], ffn_w2[l, 1])
        return x, new

    bp = x_prompt.shape[0]
    cond_ctx = c_ctx[None, :]
    xp, xs = x_prompt, x_sample
    ret_l, rwkv_l, mc_l, mn_l, mm_l, k_l, v_l = [], [], [], [], [], [], []
    for l in range(DEPTH):
        if l % 2 == 0:
            e = l // 2
            zeros = (jnp.zeros((bp, 2, H_A, DK_A, DV_A), f32), jnp.zeros((bp, 2, H_B, HS_B, HS_B), f32))
            xp, (s_ret, s_rwkv) = layer(xp, cond_ctx, l, False, zeros)
            ret_l.append(s_ret)
            rwkv_l.append(s_rwkv)
            xs, _ = layer(xs, c, l, True, (state_ret[:, e], state_rwkv[:, e]))
        else:
            o = l // 2
            zeros = (jnp.zeros((bp, 2, H_C, DH_C, DH_C), f32), jnp.zeros((bp, 2, H_C, DH_C), f32),
                     jnp.zeros((bp, 2, H_C), f32), None, None)
            xp, (s_c, s_n, s_m, k_ctx, v_ctx) = layer(xp, cond_ctx, l, False, zeros)
            mc_l.append(s_c)
            mn_l.append(s_n)
            mm_l.append(s_m)
            k_l.append(k_ctx)
            v_l.append(v_ctx)
            xs, _ = layer(xs, c, l, True, (state_mlstm_c[:, o], state_mlstm_n[:, o], state_mlstm_m[:, o],
                                           cache_k[:, o], cache_v[:, o]))
    y_prompt = rms_norm(xp, final_norm)
    y_sample = rms_norm(xs, final_norm)
    new_state_ret = jnp.stack(ret_l, axis=1)
    new_state_rwkv = jnp.stack(rwkv_l, axis=1)
    new_state_mlstm_c = jnp.stack(mc_l, axis=1)
    new_state_mlstm_n = jnp.stack(mn_l, axis=1)
    new_state_mlstm_m = jnp.stack(mm_l, axis=1)
    new_cache_k = jnp.stack(k_l, axis=1)
    new_cache_v = jnp.stack(v_l, axis=1)
    return (y_prompt, y_sample, new_state_ret, new_state_rwkv, new_state_mlstm_c, new_state_mlstm_n,
            new_state_mlstm_m, new_cache_k, new_cache_v)
```

```python
import functools

import jax
import jax.numpy as jnp
from jax import lax
from jax.experimental import pallas as pl
from jax.experimental.pallas import tpu as pltpu

F32 = jnp.float32
BF16 = jnp.bfloat16

GRID_W = 64
CHUNK = 128
ROPE_BASE = 10000.0
RMS_EPS = 1e-6
N_MOD = 9
LANES = 128
SUBLANES = 8
VMEM_LIMIT = 56 * 1024 * 1024


def _cparams(n_axes):
    return pltpu.CompilerParams(dimension_semantics=("arbitrary",) * n_axes, vmem_limit_bytes=VMEM_LIMIT)


def _bdot(a, b):
    return jnp.dot(a.astype(BF16), b.astype(BF16), preferred_element_type=F32)


def _bdot_nt(a, b):
    return lax.dot_general(a.astype(BF16), b.astype(BF16), (((1,), (1,)), ((), ())),
                           preferred_element_type=F32)


def _bdot_tn(a, b):
    return lax.dot_general(a.astype(BF16), b.astype(BF16), (((0,), (0,)), ((), ())),
                           preferred_element_type=F32)


def _split3(x):
    hi = x.astype(BF16)
    r1 = x - hi.astype(F32)
    mid = r1.astype(BF16)
    lo = (r1 - mid.astype(F32)).astype(BF16)
    return hi, mid, lo


def _dot_exact_rhs(x, e):
    e = e.astype(BF16)
    hi, mid, lo = _split3(x)
    return (jnp.dot(hi, e, preferred_element_type=F32) + jnp.dot(mid, e, preferred_element_type=F32)
            + jnp.dot(lo, e, preferred_element_type=F32))


def _dot_exact_lhs(e, x):
    e = e.astype(BF16)
    hi, mid, lo = _split3(x)
    return (jnp.dot(e, hi, preferred_element_type=F32) + jnp.dot(e, mid, preferred_element_type=F32)
            + jnp.dot(e, lo, preferred_element_type=F32))


def _sigmoid(x):
    return 1.0 / (1.0 + jnp.exp(-x))


def _silu(x):
    return x * _sigmoid(x)


def _softplus(x):
    return jnp.maximum(x, 0.0) + jnp.log(1.0 + jnp.exp(-jnp.abs(x)))


def _rms(x, g):
    return x * lax.rsqrt(jnp.mean(x * x, axis=-1, keepdims=True) + RMS_EPS) * g


def _seg_ones(n, seg):
    r = lax.broadcasted_iota(jnp.int32, (n, n), 0) // seg
    c = lax.broadcasted_iota(jnp.int32, (n, n), 1) // seg
    return (r == c).astype(BF16)


def _full_spec(arr):
    nd = arr.ndim
    return pl.BlockSpec(arr.shape, lambda *_: (0,) * nd, pipeline_mode=pl.Buffered(1))


def _tok_spec(tm, w):
    return pl.BlockSpec((None, tm, w), lambda b, i: (b, i, 0))


def _row(a):
    return a.reshape(1, -1)


def _mod_kernel(c_ref, w_ref, b_ref, o_ref):
    o_ref[...] = _bdot(_silu(c_ref[...]), w_ref[...]) + b_ref[...]


def modulation_all(cond, ada_w, ada_b, tn=1024):
    depth, d, n = ada_w.shape
    rows = cond.shape[0]
    return pl.pallas_call(
        _mod_kernel,
        grid=(depth, n // tn),
        in_specs=[pl.BlockSpec((rows, d), lambda l, j: (0, 0)),
                  pl.BlockSpec((None, d, tn), lambda l, j: (l, 0, j)),
                  pl.BlockSpec((None, 1, tn), lambda l, j: (l, 0, j))],
        out_specs=pl.BlockSpec((None, rows, tn), lambda l, j: (l, 0, j)),
        out_shape=jax.ShapeDtypeStruct((depth, rows, n), F32),
        compiler_params=_cparams(2),
        name="adaln_modulation",
    )(cond, ada_w, ada_b.reshape(depth, 1, n))


def _ff_chunk(d_ff):
    return 1408 if d_ff % 1408 == 0 else d_ff


def _swiglu(h_bf, w1_ref, w2_ref, d_ff):
    fc = _ff_chunk(d_ff)
    acc = None
    for c in range(d_ff // fc):
        gate = jnp.dot(h_bf, w1_ref[:, c * fc:(c + 1) * fc], preferred_element_type=F32)
        up = jnp.dot(h_bf, w1_ref[:, d_ff + c * fc:d_ff + (c + 1) * fc], preferred_element_type=F32)
        a = (_silu(gate) * up).astype(BF16)
        part = jnp.dot(a, w2_ref[c * fc:(c + 1) * fc, :], preferred_element_type=F32)
        acc = part if acc is None else acc + part
    return acc


def _mod_specs(mod, ks):
    bc, _, d = mod.shape
    mod4 = mod.reshape(bc, N_MOD, 1, d)

    def spec(k):
        if bc == 1:
            return pl.BlockSpec((None, None, 1, d), lambda b, i: (0, k, 0, 0))
        return pl.BlockSpec((None, None, 1, d), lambda b, i: (b, k, 0, 0))

    return [mod4] * len(ks), [spec(k) for k in ks]


def _pre_kernel(x_ref, sh1_ref, sc1_ref, gt1_ref, sh2_ref, sc2_ref, g1_ref, g2_ref,
                w1_ref, w2_ref, win_ref, x_out_ref, *p_refs, d_ff, splits):
    x = x_ref[...]
    h = _rms(x, g1_ref[...]) * (1.0 + sc1_ref[...]) + sh1_ref[...]
    x1 = x + 0.5 * gt1_ref[...] * _swiglu(h.astype(BF16), w1_ref, w2_ref, d_ff)
    x_out_ref[...] = x1
    h2 = (_rms(x1, g2_ref[...]) * (1.0 + sc2_ref[...]) + sh2_ref[...]).astype(BF16)
    off = 0
    for ref, wdt in zip(p_refs, splits):
        ref[...] = jnp.dot(h2, win_ref[:, off:off + wdt], preferred_element_type=F32)
        off += wdt


def dense_pre(x, mod, g1, g2, w1, w2, w_in, splits, tm):
    b, l, d = x.shape
    margs, mspecs = _mod_specs(mod, (0, 1, 2, 3, 4))
    params = [_row(g1), _row(g2), w1, w2, w_in]
    outs = pl.pallas_call(
        functools.partial(_pre_kernel, d_ff=w2.shape[0], splits=tuple(splits)),
        grid=(b, l // tm),
        in_specs=[_tok_spec(tm, d)] + mspecs + [_full_spec(a) for a in params],
        out_specs=[_tok_spec(tm, d)] + [_tok_spec(tm, w) for w in splits],
        out_shape=[jax.ShapeDtypeStruct((b, l, d), F32)] + [jax.ShapeDtypeStruct((b, l, w), F32) for w in splits],
        compiler_params=_cparams(2),
        name="dense_pre",
    )(x, *margs, *params)
    return outs[0], outs[1:]


def _head_ln(x, width, eps):
    parts = []
    for h in range(x.shape[-1] // width):
        xh = x[:, h * width:(h + 1) * width]
        mu = jnp.mean(xh, axis=-1, keepdims=True)
        xc = xh - mu
        var = jnp.mean(xc * xc, axis=-1, keepdims=True)
        parts.append(xc * lax.rsqrt(var + eps))
    return jnp.concatenate(parts, axis=-1)


def _seg_ln(x, e, width, eps):
    mu = _dot_exact_rhs(x, e) * (1.0 / width)
    xc = x - mu
    var = _dot_exact_rhs(xc * xc, e) * (1.0 / width)
    return xc * lax.rsqrt(var + eps)


def _post_tail(x, mix, gt2_ref, sh3_ref, sc3_ref, gt3_ref, g3_ref, w1_ref, w2_ref, fg_ref, o_ref, d_ff):
    x2 = x + gt2_ref[...] * mix
    h3 = (_rms(x2, g3_ref[...]) * (1.0 + sc3_ref[...]) + sh3_ref[...]).astype(BF16)
    y = x2 + 0.5 * gt3_ref[...] * _swiglu(h3, w1_ref, w2_ref, d_ff)
    if fg_ref is not None:
        y = _rms(y, fg_ref[...])
    o_ref[...] = y


def _post_even_kernel(x_ref, of_ref, ob_ref, ga_ref, y0_ref, y1_ref, bon_ref, gb_ref,
                      gt2_ref, sh3_ref, sc3_ref, gt3_ref,
                      gnw_ref, lnw_ref, lnb_ref, seg_ref, g3_ref, wout_ref, w1_ref, w2_ref,
                      *rest, d_ff, dv, hs, final):
    fg_ref = rest[0] if final else None
    o_ref = rest[-1]
    wa = of_ref.shape[-1]
    o = _head_ln(of_ref[...] + ob_ref[...], dv, 1e-5) * gnw_ref[...]
    out_a = _silu(ga_ref[...]) * o
    y = _seg_ln(y0_ref[...] + y1_ref[...], seg_ref[...], hs, 64e-5) * lnw_ref[...] + lnb_ref[...]
    out_b = (y + bon_ref[...]) * gb_ref[...]
    mix = _bdot(out_a, wout_ref[0:wa, :]) + _bdot(out_b, wout_ref[wa:, :])
    _post_tail(x_ref[...], mix, gt2_ref, sh3_ref, sc3_ref, gt3_ref, g3_ref, w1_ref, w2_ref, fg_ref, o_ref, d_ff)


def _post_odd_kernel(x_ref, hf_ref, hb_ref, og_ref, od_ref,
                     gt2_ref, sh3_ref, sc3_ref, gt3_ref,
                     nw_ref, g3_ref, wout_ref, w1_ref, w2_ref, *rest, d_ff, dh, final):
    fg_ref = rest[0] if final else None
    o_ref = rest[-1]
    wc = hf_ref.shape[-1]
    h = _head_ln(hf_ref[...] + hb_ref[...], dh, 1e-5) * nw_ref[...]
    out_c = _sigmoid(og_ref[...]) * h
    mix = _bdot(out_c, wout_ref[0:wc, :]) + _bdot(od_ref[...], wout_ref[wc:, :])
    _post_tail(x_ref[...], mix, gt2_ref, sh3_ref, sc3_ref, gt3_ref, g3_ref, w1_ref, w2_ref, fg_ref, o_ref, d_ff)


def _dense_post(kernel_fn, name, x, toks, mod, params, final_g, tm, **kw):
    b, l, d = x.shape
    margs, mspecs = _mod_specs(mod, (5, 6, 7, 8))
    final = final_g is not None
    params = list(params) + ([_row(final_g)] if final else [])
    return pl.pallas_call(
        functools.partial(kernel_fn, final=final, **kw),
        grid=(b, l // tm),
        in_specs=([_tok_spec(tm, d)] + [_tok_spec(tm, a.shape[-1]) for a in toks] + mspecs
                  + [_full_spec(a) for a in params]),
        out_specs=_tok_spec(tm, d),
        out_shape=jax.ShapeDtypeStruct((b, l, d), F32),
        compiler_params=_cparams(2),
        name=name,
    )(x, *toks, *margs, *params)


def _rope_tables(n, d, reps):
    rows = n // GRID_W
    row = jnp.repeat(jnp.arange(rows), GRID_W).astype(F32)
    col = (jnp.arange(rows * GRID_W) % GRID_W).astype(F32)
    nf = d // 4
    inv = ROPE_BASE ** (-jnp.arange(nf, dtype=F32) / nf)
    ang_r = row[:, None] * inv[None, :]
    ang_c = col[:, None] * inv[None, :]
    cos = jnp.concatenate([jnp.cos(ang_r), jnp.cos(ang_r), jnp.cos(ang_c), jnp.cos(ang_c)], axis=-1)
    sin = jnp.concatenate([-jnp.sin(ang_r), jnp.sin(ang_r), -jnp.sin(ang_c), jnp.sin(ang_c)], axis=-1)
    return jnp.tile(cos, (1, reps)), jnp.tile(sin, (1, reps))


def _rope(x, cos, sin, nf):
    w = x.shape[-1]
    lane = lax.broadcasted_iota(jnp.int32, x.shape, 1)
    first = (lane % (2 * nf)) < nf
    partner = jnp.where(first, pltpu.roll(x, w - nf, axis=1), pltpu.roll(x, nf, axis=1))
    return x * cos + partner * sin


def _ret_kernel(ld_ref, *refs, h, dk, dv, c, nc, latent, emit_state):
    it = iter(refs)
    qkf_ref, vf_ref, qkb_ref, vb_ref = next(it), next(it), next(it), next(it)
    if latent:
        cosf_ref, sinf_ref, cosb_ref, sinb_ref, s0_ref = next(it), next(it), next(it), next(it), next(it)
    of_ref, ob_ref = next(it), next(it)
    sfin_ref = next(it) if emit_state else None
    z_scr = next(it)

    ci = pl.program_id(1)
    hk = h * dk

    @pl.when(ci == 0)
    def _():
        z_scr[...] = jnp.zeros_like(z_scr)
        if latent:
            for d in range(2):
                for hh in range(h):
                    z_scr[d, hh, hh * dk:(hh + 1) * dk, :] = s0_ref[d, hh]

    ii = lax.broadcasted_iota(jnp.int32, (c, c), 0)
    jj = lax.broadcasted_iota(jnp.int32, (c, c), 1)
    icol = lax.broadcasted_iota(jnp.int32, (c, 1), 0).astype(F32)
    lane = lax.broadcasted_iota(jnp.int32, (1, hk), 1)

    for d, (qk_ref, v_ref, o_ref) in enumerate(((qkf_ref, vf_ref, of_ref), (qkb_ref, vb_ref, ob_ref))):
        qk = qk_ref[...]
        if latent:
            cos_ref, sin_ref = (cosf_ref, sinf_ref) if d == 0 else (cosb_ref, sinb_ref)
            qk = _rope(qk, cos_ref[...], sin_ref[...], dk // 4)
        q = qk[:, :hk]
        k = qk[:, hk:] * (dk ** -0.5)
        v = v_ref[...]
        diff = (ii - jj) if d == 0 else (jj - ii)
        causal = diff >= 0
        dpos = jnp.where(causal, diff, 0).astype(F32)
        qe = (icol + 1.0) if d == 0 else (c - icol)
        ke = (c - 1.0 - icol) if d == 0 else icol
        outs = []
        for hh in range(h):
            lg = ld_ref[d, hh]
            msk = (lane // dk) == hh
            qh = jnp.where(msk, q, 0.0)
            kh = jnp.where(msk, k, 0.0)
            vh = v[:, hh * dv:(hh + 1) * dv]
            d_intra = jnp.where(causal, jnp.exp(lg * dpos), 0.0)
            att = _bdot_nt(qh, kh) * d_intra
            z = z_scr[d, hh]
            o = _bdot(att, vh) + _bdot(qh * jnp.exp(lg * qe), z)
            c_dec = jnp.exp(lg * jnp.full((1, 1), c, F32))
            z_scr[d, hh] = z * c_dec + _bdot_tn(kh * jnp.exp(lg * ke), vh)
            outs.append(o)
        o_ref[...] = jnp.concatenate(outs, axis=-1)

    if emit_state:
        @pl.when(ci == nc - 1)
        def _():
            for d in range(2):
                for hh in range(h):
                    sfin_ref[d, hh] = z_scr[d, hh, hh * dk:(hh + 1) * dk, :]


def retention_scan(qk, v, log_decay, s0, h, dk, dv, latent, emit_state):
    b, l, _ = qk.shape
    c = CHUNK
    nc = l // c
    hk = h * dk
    fwd = lambda w: pl.BlockSpec((None, c, w), lambda bb, ci: (bb, ci, 0))
    bwd = lambda w: pl.BlockSpec((None, c, w), lambda bb, ci: (bb, nc - 1 - ci, 0))
    args = [log_decay, qk, v, qk, v]
    specs = [pl.BlockSpec(memory_space=pltpu.SMEM), fwd(2 * hk), fwd(h * dv), bwd(2 * hk), bwd(h * dv)]
    if latent:
        cos, sin = _rope_tables(l, dk, 2 * h)
        args += [cos, sin, cos, sin, s0]
        tf = pl.BlockSpec((c, 2 * hk), lambda bb, ci: (ci, 0))
        tb = pl.BlockSpec((c, 2 * hk), lambda bb, ci: (nc - 1 - ci, 0))
        specs += [tf, tf, tb, tb, pl.BlockSpec((None, 2, h, dk, dv), lambda bb, ci: (bb, 0, 0, 0, 0))]
    out_shape = [jax.ShapeDtypeStruct((b, l, h * dv), F32)] * 2
    out_specs = [fwd(h * dv), bwd(h * dv)]
    if emit_state:
        out_shape.append(jax.ShapeDtypeStruct((b, 2, h, dk, dv), F32))
        out_specs.append(pl.BlockSpec((None, 2, h, dk, dv), lambda bb, ci: (bb, 0, 0, 0, 0)))
    return pl.pallas_call(
        functools.partial(_ret_kernel, h=h, dk=dk, dv=dv, c=c, nc=nc, latent=latent, emit_state=emit_state),
        grid=(b, nc),
        in_specs=specs,
        out_specs=out_specs,
        out_shape=out_shape,
        scratch_shapes=[pltpu.VMEM((2, h, hk, dv), F32)],
        compiler_params=_cparams(2),
        name="retention_scan",
    )(*args)


def _rwkv_prep_kernel(p_ref, prev_ref, next_ref, mu_ref, w0_ref, a0_ref, w2_ref, a2_ref, g2_ref,
                      kk_ref, ka_ref, rk_ref, seg_ref,
                      r_out, v_out, nkk_out, dec0_out, dec1_out, kd0_out, kd1_out, b0_out, b1_out,
                      g_out, bon_out, *, tm, nt, wb):
    i = pl.program_id(1)
    p = p_ref[...]
    row = lax.broadcasted_iota(jnp.int32, p.shape, 0)
    prow = jnp.where(i == 0, 0.0, prev_ref[SUBLANES - 1:SUBLANES, :])
    nrow = jnp.where(i == nt - 1, 0.0, next_ref[0:1, :])
    prev = jnp.where(row == 0, prow, pltpu.roll(p, 1, axis=0))
    nxt = jnp.where(row == tm - 1, nrow, pltpu.roll(p, tm - 1, axis=0))
    ps = p + mu_ref[...] * (0.5 * (prev + nxt) - p)
    r = ps[:, 0:wb]
    k = ps[:, wb:2 * wb]
    v = ps[:, 2 * wb:3 * wb]
    lora = ps[:, 3 * wb:]
    tl = jnp.tanh(lora)
    seg = seg_ref[...]
    kk = k * kk_ref[...]
    nrm = jnp.sqrt(_dot_exact_rhs(kk * kk, seg))
    kk = kk / jnp.maximum(nrm, 1e-12)
    r_out[...] = r
    v_out[...] = v
    nkk_out[...] = -kk
    g_out[...] = _bdot(_sigmoid(lora), g2_ref[...])
    bon_out[...] = _dot_exact_rhs(r * k * rk_ref[...], seg) * v
    for d, (dec_out, kd_out, b_out) in enumerate(((dec0_out, kd0_out, b0_out), (dec1_out, kd1_out, b1_out))):
        w_log = -_softplus(-(w0_ref[d:d + 1, :] + _bdot(tl, w2_ref[d]))) - 0.5
        dec_out[...] = jnp.exp(-jnp.exp(w_log))
        a = _sigmoid(a0_ref[d:d + 1, :] + _bdot(lora, a2_ref[d]))
        kd_out[...] = k * (1.0 + (a - 1.0) * ka_ref[...])
        b_out[...] = kk * a


def rwkv_prep(p, mu, w0, w2p, a0, a2p, g2p, k_k, k_a, r_k, seg, wb, tm):
    b, l, pw = p.shape
    nt = l // tm
    r8 = tm // SUBLANES
    nb8 = l // SUBLANES
    params = [_row(mu), w0, a0, w2p, a2p, g2p, _row(k_k), _row(k_a), _row(r_k), seg]
    return pl.pallas_call(
        functools.partial(_rwkv_prep_kernel, tm=tm, nt=nt, wb=wb),
        grid=(b, nt),
        in_specs=[_tok_spec(tm, pw),
                  pl.BlockSpec((None, SUBLANES, pw), lambda bb, i: (bb, jnp.maximum(i * r8 - 1, 0), 0)),
                  pl.BlockSpec((None, SUBLANES, pw), lambda bb, i: (bb, jnp.minimum((i + 1) * r8, nb8 - 1), 0))]
                 + [_full_spec(a) for a in params],
        out_specs=[_tok_spec(tm, wb)] * 11,
        out_shape=[jax.ShapeDtypeStruct((b, l, wb), F32)] * 11,
        compiler_params=_cparams(2),
        name="rwkv_prep",
    )(p, p, p, *params)


def _rwkv_scan_kernel(*refs, n, tt, nt, has_state, emit_state):
    it = iter(refs)
    r_ref, w_ref, k_ref, v_ref, an_ref, b_ref = (next(it) for _ in range(6))
    if has_state:
        a0_ref, s0_ref = next(it), next(it)
    y_ref = next(it)
    sfin_ref = next(it) if emit_state else None
    s_scr, sa_scr = next(it), next(it)
    ti = pl.program_id(1)

    @pl.when(ti == 0)
    def _():
        if has_state:
            acc = jnp.zeros(sa_scr.shape, F32)
            for j in range(n):
                sj = s0_ref[j]
                s_scr[j] = sj
                acc = acc + sj * a0_ref[j:j + 1, :]
            sa_scr[...] = acc
        else:
            s_scr[...] = jnp.zeros_like(s_scr)
            sa_scr[...] = jnp.zeros_like(sa_scr)

    def step(t, sa):
        vt = v_ref[t]
        y = jnp.zeros_like(sa)
        san = jnp.zeros_like(sa)
        for j in range(n):
            new = (s_scr[j] * w_ref[t, j:j + 1, :] + sa * b_ref[t, j:j + 1, :]) + vt * k_ref[t, j:j + 1, :]
            s_scr[j] = new
            y = y + new * r_ref[t, j:j + 1, :]
            san = san + new * an_ref[t, j:j + 1, :]
        y_ref[t] = y
        return san

    sa_scr[...] = lax.fori_loop(0, tt, step, sa_scr[...])

    if emit_state:
        @pl.when(ti == nt - 1)
        def _():
            sfin_ref[...] = s_scr[...]


def rwkv_scan(r, w, k, v, an, bb, a0, s0, emit_state, tt=32):
    ng, l, n, lanes = r.shape
    nt = l // tt
    has_state = s0 is not None
    seq = pl.BlockSpec((None, tt, n, lanes), lambda g, ti: (g, ti, 0, 0))
    st = pl.BlockSpec((None, n, n, lanes), lambda g, ti: (g, 0, 0, 0))
    args = [r, w, k, v, an, bb]
    specs = [seq] * 6
    if has_state:
        args += [a0, s0]
        specs += [pl.BlockSpec((None, n, lanes), lambda g, ti: (g, 0, 0)), st]
    out_shape = [jax.ShapeDtypeStruct((ng, l, n, lanes), F32)]
    out_specs = [seq]
    if emit_state:
        out_shape.append(jax.ShapeDtypeStruct((ng, n, n, lanes), F32))
        out_specs.append(st)
    return pl.pallas_call(
        functools.partial(_rwkv_scan_kernel, n=n, tt=tt, nt=nt, has_state=has_state, emit_state=emit_state),
        grid=(ng, nt),
        in_specs=specs,
        out_specs=out_specs,
        out_shape=out_shape,
        scratch_shapes=[pltpu.VMEM((n, n, lanes), F32), pltpu.VMEM((n, lanes), F32)],
        compiler_params=_cparams(2),
        name="rwkv_scan",
    )(*args)


def _to_chains(x0, x1, h):
    b, l, w = x0.shape
    n = w // h

    def one(x):
        return x.reshape(b, l, h, n).transpose(1, 3, 0, 2).reshape(l, n, b * h)

    cat = jnp.concatenate([one(x0), one(jnp.flip(x1, axis=1))], axis=-1)
    ng = (2 * b * h) // LANES
    return cat.reshape(l, n, ng, LANES).transpose(2, 0, 1, 3)


def _from_chains(y, b, h):
    ng, l, n, _ = y.shape
    cat = y.transpose(1, 2, 0, 3).reshape(l, n, 2, b, h)
    both = cat.transpose(2, 3, 0, 4, 1).reshape(2, b, l, h * n)
    return both[0], jnp.flip(both[1], axis=1)


def _state_to_chains(s):
    b, _, h, n, _ = s.shape
    ng = (2 * b * h) // LANES
    return s.transpose(4, 3, 1, 0, 2).reshape(n, n, ng, LANES).transpose(2, 0, 1, 3)


def _state_from_chains(s, b, h):
    ng, n, _, _ = s.shape
    return s.transpose(1, 2, 0, 3).reshape(n, n, 2, b, h).transpose(3, 2, 4, 1, 0)


def _mlstm_kernel(*refs, h, dh, c, nc, has_state, emit_state):
    it = iter(refs)
    qkvf_ref, gf_ref, qkvb_ref, gb_ref, bias_ref = (next(it) for _ in range(5))
    if has_state:
        c0_ref, n0_ref, m0_ref = next(it), next(it), next(it)
    hf_ref, hb_ref = next(it), next(it)
    if emit_state:
        cfin_ref, nfin_ref, mfin_ref = next(it), next(it), next(it)
    cm_scr, nv_scr, m_scr = next(it), next(it), next(it)
    ci = pl.program_id(1)

    @pl.when(ci == 0)
    def _():
        if has_state:
            for d in range(2):
                for hh in range(h):
                    cm_scr[d * h + hh] = c0_ref[d, hh]
            nv_scr[...] = n0_ref[...]
            m_scr[...] = m0_ref[...]
        else:
            cm_scr[...] = jnp.zeros_like(cm_scr)
            nv_scr[...] = jnp.zeros_like(nv_scr)
            m_scr[...] = jnp.zeros_like(m_scr)

    ii = lax.broadcasted_iota(jnp.int32, (c, c), 0)
    jj = lax.broadcasted_iota(jnp.int32, (c, c), 1)
    gl_lane = lax.broadcasted_iota(jnp.int32, (c, LANES), 1)
    is_fg = (gl_lane >= 2 * h) & (gl_lane < 4 * h)

    for d, (qkv_ref, g_ref, o_ref) in enumerate(((qkvf_ref, gf_ref, hf_ref), (qkvb_ref, gb_ref, hb_ref))):
        causal = (ii >= jj) if d == 0 else (ii <= jj)
        tri = causal.astype(BF16)
        gx = g_ref[...] + bias_ref[...]
        gl = jnp.where(is_fg, -_softplus(-gx), gx)
        bcum = _dot_exact_lhs(tri, gl)
        glt = gl.T
        bt = bcum.T
        qkv = qkv_ref[...]
        end = c - 1 if d == 0 else 0
        outs = []
        for hh in range(h):
            ig_c = d * h + hh
            fg_c = 2 * h + d * h + hh
            sr = d * h + hh
            q = qkv[:, hh * dh:(hh + 1) * dh]
            k = qkv[:, (h + hh) * dh:(h + hh + 1) * dh] * (dh ** -0.5)
            v = qkv[:, (2 * h + hh) * dh:(2 * h + hh + 1) * dh]
            b_col = bcum[:, fg_c:fg_c + 1]
            b_row = bt[fg_c:fg_c + 1, :]
            ig_row = glt[ig_c:ig_c + 1, :]
            ig_col = gl[:, ig_c:ig_c + 1]
            m = m_scr[sr:sr + 1, 0:1]
            nv = nv_scr[sr:sr + 1, :]
            cm = cm_scr[sr]
            a_col = b_col + m
            dlog = jnp.where(causal, b_col - b_row + ig_row, -jnp.inf)
            mt = jnp.maximum(a_col, jnp.max(dlog, axis=-1, keepdims=True))
            wgt = jnp.exp(dlog - mt)
            s = _bdot_nt(q, k) * wgt
            inter = jnp.exp(a_col - mt)
            num = _bdot(s, v) + inter * _bdot(q, cm)
            den = jnp.sum(s, axis=-1, keepdims=True) + inter * jnp.sum(q * nv, axis=-1, keepdims=True)
            outs.append(num / jnp.maximum(jnp.abs(den), jnp.exp(-mt)))
            bl = bt[fg_c:fg_c + 1, end:end + 1]
            wl = bl - b_col + ig_col
            m_new = jnp.maximum(bl + m, jnp.max(wl, axis=0, keepdims=True))
            wk = jnp.exp(wl - m_new) * k
            dec = jnp.exp(bl + m - m_new)
            cm_scr[sr] = dec * cm + _bdot_tn(wk, v)
            nv_scr[sr:sr + 1, :] = dec * nv + jnp.sum(wk, axis=0, keepdims=True)
            m_scr[sr:sr + 1, :] = jnp.broadcast_to(m_new, (1, LANES))
        o_ref[...] = jnp.concatenate(outs, axis=-1)

    if emit_state:
        @pl.when(ci == nc - 1)
        def _():
            for d in range(2):
                for hh in range(h):
                    cfin_ref[d, hh] = cm_scr[d * h + hh]
            nfin_ref[...] = nv_scr[...]
            mfin_ref[...] = m_scr[...]


def mlstm_scan(qkv, gates, bias_row, c0, n0, m0, h, dh, emit_state):
    b, l, _ = qkv.shape
    c = CHUNK
    nc = l // c
    has_state = c0 is not None
    w = qkv.shape[-1]
    fwd = lambda ww: pl.BlockSpec((None, c, ww), lambda bb, ci: (bb, ci, 0))
    bwd = lambda ww: pl.BlockSpec((None, c, ww), lambda bb, ci: (bb, nc - 1 - ci, 0))
    cspec = pl.BlockSpec((None, 2, h, dh, dh), lambda bb, ci: (bb, 0, 0, 0, 0))
    rspec = pl.BlockSpec((None, 2 * h, LANES), lambda bb, ci: (bb, 0, 0))
    args = [qkv, gates, qkv, gates, bias_row]
    specs = [fwd(w), fwd(LANES), bwd(w), bwd(LANES), _full_spec(bias_row)]
    if has_state:
        args += [c0, n0.reshape(b, 2 * h, dh), jnp.broadcast_to(m0.reshape(b, 2 * h, 1), (b, 2 * h, LANES))]
        specs += [cspec, rspec, rspec]
    out_shape = [jax.ShapeDtypeStruct((b, l, h * dh), F32)] * 2
    out_specs = [fwd(h * dh), bwd(h * dh)]
    if emit_state:
        out_shape += [jax.ShapeDtypeStruct((b, 2, h, dh, dh), F32),
                      jax.ShapeDtypeStruct((b, 2 * h, dh), F32), jax.ShapeDtypeStruct((b, 2 * h, LANES), F32)]
        out_specs += [cspec, rspec, rspec]
    return pl.pallas_call(
        functools.partial(_mlstm_kernel, h=h, dh=dh, c=c, nc=nc, has_state=has_state, emit_state=emit_state),
        grid=(b, nc),
        in_specs=specs,
        out_specs=out_specs,
        out_shape=out_shape,
        scratch_shapes=[pltpu.VMEM((2 * h, dh, dh), F32), pltpu.VMEM((2 * h, dh), F32),
                        pltpu.VMEM((2 * h, LANES), F32)],
        compiler_params=_cparams(2),
        name="mlstm_scan",
    )(*args)


def _attn_prep_kernel(*refs, hq, hkv, hd, latent):
    it = iter(refs)
    q_ref, kv_ref, gq_ref, gk_ref, segq_ref, segk_ref = (next(it) for _ in range(6))
    if latent:
        cq_ref, sq_ref, ck_ref, sk_ref = (next(it) for _ in range(4))
    qo_ref, ko_ref, vo_ref = next(it), next(it), next(it)
    q = q_ref[...]
    kv = kv_ref[...]
    kw = hkv * hd
    k = kv[:, :kw]
    v = kv[:, kw:]
    qn = q * lax.rsqrt(_dot_exact_rhs(q * q, segq_ref[...]) * (1.0 / hd) + RMS_EPS) * gq_ref[...]
    kn = k * lax.rsqrt(_dot_exact_rhs(k * k, segk_ref[...]) * (1.0 / hd) + RMS_EPS) * gk_ref[...]
    if latent:
        qn = _rope(qn, cq_ref[...], sq_ref[...], hd // 4)
        kn = _rope(kn, ck_ref[...], sk_ref[...], hd // 4)
    qo_ref[...] = qn * (hd ** -0.5)
    for j in range(hkv):
        ko_ref[j] = kn[:, j * hd:(j + 1) * hd]
        vo_ref[j] = v[:, j * hd:(j + 1) * hd]


def attn_prep(q, kv, qk_gain, hq, hkv, hd, latent, tm):
    b, l, wq = q.shape
    wkv = kv.shape[-1]
    kw = hkv * hd
    params = [jnp.tile(qk_gain[0], hq).reshape(1, wq), jnp.tile(qk_gain[1], hkv).reshape(1, kw),
              _seg_ones(wq, hd), _seg_ones(kw, hd)]
    args = [q, kv] + params
    specs = [_tok_spec(tm, wq), _tok_spec(tm, wkv)] + [_full_spec(a) for a in params]
    if latent:
        cq, sq = _rope_tables(l, hd, hq)
        ck, sk = _rope_tables(l, hd, hkv)
        args += [cq, sq, ck, sk]
        specs += [pl.BlockSpec((tm, wq), lambda bb, i: (i, 0))] * 2 + [pl.BlockSpec((tm, kw), lambda bb, i: (i, 0))] * 2
    kvspec = pl.BlockSpec((None, hkv, tm, hd), lambda bb, i: (bb, 0, i, 0))
    return pl.pallas_call(
        functools.partial(_attn_prep_kernel, hq=hq, hkv=hkv, hd=hd, latent=latent),
        grid=(b, l // tm),
        in_specs=specs,
        out_specs=[_tok_spec(tm, wq), kvspec, kvspec],
        out_shape=[jax.ShapeDtypeStruct((b, l, wq), F32)] + [jax.ShapeDtypeStruct((b, hkv, l, hd), F32)] * 2,
        compiler_params=_cparams(2),
        name="attn_prep",
    )(*args)


def _attn_kernel(*refs, g, hd, tq, has_ctx):
    it = iter(refs)
    q_ref, k_ref, v_ref = next(it), next(it), next(it)
    if has_ctx:
        ck_ref, cv_ref = next(it), next(it)
    o_ref = next(it)
    q = q_ref[...]
    qs = jnp.concatenate([q[:, i * hd:(i + 1) * hd] for i in range(g)], axis=0).astype(BF16)
    k = k_ref[...].astype(BF16)
    v = v_ref[...].astype(BF16)
    s2 = _bdot_nt(qs, k)
    m = jnp.max(s2, axis=-1, keepdims=True)
    if has_ctx:
        ck = ck_ref[...].astype(BF16)
        cv = cv_ref[...].astype(BF16)
        s1 = _bdot_nt(qs, ck)
        m = jnp.maximum(m, jnp.max(s1, axis=-1, keepdims=True))
        p1 = jnp.exp(s1 - m)
    p2 = jnp.exp(s2 - m)
    den = jnp.sum(p2, axis=-1, keepdims=True)
    o = _bdot(p2, v)
    if has_ctx:
        den = den + jnp.sum(p1, axis=-1, keepdims=True)
        o = o + _bdot(p1, cv)
    o = o / den
    for i in range(g):
        o_ref[:, i * hd:(i + 1) * hd] = o[i * tq:(i + 1) * tq, :]


def attention(q, k, v, ctx_k, ctx_v, layer_o, hq, hkv, hd, tq):
    b, l, wq = q.shape
    g = hq // hkv
    has_ctx = ctx_k is not None
    qspec = pl.BlockSpec((None, tq, g * hd), lambda bb, j, i: (bb, i, j))
    kvspec = pl.BlockSpec((None, None, l, hd), lambda bb, j, i: (bb, j, 0, 0))
    args = [q, k, v]
    specs = [qspec, kvspec, kvspec]
    if has_ctx:
        s = ctx_k.shape[3]
        cspec = pl.BlockSpec((None, None, None, s, hd), lambda bb, j, i: (bb, layer_o, j, 0, 0))
        args += [ctx_k, ctx_v]
        specs += [cspec, cspec]
    return pl.pallas_call(
        functools.partial(_attn_kernel, g=g, hd=hd, tq=tq, has_ctx=has_ctx),
        grid=(b, hkv, l // tq),
        in_specs=specs,
        out_specs=qspec,
        out_shape=jax.ShapeDtypeStruct((b, l, wq), F32),
        compiler_params=_cparams(3),
        name="gqa_attention",
    )(*args)


def _pad_rows(w, lo, total):
    return jnp.zeros((total, w.shape[-1]), w.dtype).at[lo:lo + w.shape[0]].set(w)


def kernel(x_prompt, x_sample, state_ret, state_rwkv, state_mlstm_c, state_mlstm_n, state_mlstm_m, cache_k, cache_v, c, c_ctx, ada_w, ada_b, norm_g, ffn_w1, ffn_w2, w_in_even, w_out_even, ret_log_decay, ret_gn_w, rwkv_mu, rwkv_w0, rwkv_w2, rwkv_a0, rwkv_a2, rwkv_g2, rwkv_k_k, rwkv_k_a, rwkv_r_k, rwkv_ln_w, rwkv_ln_b, w_in_odd, w_out_odd, mlstm_i_bias, mlstm_f_bias, mlstm_norm_w, attn_qk_norm, final_norm):
    depth = ada_w.shape[0]
    d_model = x_prompt.shape[-1]
    h_a, dk_a, dv_a = state_ret.shape[3:]
    h_b, hs_b = state_rwkv.shape[3:5]
    h_c, dh_c = state_mlstm_c.shape[3:5]
    hkv_d, hd_d = cache_k.shape[2], cache_k.shape[4]
    wa, wb, wc = h_a * dv_a, h_b * hs_b, h_c * dh_c
    wd = w_out_odd.shape[1] - wc
    hq_d = wd // hd_d
    kvw = hkv_d * hd_d
    n_dec = c.shape[0]
    lora_w, lora_a, lora_g = rwkv_w2.shape[2], rwkv_a2.shape[2], rwkv_g2.shape[1]
    lora_tot = lora_w + lora_a + lora_g

    rows = -(-(n_dec + 1) // SUBLANES) * SUBLANES
    cond = jnp.zeros((rows, d_model), F32).at[:n_dec].set(c).at[n_dec].set(c_ctx)
    mod = modulation_all(cond, ada_w, ada_b).reshape(depth, rows, N_MOD, d_model)

    seg_b = _seg_ones(wb, hs_b)
    streams = {
        "prompt": dict(x=x_prompt, latent=False, tm=x_prompt.shape[1], tq=x_prompt.shape[1]),
        "sample": dict(x=x_sample, latent=True, tm=256, tq=128),
    }
    new_states = {}

    for l in range(depth):
        w1a, w1b = ffn_w1[l, 0].astype(BF16), ffn_w1[l, 1].astype(BF16)
        w2a, w2b = ffn_w2[l, 0].astype(BF16), ffn_w2[l, 1].astype(BF16)
        fin = final_norm if l == depth - 1 else None
        if l % 2 == 0:
            e = l // 2
            w_in = w_in_even[e].astype(BF16)
            w_out = w_out_even[e].astype(BF16)
            splits = (2 * h_a * dk_a, wa, wa, 3 * wb + lora_tot)
            w2p = jnp.stack([_pad_rows(rwkv_w2[e, d], 0, lora_tot) for d in range(2)]).astype(BF16)
            a2p = jnp.stack([_pad_rows(rwkv_a2[e, d], lora_w, lora_tot) for d in range(2)]).astype(BF16)
            g2p = _pad_rows(rwkv_g2[e], lora_w + lora_a, lora_tot).astype(BF16)
        else:
            o = l // 2
            wi = w_in_odd[o]
            g0 = 3 * wc
            p_c = g0 + 4 * h_c + wc
            gates_w = jnp.zeros((d_model, LANES), F32).at[:, :4 * h_c].set(wi[:, g0:g0 + 4 * h_c])
            w_in = jnp.concatenate([wi[:, :g0], wi[:, g0 + 4 * h_c:p_c], wi[:, p_c:p_c + wd],
                                    wi[:, p_c + wd:], gates_w], axis=1).astype(BF16)
            w_out = w_out_odd[o].astype(BF16)
            splits = (3 * wc, wc, wd, 2 * kvw, LANES)
            bias_row = jnp.zeros((1, LANES), F32).at[0, :2 * h_c].set(mlstm_i_bias[o].reshape(-1))
            bias_row = bias_row.at[0, 2 * h_c:4 * h_c].set(mlstm_f_bias[o].reshape(-1))

        for name, st in streams.items():
            x, latent, tm = st["x"], st["latent"], st["tm"]
            b, seq, _ = x.shape
            m = mod[l, :n_dec] if latent else mod[l, n_dec:n_dec + 1]
            emit = not latent
            x1, parts = dense_pre(x, m, norm_g[l, 0], norm_g[l, 1], w1a, w2a, w_in, splits, tm)
            if l % 2 == 0:
                qk_a, v_a, g_a, p_b = parts
                ret = retention_scan(qk_a, v_a, ret_log_decay[e], state_ret[:, e] if latent else None,
                                     h_a, dk_a, dv_a, latent, emit)
                (r, v, nkk, dec0, dec1, kd0, kd1, b0, b1, g_b, bonus) = rwkv_prep(
                    p_b, rwkv_mu[e], rwkv_w0[e], w2p, rwkv_a0[e], a2p, g2p, rwkv_k_k[e], rwkv_k_a[e],
                    rwkv_r_k[e].reshape(-1), seg_b, wb, min(tm, 256))
                a_ch = _to_chains(nkk, nkk, h_b)
                an_ch = jnp.concatenate([a_ch[:, 1:], jnp.zeros_like(a_ch[:, :1])], axis=1)
                scan = rwkv_scan(_to_chains(r, r, h_b), _to_chains(dec0, dec1, h_b), _to_chains(kd0, kd1, h_b),
                                 _to_chains(v, v, h_b), an_ch, _to_chains(b0, b1, h_b),
                                 a_ch[:, 0] if latent else None,
                                 _state_to_chains(state_rwkv[:, e]) if latent else None, emit)
                y0, y1 = _from_chains(scan[0], b, h_b)
                if emit:
                    new_states.setdefault("ret", []).append(ret[2])
                    new_states.setdefault("rwkv", []).append(_state_from_chains(scan[1], b, h_b))
                x = _dense_post(_post_even_kernel, "dense_post_even", x1,
                                [ret[0], ret[1], g_a, y0, y1, bonus, g_b], m,
                                [_row(ret_gn_w[e]), _row(rwkv_ln_w[e]), _row(rwkv_ln_b[e]), seg_b,
                                 _row(norm_g[l, 2]), w_out, w1b, w2b],
                                fin, tm, d_ff=w2b.shape[0], dv=dv_a, hs=hs_b)
            else:
                qkv_c, og, q_d, kv_d, gates = parts
                ml = mlstm_scan(qkv_c, gates, bias_row,
                                state_mlstm_c[:, o] if latent else None,
                                state_mlstm_n[:, o] if latent else None,
                                state_mlstm_m[:, o] if latent else None, h_c, dh_c, emit)
                qn, kn, vn = attn_prep(q_d, kv_d, attn_qk_norm[o], hq_d, hkv_d, hd_d, latent, min(tm, 256))
                out_d = attention(qn, kn, vn, cache_k if latent else None, cache_v if latent else None, o,
                                  hq_d, hkv_d, hd_d, st["tq"])
                if emit:
                    new_states.setdefault("mc", []).append(ml[2])
                    new_states.setdefault("mn", []).append(ml[3].reshape(b, 2, h_c, dh_c))
                    new_states.setdefault("mm", []).append(ml[4][:, :, 0].reshape(b, 2, h_c))
                    new_states.setdefault("k", []).append(kn)
                    new_states.setdefault("v", []).append(vn)
                x = _dense_post(_post_odd_kernel, "dense_post_odd", x1, [ml[0], ml[1], og, out_d], m,
                                [_row(mlstm_norm_w[o]), _row(norm_g[l, 2]), w_out, w1b, w2b],
                                fin, tm, d_ff=w2b.shape[0], dh=dh_c)
            st["x"] = x

    stack = lambda key: jnp.stack(new_states[key], axis=1)
    return (streams["prompt"]["x"], streams["sample"]["x"], stack("ret"), stack("rwkv"), stack("mc"),
            stack("mn"), stack("mm"), stack("k"), stack("v"))
```

```python
import functools

import jax
import jax.numpy as jnp
from jax import lax
from jax.experimental import pallas as pl
from jax.experimental.pallas import tpu as pltpu

F32 = jnp.float32
BF16 = jnp.bfloat16

GRID_W = 64
CHUNK = 128
ROPE_BASE = 10000.0
RMS_EPS = 1e-6
N_MOD = 9
LANES = 128
SUBLANES = 8
VMEM_LIMIT = 56 * 1024 * 1024


def _cparams(n_axes):
    return pltpu.CompilerParams(dimension_semantics=("arbitrary",) * n_axes, vmem_limit_bytes=VMEM_LIMIT)


def _bdot(a, b):
    return jnp.dot(a.astype(BF16), b.astype(BF16), preferred_element_type=F32)


def _bdot_nt(a, b):
    return lax.dot_general(a.astype(BF16), b.astype(BF16), (((1,), (1,)), ((), ())),
                           preferred_element_type=F32)


def _bdot_tn(a, b):
    return lax.dot_general(a.astype(BF16), b.astype(BF16), (((0,), (0,)), ((), ())),
                           preferred_element_type=F32)


def _split3(x):
    hi = x.astype(BF16)
    r1 = x - hi.astype(F32)
    mid = r1.astype(BF16)
    lo = (r1 - mid.astype(F32)).astype(BF16)
    return hi, mid, lo


def _dot_exact_rhs(x, e):
    e = e.astype(BF16)
    hi, mid, lo = _split3(x)
    return (jnp.dot(hi, e, preferred_element_type=F32) + jnp.dot(mid, e, preferred_element_type=F32)
            + jnp.dot(lo, e, preferred_element_type=F32))


def _dot_exact_lhs(e, x):
    e = e.astype(BF16)
    hi, mid, lo = _split3(x)
    return (jnp.dot(e, hi, preferred_element_type=F32) + jnp.dot(e, mid, preferred_element_type=F32)
            + jnp.dot(e, lo, preferred_element_type=F32))


def _sigmoid(x):
    return 1.0 / (1.0 + jnp.exp(-x))


def _silu(x):
    return x * _sigmoid(x)


def _softplus(x):
    return jnp.maximum(x, 0.0) + jnp.log(1.0 + jnp.exp(-jnp.abs(x)))


def _rms(x, g):
    return x * lax.rsqrt(jnp.mean(x * x, axis=-1, keepdims=True) + RMS_EPS) * g


def _seg_ones(n, seg):
    r = lax.broadcasted_iota(jnp.int32, (n, n), 0) // seg
    c = lax.broadcasted_iota(jnp.int32, (n, n), 1) // seg
    return (r == c).astype(BF16)


def _full_spec(arr):
    nd = arr.ndim
    return pl.BlockSpec(arr.shape, lambda *_: (0,) * nd, pipeline_mode=pl.Buffered(1))


def _tok_spec(tm, w):
    return pl.BlockSpec((None, tm, w), lambda b, i: (b, i, 0))


def _row(a):
    return a.reshape(1, -1)


def _mod_kernel(c_ref, w_ref, b_ref, o_ref):
    o_ref[...] = _bdot(_silu(c_ref[...]), w_ref[...]) + b_ref[...]


def modulation_all(cond, ada_w, ada_b, tn=1024):
    depth, d, n = ada_w.shape
    rows = cond.shape[0]
    return pl.pallas_call(
        _mod_kernel,
        grid=(depth, n // tn),
        in_specs=[pl.BlockSpec((rows, d), lambda l, j: (0, 0)),
                  pl.BlockSpec((None, d, tn), lambda l, j: (l, 0, j)),
                  pl.BlockSpec((None, 1, tn), lambda l, j: (l, 0, j))],
        out_specs=pl.BlockSpec((None, rows, tn), lambda l, j: (l, 0, j)),
        out_shape=jax.ShapeDtypeStruct((depth, rows, n), F32),
        compiler_params=_cparams(2),
        name="adaln_modulation",
    )(cond, ada_w, ada_b.reshape(depth, 1, n))


def _ff_chunk(d_ff):
    return 1408 if d_ff % 1408 == 0 else d_ff


def _swiglu(h_bf, w1_ref, w2_ref, d_ff):
    fc = _ff_chunk(d_ff)
    acc = None
    for c in range(d_ff // fc):
        gate = jnp.dot(h_bf, w1_ref[:, c * fc:(c + 1) * fc], preferred_element_type=F32)
        up = jnp.dot(h_bf, w1_ref[:, d_ff + c * fc:d_ff + (c + 1) * fc], preferred_element_type=F32)
        a = (_silu(gate) * up).astype(BF16)
        part = jnp.dot(a, w2_ref[c * fc:(c + 1) * fc, :], preferred_element_type=F32)
        acc = part if acc is None else acc + part
    return acc


def _mod_specs(mod, ks):
    bc, _, d = mod.shape
    mod4 = mod.reshape(bc, N_MOD, 1, d)

    def spec(k):
        if bc == 1:
            return pl.BlockSpec((None, None, 1, d), lambda b, i: (0, k, 0, 0))
        return pl.BlockSpec((None, None, 1, d), lambda b, i: (b, k, 0, 0))

    return [mod4] * len(ks), [spec(k) for k in ks]


def _pre_kernel(x_ref, sh1_ref, sc1_ref, gt1_ref, sh2_ref, sc2_ref, g1_ref, g2_ref,
                w1_ref, w2_ref, win_ref, x_out_ref, *p_refs, d_ff, splits):
    x = x_ref[...]
    h = _rms(x, g1_ref[...]) * (1.0 + sc1_ref[...]) + sh1_ref[...]
    x1 = x + 0.5 * gt1_ref[...] * _swiglu(h.astype(BF16), w1_ref, w2_ref, d_ff)
    x_out_ref[...] = x1
    h2 = (_rms(x1, g2_ref[...]) * (1.0 + sc2_ref[...]) + sh2_ref[...]).astype(BF16)
    off = 0
    for ref, wdt in zip(p_refs, splits):
        ref[...] = jnp.dot(h2, win_ref[:, off:off + wdt], preferred_element_type=F32)
        off += wdt


def dense_pre(x, mod, g1, g2, w1, w2, w_in, splits, tm):
    b, l, d = x.shape
    margs, mspecs = _mod_specs(mod, (0, 1, 2, 3, 4))
    params = [_row(g1), _row(g2), w1, w2, w_in]
    outs = pl.pallas_call(
        functools.partial(_pre_kernel, d_ff=w2.shape[0], splits=tuple(splits)),
        grid=(b, l // tm),
        in_specs=[_tok_spec(tm, d)] + mspecs + [_full_spec(a) for a in params],
        out_specs=[_tok_spec(tm, d)] + [_tok_spec(tm, w) for w in splits],
        out_shape=[jax.ShapeDtypeStruct((b, l, d), F32)] + [jax.ShapeDtypeStruct((b, l, w), F32) for w in splits],
        compiler_params=_cparams(2),
        name="dense_pre",
    )(x, *margs, *params)
    return outs[0], outs[1:]


def _head_ln(x, width, eps):
    parts = []
    for h in range(x.shape[-1] // width):
        xh = x[:, h * width:(h + 1) * width]
        mu = jnp.mean(xh, axis=-1, keepdims=True)
        xc = xh - mu
        var = jnp.mean(xc * xc, axis=-1, keepdims=True)
        parts.append(xc * lax.rsqrt(var + eps))
    return jnp.concatenate(parts, axis=-1)


def _seg_ln(x, e, width, eps):
    mu = _dot_exact_rhs(x, e) * (1.0 / width)
    xc = x - mu
    var = _dot_exact_rhs(xc * xc, e) * (1.0 / width)
    return xc * lax.rsqrt(var + eps)


def _post_tail(x, mix, gt2_ref, sh3_ref, sc3_ref, gt3_ref, g3_ref, w1_ref, w2_ref, fg_ref, o_ref, d_ff):
    x2 = x + gt2_ref[...] * mix
    h3 = (_rms(x2, g3_ref[...]) * (1.0 + sc3_ref[...]) + sh3_ref[...]).astype(BF16)
    y = x2 + 0.5 * gt3_ref[...] * _swiglu(h3, w1_ref, w2_ref, d_ff)
    if fg_ref is not None:
        y = _rms(y, fg_ref[...])
    o_ref[...] = y


def _post_even_kernel(x_ref, of_ref, ob_ref, ga_ref, y_ref, bon_ref, gb_ref,
                      gt2_ref, sh3_ref, sc3_ref, gt3_ref,
                      gnw_ref, lnw_ref, lnb_ref, seg_ref, g3_ref, wout_ref, w1_ref, w2_ref,
                      *rest, d_ff, dv, hs, final):
    fg_ref = rest[0] if final else None
    o_ref = rest[-1]
    wa = of_ref.shape[-1]
    o = _head_ln(of_ref[...] + ob_ref[...], dv, 1e-5) * gnw_ref[...]
    out_a = _silu(ga_ref[...]) * o
    y = _seg_ln(y_ref[...], seg_ref[...], hs, 64e-5) * lnw_ref[...] + lnb_ref[...]
    out_b = (y + bon_ref[...]) * gb_ref[...]
    mix = _bdot(out_a, wout_ref[0:wa, :]) + _bdot(out_b, wout_ref[wa:, :])
    _post_tail(x_ref[...], mix, gt2_ref, sh3_ref, sc3_ref, gt3_ref, g3_ref, w1_ref, w2_ref, fg_ref, o_ref, d_ff)


def _post_odd_kernel(x_ref, hf_ref, hb_ref, og_ref, od_ref,
                     gt2_ref, sh3_ref, sc3_ref, gt3_ref,
                     nw_ref, g3_ref, wout_ref, w1_ref, w2_ref, *rest, d_ff, dh, final):
    fg_ref = rest[0] if final else None
    o_ref = rest[-1]
    wc = hf_ref.shape[-1]
    h = _head_ln(hf_ref[...] + hb_ref[...], dh, 1e-5) * nw_ref[...]
    out_c = _sigmoid(og_ref[...]) * h
    mix = _bdot(out_c, wout_ref[0:wc, :]) + _bdot(od_ref[...], wout_ref[wc:, :])
    _post_tail(x_ref[...], mix, gt2_ref, sh3_ref, sc3_ref, gt3_ref, g3_ref, w1_ref, w2_ref, fg_ref, o_ref, d_ff)


def _dense_post(kernel_fn, name, x, toks, mod, params, final_g, tm, **kw):
    b, l, d = x.shape
    margs, mspecs = _mod_specs(mod, (5, 6, 7, 8))
    final = final_g is not None
    params = list(params) + ([_row(final_g)] if final else [])
    return pl.pallas_call(
        functools.partial(kernel_fn, final=final, **kw),
        grid=(b, l // tm),
        in_specs=([_tok_spec(tm, d)] + [_tok_spec(tm, a.shape[-1]) for a in toks] + mspecs
                  + [_full_spec(a) for a in params]),
        out_specs=_tok_spec(tm, d),
        out_shape=jax.ShapeDtypeStruct((b, l, d), F32),
        compiler_params=_cparams(2),
        name=name,
    )(x, *toks, *margs, *params)


def _rope_tables(n, d, reps):
    rows = n // GRID_W
    row = jnp.repeat(jnp.arange(rows), GRID_W).astype(F32)
    col = (jnp.arange(rows * GRID_W) % GRID_W).astype(F32)
    nf = d // 4
    inv = ROPE_BASE ** (-jnp.arange(nf, dtype=F32) / nf)
    ang_r = row[:, None] * inv[None, :]
    ang_c = col[:, None] * inv[None, :]
    cos = jnp.concatenate([jnp.cos(ang_r), jnp.cos(ang_r), jnp.cos(ang_c), jnp.cos(ang_c)], axis=-1)
    sin = jnp.concatenate([-jnp.sin(ang_r), jnp.sin(ang_r), -jnp.sin(ang_c), jnp.sin(ang_c)], axis=-1)
    return jnp.tile(cos, (1, reps)), jnp.tile(sin, (1, reps))


def _rope(x, cos, sin, nf):
    w = x.shape[-1]
    lane = lax.broadcasted_iota(jnp.int32, x.shape, 1)
    first = (lane % (2 * nf)) < nf
    partner = jnp.where(first, pltpu.roll(x, w - nf, axis=1), pltpu.roll(x, nf, axis=1))
    return x * cos + partner * sin


def _ret_kernel(ld_ref, *refs, h, dk, dv, c, nc, latent, emit_state):
    it = iter(refs)
    qkf_ref, vf_ref, qkb_ref, vb_ref = next(it), next(it), next(it), next(it)
    if latent:
        cosf_ref, sinf_ref, cosb_ref, sinb_ref, s0_ref = next(it), next(it), next(it), next(it), next(it)
    of_ref, ob_ref = next(it), next(it)
    sfin_ref = next(it) if emit_state else None
    z_scr = next(it)

    ci = pl.program_id(1)
    hk = h * dk

    @pl.when(ci == 0)
    def _():
        z_scr[...] = jnp.zeros_like(z_scr)
        if latent:
            for d in range(2):
                for hh in range(h):
                    z_scr[d, hh, hh * dk:(hh + 1) * dk, :] = s0_ref[d, hh]

    ii = lax.broadcasted_iota(jnp.int32, (c, c), 0)
    jj = lax.broadcasted_iota(jnp.int32, (c, c), 1)
    icol = lax.broadcasted_iota(jnp.int32, (c, 1), 0).astype(F32)
    lane = lax.broadcasted_iota(jnp.int32, (1, hk), 1)

    for d, (qk_ref, v_ref, o_ref) in enumerate(((qkf_ref, vf_ref, of_ref), (qkb_ref, vb_ref, ob_ref))):
        qk = qk_ref[...]
        if latent:
            cos_ref, sin_ref = (cosf_ref, sinf_ref) if d == 0 else (cosb_ref, sinb_ref)
            qk = _rope(qk, cos_ref[...], sin_ref[...], dk // 4)
        q = qk[:, :hk]
        k = qk[:, hk:] * (dk ** -0.5)
        v = v_ref[...]
        diff = (ii - jj) if d == 0 else (jj - ii)
        causal = diff >= 0
        dpos = jnp.where(causal, diff, 0).astype(F32)
        qe = (icol + 1.0) if d == 0 else (c - icol)
        ke = (c - 1.0 - icol) if d == 0 else icol
        outs = []
        for hh in range(h):
            lg = ld_ref[d, hh]
            msk = (lane // dk) == hh
            qh = jnp.where(msk, q, 0.0)
            kh = jnp.where(msk, k, 0.0)
            vh = v[:, hh * dv:(hh + 1) * dv]
            d_intra = jnp.where(causal, jnp.exp(lg * dpos), 0.0)
            att = _bdot_nt(qh, kh) * d_intra
            z = z_scr[d, hh]
            o = _bdot(att, vh) + _bdot(qh * jnp.exp(lg * qe), z)
            c_dec = jnp.exp(lg * jnp.full((1, 1), c, F32))
            z_scr[d, hh] = z * c_dec + _bdot_tn(kh * jnp.exp(lg * ke), vh)
            outs.append(o)
        o_ref[...] = jnp.concatenate(outs, axis=-1)

    if emit_state:
        @pl.when(ci == nc - 1)
        def _():
            for d in range(2):
                for hh in range(h):
                    sfin_ref[d, hh] = z_scr[d, hh, hh * dk:(hh + 1) * dk, :]


def retention_scan(qk, v, log_decay, s0, h, dk, dv, latent, emit_state):
    b, l, _ = qk.shape
    c = CHUNK
    nc = l // c
    hk = h * dk
    fwd = lambda w: pl.BlockSpec((None, c, w), lambda bb, ci: (bb, ci, 0))
    bwd = lambda w: pl.BlockSpec((None, c, w), lambda bb, ci: (bb, nc - 1 - ci, 0))
    args = [log_decay, qk, v, qk, v]
    specs = [pl.BlockSpec(memory_space=pltpu.SMEM), fwd(2 * hk), fwd(h * dv), bwd(2 * hk), bwd(h * dv)]
    if latent:
        cos, sin = _rope_tables(l, dk, 2 * h)
        args += [cos, sin, cos, sin, s0]
        tf = pl.BlockSpec((c, 2 * hk), lambda bb, ci: (ci, 0))
        tb = pl.BlockSpec((c, 2 * hk), lambda bb, ci: (nc - 1 - ci, 0))
        specs += [tf, tf, tb, tb, pl.BlockSpec((None, 2, h, dk, dv), lambda bb, ci: (bb, 0, 0, 0, 0))]
    out_shape = [jax.ShapeDtypeStruct((b, l, h * dv), F32)] * 2
    out_specs = [fwd(h * dv), bwd(h * dv)]
    if emit_state:
        out_shape.append(jax.ShapeDtypeStruct((b, 2, h, dk, dv), F32))
        out_specs.append(pl.BlockSpec((None, 2, h, dk, dv), lambda bb, ci: (bb, 0, 0, 0, 0)))
    return pl.pallas_call(
        functools.partial(_ret_kernel, h=h, dk=dk, dv=dv, c=c, nc=nc, latent=latent, emit_state=emit_state),
        grid=(b, nc),
        in_specs=specs,
        out_specs=out_specs,
        out_shape=out_shape,
        scratch_shapes=[pltpu.VMEM((2, h, hk, dv), F32)],
        compiler_params=_cparams(2),
        name="retention_scan",
    )(*args)


def _rwkv_prep_kernel(p_ref, prev_ref, next_ref, mu_ref, w0_ref, a0_ref, w2_ref, a2_ref, g2_ref,
                      kk_ref, ka_ref, rk_ref, seg_ref, s_out, g_out, bon_out, *, tm, nt, wb):
    i = pl.program_id(1)
    p = p_ref[...]
    row = lax.broadcasted_iota(jnp.int32, p.shape, 0)
    prow = jnp.where(i == 0, 0.0, prev_ref[SUBLANES - 1:SUBLANES, :])
    nrow = jnp.where(i == nt - 1, 0.0, next_ref[0:1, :])
    prev = jnp.where(row == 0, prow, pltpu.roll(p, 1, axis=0))
    nxt = jnp.where(row == tm - 1, nrow, pltpu.roll(p, tm - 1, axis=0))
    ps = p + mu_ref[...] * (0.5 * (prev + nxt) - p)
    r = ps[:, 0:wb]
    k = ps[:, wb:2 * wb]
    v = ps[:, 2 * wb:3 * wb]
    lora = ps[:, 3 * wb:]
    tl = jnp.tanh(lora)
    seg = seg_ref[...]
    kk = k * kk_ref[...]
    nrm = jnp.sqrt(_dot_exact_rhs(kk * kk, seg))
    kk = kk / jnp.maximum(nrm, 1e-12)
    s_out[0] = r
    s_out[1] = v
    s_out[2] = -kk
    g_out[...] = _bdot(_sigmoid(lora), g2_ref[...])
    bon_out[...] = _dot_exact_rhs(r * k * rk_ref[...], seg) * v
    for d in range(2):
        w_log = -_softplus(-(w0_ref[d:d + 1, :] + _bdot(tl, w2_ref[d]))) - 0.5
        s_out[3 + 3 * d] = jnp.exp(-jnp.exp(w_log))
        a = _sigmoid(a0_ref[d:d + 1, :] + _bdot(lora, a2_ref[d]))
        s_out[4 + 3 * d] = k * (1.0 + (a - 1.0) * ka_ref[...])
        s_out[5 + 3 * d] = kk * a


def rwkv_prep(p, mu, w0, w2p, a0, a2p, g2p, k_k, k_a, r_k, seg, wb, tm):
    b, l, pw = p.shape
    nt = l // tm
    r8 = tm // SUBLANES
    nb8 = l // SUBLANES
    params = [_row(mu), w0, a0, w2p, a2p, g2p, _row(k_k), _row(k_a), _row(r_k), seg]
    return pl.pallas_call(
        functools.partial(_rwkv_prep_kernel, tm=tm, nt=nt, wb=wb),
        grid=(b, nt),
        in_specs=[_tok_spec(tm, pw),
                  pl.BlockSpec((None, SUBLANES, pw), lambda bb, i: (bb, jnp.maximum(i * r8 - 1, 0), 0)),
                  pl.BlockSpec((None, SUBLANES, pw), lambda bb, i: (bb, jnp.minimum((i + 1) * r8, nb8 - 1), 0))]
                 + [_full_spec(a) for a in params],
        out_specs=[pl.BlockSpec((9, None, tm, wb), lambda bb, i: (0, bb, i, 0))] + [_tok_spec(tm, wb)] * 2,
        out_shape=[jax.ShapeDtypeStruct((9, b, l, wb), F32)] + [jax.ShapeDtypeStruct((b, l, wb), F32)] * 2,
        compiler_params=_cparams(2),
        name="rwkv_prep",
    )(p, p, p, *params)


ROW_PITCH = 72
CHAIN_TILE = 128
N_CHAIN_ARRAYS = 6


def _chain_groups(b, h):
    per_dir = b * h
    if 2 * per_dir == LANES:
        return 1, True
    assert per_dir % LANES == 0
    return 2 * per_dir // LANES, False


def _to_chains_kernel(lo_ref, hi_ref, o_ref, t_scr, *, n, h, nb, tt):
    for half, ref in enumerate((lo_ref, hi_ref)):
        for bb in range(nb):
            xt = ref[bb].T
            for hh in range(h):
                t_scr[half * nb + bb, hh * ROW_PITCH:hh * ROW_PITCH + n, :] = xt[hh * n:(hh + 1) * n]
    for c in range(n):
        q = jnp.concatenate([t_scr[k, pl.ds(c, h, stride=ROW_PITCH), :] for k in range(2 * nb)], axis=0)
        o_ref[pl.ds(c, tt, stride=ROW_PITCH), :] = q.T
    pad = jnp.zeros((ROW_PITCH - n, LANES), F32)
    for t in range(tt):
        o_ref[t * ROW_PITCH + n:(t + 1) * ROW_PITCH, :] = pad


def to_chains(stack, h):
    _, b, l, w = stack.shape
    n = w // h
    nb = LANES // (2 * h)
    tt = CHAIN_TILE
    ng, mixed = _chain_groups(b, h)
    src1 = lambda a: jnp.where(a < 3, a, a + 3)
    if mixed:
        lo_map = lambda a, g, ti: (a, 0, ti, 0)
        hi_map = lambda a, g, ti: (src1(a), 0, ti, 0)
    else:
        src = lambda a, g: jnp.where(g < ng // 2, a, src1(a))
        blocks_per_dir = ng // 2
        lo_map = lambda a, g, ti: (src(a, g), 2 * (g % blocks_per_dir), ti, 0)
        hi_map = lambda a, g, ti: (src(a, g), 2 * (g % blocks_per_dir) + 1, ti, 0)
    out = pl.pallas_call(
        functools.partial(_to_chains_kernel, n=n, h=h, nb=nb, tt=tt),
        grid=(N_CHAIN_ARRAYS, ng, l // tt),
        in_specs=[pl.BlockSpec((None, nb, tt, w), lo_map), pl.BlockSpec((None, nb, tt, w), hi_map)],
        out_specs=pl.BlockSpec((None, None, tt * ROW_PITCH, LANES), lambda a, g, ti: (a, g, ti, 0)),
        out_shape=jax.ShapeDtypeStruct((N_CHAIN_ARRAYS, ng, l * ROW_PITCH, LANES), F32),
        scratch_shapes=[pltpu.VMEM((2 * nb, h * ROW_PITCH, tt), F32)],
        compiler_params=_cparams(3),
        name="to_chains",
    )(stack, stack)
    return out.reshape(N_CHAIN_ARRAYS, ng, l, ROW_PITCH, LANES)


def _from_chains_kernel(a_ref, b_ref, o_ref, t_scr, *, n, h, nb, tt, mixed):
    lane = lax.broadcasted_iota(jnp.int32, (tt, LANES), 1)
    nk = LANES // h
    for c in range(n):
        va = a_ref[pl.ds(c, tt, stride=ROW_PITCH), :]
        vb = b_ref[pl.ds(c, tt, stride=ROW_PITCH), :]
        tiles = [jnp.where(lane >= LANES // 2, vb, va)] if mixed else [va, vb]
        for idx, v in enumerate(tiles):
            vt = v.T
            for k in range(nk):
                t_scr[idx * nk + k, pl.ds(c, h, stride=ROW_PITCH), :] = vt[k * h:(k + 1) * h]
    for bb in range(nb):
        k0, k1 = (bb, nb + bb) if mixed else (bb, nk + bb)
        parts = [t_scr[k0, hh * ROW_PITCH:hh * ROW_PITCH + n, :] + t_scr[k1, hh * ROW_PITCH:hh * ROW_PITCH + n, :]
                 for hh in range(h)]
        o_ref[bb] = jnp.concatenate(parts, axis=0).T


def from_chains(yf, yb, b, h, n):
    ng, l, _, _ = yf.shape
    _, mixed = _chain_groups(b, h)
    tt = CHAIN_TILE
    nk = LANES // h
    nb = nk // 2 if mixed else nk
    yf2 = yf.reshape(ng, l * ROW_PITCH, LANES)
    yb2 = yb.reshape(ng, l * ROW_PITCH, LANES)
    if mixed:
        a_map = lambda gb, ti: (0, ti, 0)
        b_map = a_map
    else:
        a_map = lambda gb, ti: (gb, ti, 0)
        b_map = lambda gb, ti: (ng // 2 + gb, ti, 0)
    spec = lambda m: pl.BlockSpec((None, tt * ROW_PITCH, LANES), m)
    return pl.pallas_call(
        functools.partial(_from_chains_kernel, n=n, h=h, nb=nb, tt=tt, mixed=mixed),
        grid=(b // nb, l // tt),
        in_specs=[spec(a_map), spec(b_map)],
        out_specs=pl.BlockSpec((nb, tt, h * n), lambda gb, ti: (gb, ti, 0)),
        out_shape=jax.ShapeDtypeStruct((b, l, h * n), F32),
        scratch_shapes=[pltpu.VMEM(((1 if mixed else 2) * nk, h * ROW_PITCH, tt), F32)],
        compiler_params=_cparams(2),
        name="from_chains",
    )(yf2, yb2)


def _rwkv_scan_kernel(*refs, n, tt, nt, ng, mixed, has_state, emit_state):
    it = iter(refs)
    fw = [next(it) for _ in range(N_CHAIN_ARRAYS)]
    bw = [next(it) for _ in range(N_CHAIN_ARRAYS)]
    a_next_ref, a_prev_ref = next(it), next(it)
    s0_ref = next(it) if has_state else None
    yf_ref, yb_ref = next(it), next(it)
    sfin_ref = next(it) if emit_state else None
    s_scr, sa_scr = next(it), next(it)
    r_buf, v_buf, a_buf, w_buf, k_buf, b_buf = bufs = [next(it) for _ in range(N_CHAIN_ARRAYS)]
    g = pl.program_id(0)
    ti = pl.program_id(1)

    lane = lax.broadcasted_iota(jnp.int32, (n, LANES), 1)
    split = LANES // 2 if mixed else jnp.where(g < ng // 2, LANES, 0)
    is_bwd = lane >= split
    for s in range(tt):
        for buf, f_ref, b_ref in zip(bufs, fw, bw):
            buf[s] = jnp.where(is_bwd, b_ref[tt - 1 - s, 0:n, :], f_ref[s, 0:n, :])
    a_after = jnp.where(is_bwd, a_prev_ref[0, 0:n, :], a_next_ref[0, 0:n, :])
    a_buf[tt] = jnp.where(ti == nt - 1, 0.0, a_after)

    @pl.when(ti == 0)
    def _():
        if has_state:
            acc = jnp.zeros(sa_scr.shape, F32)
            for j in range(n):
                sj = s0_ref[j]
                s_scr[j] = sj
                acc = acc + sj * a_buf[0, j:j + 1, :]
            sa_scr[...] = acc
        else:
            s_scr[...] = jnp.zeros_like(s_scr)
            sa_scr[...] = jnp.zeros_like(sa_scr)

    pad = jnp.zeros((ROW_PITCH - n, LANES), F32)

    def step(t, sa):
        vt = v_buf[t]
        y = jnp.zeros_like(sa)
        san = jnp.zeros_like(sa)
        for j in range(n):
            new = (s_scr[j] * w_buf[t, j:j + 1, :] + sa * b_buf[t, j:j + 1, :]) + vt * k_buf[t, j:j + 1, :]
            s_scr[j] = new
            y = y + new * r_buf[t, j:j + 1, :]
            san = san + new * a_buf[t + 1, j:j + 1, :]
        yf_ref[t, 0:n, :] = y
        yf_ref[t, n:, :] = pad
        yb_ref[tt - 1 - t, 0:n, :] = y
        yb_ref[tt - 1 - t, n:, :] = pad
        return san

    sa_scr[...] = lax.fori_loop(0, tt, step, sa_scr[...])

    if emit_state:
        @pl.when(ti == nt - 1)
        def _():
            sfin_ref[...] = s_scr[...]


def rwkv_scan(chains, s0, n, mixed, emit_state, tt=32):
    _, ng, l, rp, lanes = chains.shape
    nt = l // tt
    has_state = s0 is not None

    def seq(a, rev):
        if rev:
            return pl.BlockSpec((None, None, tt, rp, lanes), lambda g, ti: (a, g, nt - 1 - ti, 0, 0))
        return pl.BlockSpec((None, None, tt, rp, lanes), lambda g, ti: (a, g, ti, 0, 0))

    one = lambda m: pl.BlockSpec((None, None, 1, rp, lanes), m)
    st = pl.BlockSpec((None, n, n, lanes), lambda g, ti: (g, 0, 0, 0))
    yspec_f = pl.BlockSpec((None, tt, rp, lanes), lambda g, ti: (g, ti, 0, 0))
    yspec_b = pl.BlockSpec((None, tt, rp, lanes), lambda g, ti: (g, nt - 1 - ti, 0, 0))
    args = [chains] * (2 * N_CHAIN_ARRAYS + 2)
    specs = ([seq(a, False) for a in range(N_CHAIN_ARRAYS)] + [seq(a, True) for a in range(N_CHAIN_ARRAYS)]
             + [one(lambda g, ti: (2, g, jnp.minimum((ti + 1) * tt, l - 1), 0, 0)),
                one(lambda g, ti: (2, g, jnp.maximum((nt - 1 - ti) * tt - 1, 0), 0, 0))])
    if has_state:
        args.append(s0)
        specs.append(st)
    out_shape = [jax.ShapeDtypeStruct((ng, l, rp, lanes), F32)] * 2
    out_specs = [yspec_f, yspec_b]
    if emit_state:
        out_shape.append(jax.ShapeDtypeStruct((ng, n, n, lanes), F32))
        out_specs.append(st)
    seq_buf = pltpu.VMEM((tt, n, lanes), F32)
    return pl.pallas_call(
        functools.partial(_rwkv_scan_kernel, n=n, tt=tt, nt=nt, ng=ng, mixed=mixed,
                          has_state=has_state, emit_state=emit_state),
        grid=(ng, nt),
        in_specs=specs,
        out_specs=out_specs,
        out_shape=out_shape,
        scratch_shapes=[pltpu.VMEM((n, n, lanes), F32), pltpu.VMEM((n, lanes), F32),
                        seq_buf, seq_buf, pltpu.VMEM((tt + 1, n, lanes), F32), seq_buf, seq_buf, seq_buf],
        compiler_params=_cparams(2),
        name="rwkv_scan",
    )(*args)


def _state_to_chains(s):
    b, _, h, n, _ = s.shape
    ng = (2 * b * h) // LANES
    return s.transpose(4, 3, 1, 0, 2).reshape(n, n, ng, LANES).transpose(2, 0, 1, 3)


def _state_from_chains(s, b, h):
    ng, n, _, _ = s.shape
    return s.transpose(1, 2, 0, 3).reshape(n, n, 2, b, h).transpose(3, 2, 4, 1, 0)


def _mlstm_kernel(*refs, h, dh, c, nc, has_state, emit_state):
    it = iter(refs)
    qkvf_ref, gf_ref, qkvb_ref, gb_ref, bias_ref = (next(it) for _ in range(5))
    if has_state:
        c0_ref, n0_ref, m0_ref = next(it), next(it), next(it)
    hf_ref, hb_ref = next(it), next(it)
    if emit_state:
        cfin_ref, nfin_ref, mfin_ref = next(it), next(it), next(it)
    cm_scr, nv_scr, m_scr = next(it), next(it), next(it)
    ci = pl.program_id(1)

    @pl.when(ci == 0)
    def _():
        if has_state:
            for d in range(2):
                for hh in range(h):
                    cm_scr[d * h + hh] = c0_ref[d, hh]
            nv_scr[...] = n0_ref[...]
            m_scr[...] = m0_ref[...]
        else:
            cm_scr[...] = jnp.zeros_like(cm_scr)
            nv_scr[...] = jnp.zeros_like(nv_scr)
            m_scr[...] = jnp.zeros_like(m_scr)

    ii = lax.broadcasted_iota(jnp.int32, (c, c), 0)
    jj = lax.broadcasted_iota(jnp.int32, (c, c), 1)
    gl_lane = lax.broadcasted_iota(jnp.int32, (c, LANES), 1)
    is_fg = (gl_lane >= 2 * h) & (gl_lane < 4 * h)

    for d, (qkv_ref, g_ref, o_ref) in enumerate(((qkvf_ref, gf_ref, hf_ref), (qkvb_ref, gb_ref, hb_ref))):
        causal = (ii >= jj) if d == 0 else (ii <= jj)
        tri = causal.astype(BF16)
        gx = g_ref[...] + bias_ref[...]
        gl = jnp.where(is_fg, -_softplus(-gx), gx)
        bcum = _dot_exact_lhs(tri, gl)
        glt = gl.T
        bt = bcum.T
        qkv = qkv_ref[...]
        end = c - 1 if d == 0 else 0
        outs = []
        for hh in range(h):
            ig_c = d * h + hh
            fg_c = 2 * h + d * h + hh
            sr = d * h + hh
            q = qkv[:, hh * dh:(hh + 1) * dh]
            k = qkv[:, (h + hh) * dh:(h + hh + 1) * dh] * (dh ** -0.5)
            v = qkv[:, (2 * h + hh) * dh:(2 * h + hh + 1) * dh]
            b_col = bcum[:, fg_c:fg_c + 1]
            b_row = bt[fg_c:fg_c + 1, :]
            ig_row = glt[ig_c:ig_c + 1, :]
            ig_col = gl[:, ig_c:ig_c + 1]
            m = m_scr[sr:sr + 1, 0:1]
            nv = nv_scr[sr:sr + 1, :]
            cm = cm_scr[sr]
            a_col = b_col + m
            dlog = jnp.where(causal, b_col - b_row + ig_row, -jnp.inf)
            mt = jnp.maximum(a_col, jnp.max(dlog, axis=-1, keepdims=True))
            wgt = jnp.exp(dlog - mt)
            s = _bdot_nt(q, k) * wgt
            inter = jnp.exp(a_col - mt)
            num = _bdot(s, v) + inter * _bdot(q, cm)
            den = jnp.sum(s, axis=-1, keepdims=True) + inter * jnp.sum(q * nv, axis=-1, keepdims=True)
            outs.append(num / jnp.maximum(jnp.abs(den), jnp.exp(-mt)))
            bl = bt[fg_c:fg_c + 1, end:end + 1]
            wl = bl - b_col + ig_col
            m_new = jnp.maximum(bl + m, jnp.max(wl, axis=0, keepdims=True))
            wk = jnp.exp(wl - m_new) * k
            dec = jnp.exp(bl + m - m_new)
            cm_scr[sr] = dec * cm + _bdot_tn(wk, v)
            nv_scr[sr:sr + 1, :] = dec * nv + jnp.sum(wk, axis=0, keepdims=True)
            m_scr[sr:sr + 1, :] = jnp.broadcast_to(m_new, (1, LANES))
        o_ref[...] = jnp.concatenate(outs, axis=-1)

    if emit_state:
        @pl.when(ci == nc - 1)
        def _():
            for d in range(2):
                for hh in range(h):
                    cfin_ref[d, hh] = cm_scr[d * h + hh]
            nfin_ref[...] = nv_scr[...]
            mfin_ref[...] = m_scr[...]


def mlstm_scan(qkv, gates, bias_row, c0, n0, m0, h, dh, emit_state):
    b, l, _ = qkv.shape
    c = CHUNK
    nc = l // c
    has_state = c0 is not None
    w = qkv.shape[-1]
    fwd = lambda ww: pl.BlockSpec((None, c, ww), lambda bb, ci: (bb, ci, 0))
    bwd = lambda ww: pl.BlockSpec((None, c, ww), lambda bb, ci: (bb, nc - 1 - ci, 0))
    cspec = pl.BlockSpec((None, 2, h, dh, dh), lambda bb, ci: (bb, 0, 0, 0, 0))
    rspec = pl.BlockSpec((None, 2 * h, LANES), lambda bb, ci: (bb, 0, 0))
    args = [qkv, gates, qkv, gates, bias_row]
    specs = [fwd(w), fwd(LANES), bwd(w), bwd(LANES), _full_spec(bias_row)]
    if has_state:
        args += [c0, n0.reshape(b, 2 * h, dh), jnp.broadcast_to(m0.reshape(b, 2 * h, 1), (b, 2 * h, LANES))]
        specs += [cspec, rspec, rspec]
    out_shape = [jax.ShapeDtypeStruct((b, l, h * dh), F32)] * 2
    out_specs = [fwd(h * dh), bwd(h * dh)]
    if emit_state:
        out_shape += [jax.ShapeDtypeStruct((b, 2, h, dh, dh), F32),
                      jax.ShapeDtypeStruct((b, 2 * h, dh), F32), jax.ShapeDtypeStruct((b, 2 * h, LANES), F32)]
        out_specs += [cspec, rspec, rspec]
    return pl.pallas_call(
        functools.partial(_mlstm_kernel, h=h, dh=dh, c=c, nc=nc, has_state=has_state, emit_state=emit_state),
        grid=(b, nc),
        in_specs=specs,
        out_specs=out_specs,
        out_shape=out_shape,
        scratch_shapes=[pltpu.VMEM((2 * h, dh, dh), F32), pltpu.VMEM((2 * h, dh), F32),
                        pltpu.VMEM((2 * h, LANES), F32)],
        compiler_params=_cparams(2),
        name="mlstm_scan",
    )(*args)


def _attn_prep_kernel(*refs, hq, hkv, hd, latent):
    it = iter(refs)
    q_ref, kv_ref, gq_ref, gk_ref, segq_ref, segk_ref = (next(it) for _ in range(6))
    if latent:
        cq_ref, sq_ref, ck_ref, sk_ref = (next(it) for _ in range(4))
    qo_ref, ko_ref, vo_ref = next(it), next(it), next(it)
    q = q_ref[...]
    kv = kv_ref[...]
    kw = hkv * hd
    k = kv[:, :kw]
    v = kv[:, kw:]
    qn = q * lax.rsqrt(_dot_exact_rhs(q * q, segq_ref[...]) * (1.0 / hd) + RMS_EPS) * gq_ref[...]
    kn = k * lax.rsqrt(_dot_exact_rhs(k * k, segk_ref[...]) * (1.0 / hd) + RMS_EPS) * gk_ref[...]
    if latent:
        qn = _rope(qn, cq_ref[...], sq_ref[...], hd // 4)
        kn = _rope(kn, ck_ref[...], sk_ref[...], hd // 4)
    qo_ref[...] = qn * (hd ** -0.5)
    for j in range(hkv):
        ko_ref[j] = kn[:, j * hd:(j + 1) * hd]
        vo_ref[j] = v[:, j * hd:(j + 1) * hd]


def attn_prep(q, kv, qk_gain, hq, hkv, hd, latent, tm):
    b, l, wq = q.shape
    wkv = kv.shape[-1]
    kw = hkv * hd
    params = [jnp.tile(qk_gain[0], hq).reshape(1, wq), jnp.tile(qk_gain[1], hkv).reshape(1, kw),
              _seg_ones(wq, hd), _seg_ones(kw, hd)]
    args = [q, kv] + params
    specs = [_tok_spec(tm, wq), _tok_spec(tm, wkv)] + [_full_spec(a) for a in params]
    if latent:
        cq, sq = _rope_tables(l, hd, hq)
        ck, sk = _rope_tables(l, hd, hkv)
        args += [cq, sq, ck, sk]
        specs += [pl.BlockSpec((tm, wq), lambda bb, i: (i, 0))] * 2 + [pl.BlockSpec((tm, kw), lambda bb, i: (i, 0))] * 2
    kvspec = pl.BlockSpec((None, hkv, tm, hd), lambda bb, i: (bb, 0, i, 0))
    return pl.pallas_call(
        functools.partial(_attn_prep_kernel, hq=hq, hkv=hkv, hd=hd, latent=latent),
        grid=(b, l // tm),
        in_specs=specs,
        out_specs=[_tok_spec(tm, wq), kvspec, kvspec],
        out_shape=[jax.ShapeDtypeStruct((b, l, wq), F32)] + [jax.ShapeDtypeStruct((b, hkv, l, hd), F32)] * 2,
        compiler_params=_cparams(2),
        name="attn_prep",
    )(*args)


def _attn_kernel(*refs, g, hd, tq, has_ctx):
    it = iter(refs)
    q_ref, k_ref, v_ref = next(it), next(it), next(it)
    if has_ctx:
        ck_ref, cv_ref = next(it), next(it)
    o_ref = next(it)
    q = q_ref[...]
    qs = jnp.concatenate([q[:, i * hd:(i + 1) * hd] for i in range(g)], axis=0).astype(BF16)
    k = k_ref[...].astype(BF16)
    v = v_ref[...].astype(BF16)
    s2 = _bdot_nt(qs, k)
    m = jnp.max(s2, axis=-1, keepdims=True)
    if has_ctx:
        ck = ck_ref[...].astype(BF16)
        cv = cv_ref[...].astype(BF16)
        s1 = _bdot_nt(qs, ck)
        m = jnp.maximum(m, jnp.max(s1, axis=-1, keepdims=True))
        p1 = jnp.exp(s1 - m)
    p2 = jnp.exp(s2 - m)
    den = jnp.sum(p2, axis=-1, keepdims=True)
    o = _bdot(p2, v)
    if has_ctx:
        den = den + jnp.sum(p1, axis=-1, keepdims=True)
        o = o + _bdot(p1, cv)
    o = o / den
    for i in range(g):
        o_ref[:, i * hd:(i + 1) * hd] = o[i * tq:(i + 1) * tq, :]


def attention(q, k, v, ctx_k, ctx_v, layer_o, hq, hkv, hd, tq):
    b, l, wq = q.shape
    g = hq // hkv
    has_ctx = ctx_k is not None
    qspec = pl.BlockSpec((None, tq, g * hd), lambda bb, j, i: (bb, i, j))
    kvspec = pl.BlockSpec((None, None, l, hd), lambda bb, j, i: (bb, j, 0, 0))
    args = [q, k, v]
    specs = [qspec, kvspec, kvspec]
    if has_ctx:
        s = ctx_k.shape[3]
        cspec = pl.BlockSpec((None, None, None, s, hd), lambda bb, j, i: (bb, layer_o, j, 0, 0))
        args += [ctx_k, ctx_v]
        specs += [cspec, cspec]
    return pl.pallas_call(
        functools.partial(_attn_kernel, g=g, hd=hd, tq=tq, has_ctx=has_ctx),
        grid=(b, hkv, l // tq),
        in_specs=specs,
        out_specs=qspec,
        out_shape=jax.ShapeDtypeStruct((b, l, wq), F32),
        compiler_params=_cparams(3),
        name="gqa_attention",
    )(*args)


def _pad_rows(w, lo, total):
    return jnp.zeros((total, w.shape[-1]), w.dtype).at[lo:lo + w.shape[0]].set(w)


def kernel(x_prompt, x_sample, state_ret, state_rwkv, state_mlstm_c, state_mlstm_n, state_mlstm_m, cache_k, cache_v, c, c_ctx, ada_w, ada_b, norm_g, ffn_w1, ffn_w2, w_in_even, w_out_even, ret_log_decay, ret_gn_w, rwkv_mu, rwkv_w0, rwkv_w2, rwkv_a0, rwkv_a2, rwkv_g2, rwkv_k_k, rwkv_k_a, rwkv_r_k, rwkv_ln_w, rwkv_ln_b, w_in_odd, w_out_odd, mlstm_i_bias, mlstm_f_bias, mlstm_norm_w, attn_qk_norm, final_norm):
    depth = ada_w.shape[0]
    d_model = x_prompt.shape[-1]
    h_a, dk_a, dv_a = state_ret.shape[3:]
    h_b, hs_b = state_rwkv.shape[3:5]
    h_c, dh_c = state_mlstm_c.shape[3:5]
    hkv_d, hd_d = cache_k.shape[2], cache_k.shape[4]
    wa, wb, wc = h_a * dv_a, h_b * hs_b, h_c * dh_c
    wd = w_out_odd.shape[1] - wc
    hq_d = wd // hd_d
    kvw = hkv_d * hd_d
    n_dec = c.shape[0]
    lora_w, lora_a, lora_g = rwkv_w2.shape[2], rwkv_a2.shape[2], rwkv_g2.shape[1]
    lora_tot = lora_w + lora_a + lora_g

    rows = -(-(n_dec + 1) // SUBLANES) * SUBLANES
    cond = jnp.zeros((rows, d_model), F32).at[:n_dec].set(c).at[n_dec].set(c_ctx)
    mod = modulation_all(cond, ada_w, ada_b).reshape(depth, rows, N_MOD, d_model)

    seg_b = _seg_ones(wb, hs_b)
    streams = {
        "prompt": dict(x=x_prompt, latent=False, tm=x_prompt.shape[1], tq=x_prompt.shape[1]),
        "sample": dict(x=x_sample, latent=True, tm=256, tq=128),
    }
    new_states = {}

    for l in range(depth):
        w1a, w1b = ffn_w1[l, 0].astype(BF16), ffn_w1[l, 1].astype(BF16)
        w2a, w2b = ffn_w2[l, 0].astype(BF16), ffn_w2[l, 1].astype(BF16)
        fin = final_norm if l == depth - 1 else None
        if l % 2 == 0:
            e = l // 2
            w_in = w_in_even[e].astype(BF16)
            w_out = w_out_even[e].astype(BF16)
            splits = (2 * h_a * dk_a, wa, wa, 3 * wb + lora_tot)
            w2p = jnp.stack([_pad_rows(rwkv_w2[e, d], 0, lora_tot) for d in range(2)]).astype(BF16)
            a2p = jnp.stack([_pad_rows(rwkv_a2[e, d], lora_w, lora_tot) for d in range(2)]).astype(BF16)
            g2p = _pad_rows(rwkv_g2[e], lora_w + lora_a, lora_tot).astype(BF16)
        else:
            o = l // 2
            wi = w_in_odd[o]
            g0 = 3 * wc
            p_c = g0 + 4 * h_c + wc
            gates_w = jnp.zeros((d_model, LANES), F32).at[:, :4 * h_c].set(wi[:, g0:g0 + 4 * h_c])
            w_in = jnp.concatenate([wi[:, :g0], wi[:, g0 + 4 * h_c:p_c], wi[:, p_c:p_c + wd],
                                    wi[:, p_c + wd:], gates_w], axis=1).astype(BF16)
            w_out = w_out_odd[o].astype(BF16)
            splits = (3 * wc, wc, wd, 2 * kvw, LANES)
            bias_row = jnp.zeros((1, LANES), F32).at[0, :2 * h_c].set(mlstm_i_bias[o].reshape(-1))
            bias_row = bias_row.at[0, 2 * h_c:4 * h_c].set(mlstm_f_bias[o].reshape(-1))

        for name, st in streams.items():
            x, latent, tm = st["x"], st["latent"], st["tm"]
            b, seq, _ = x.shape
            m = mod[l, :n_dec] if latent else mod[l, n_dec:n_dec + 1]
            emit = not latent
            x1, parts = dense_pre(x, m, norm_g[l, 0], norm_g[l, 1], w1a, w2a, w_in, splits, tm)
            if l % 2 == 0:
                qk_a, v_a, g_a, p_b = parts
                ret = retention_scan(qk_a, v_a, ret_log_decay[e], state_ret[:, e] if latent else None,
                                     h_a, dk_a, dv_a, latent, emit)
                stack, g_b, bonus = rwkv_prep(
                    p_b, rwkv_mu[e], rwkv_w0[e], w2p, rwkv_a0[e], a2p, g2p, rwkv_k_k[e], rwkv_k_a[e],
                    rwkv_r_k[e].reshape(-1), seg_b, wb, min(tm, 256))
                _, mixed = _chain_groups(b, h_b)
                scan = rwkv_scan(to_chains(stack, h_b),
                                 _state_to_chains(state_rwkv[:, e]) if latent else None, hs_b, mixed, emit)
                y = from_chains(scan[0], scan[1], b, h_b, hs_b)
                if emit:
                    new_states.setdefault("ret", []).append(ret[2])
                    new_states.setdefault("rwkv", []).append(_state_from_chains(scan[2], b, h_b))
                x = _dense_post(_post_even_kernel, "dense_post_even", x1,
                                [ret[0], ret[1], g_a, y, bonus, g_b], m,
                                [_row(ret_gn_w[e]), _row(rwkv_ln_w[e]), _row(rwkv_ln_b[e]), seg_b,
                                 _row(norm_g[l, 2]), w_out, w1b, w2b],
                                fin, tm, d_ff=w2b.shape[0], dv=dv_a, hs=hs_b)
            else:
                qkv_c, og, q_d, kv_d, gates = parts
                ml = mlstm_scan(qkv_c, gates, bias_row,
                                state_mlstm_c[:, o] if latent else None,
                                state_mlstm_n[:, o] if latent else None,
                                state_mlstm_m[:, o] if latent else None, h_c, dh_c, emit)
                qn, kn, vn = attn_prep(q_d, kv_d, attn_qk_norm[o], hq_d, hkv_d, hd_d, latent, min(tm, 256))
                out_d = attention(qn, kn, vn, cache_k if latent else None, cache_v if latent else None, o,
                                  hq_d, hkv_d, hd_d, st["tq"])
                if emit:
                    new_states.setdefault("mc", []).append(ml[2])
                    new_states.setdefault("mn", []).append(ml[3].reshape(b, 2, h_c, dh_c))
                    new_states.setdefault("mm", []).append(ml[4][:, :, 0].reshape(b, 2, h_c))
                    new_states.setdefault("k", []).append(kn)
                    new_states.setdefault("v", []).append(vn)
                x = _dense_post(_post_odd_kernel, "dense_post_odd", x1, [ml[0], ml[1], og, out_d], m,
                                [_row(mlstm_norm_w[o]), _row(norm_g[l, 2]), w_out, w1b, w2b],
                                fin, tm, d_ff=w2b.shape[0], dh=dh_c)
            st["x"] = x

    stack = lambda key: jnp.stack(new_states[key], axis=1)
    return (streams["prompt"]["x"], streams["sample"]["x"], stack("ret"), stack("rwkv"), stack("mc"),
            stack("mn"), stack("mm"), stack("k"), stack("v"))
```

```python
import functools

import jax
import jax.numpy as jnp
from jax import lax
from jax.experimental import pallas as pl
from jax.experimental.pallas import tpu as pltpu

F32 = jnp.float32
BF16 = jnp.bfloat16

GRID_W = 64
CHUNK = 128
ROPE_BASE = 10000.0
RMS_EPS = 1e-6
N_MOD = 9
LANES = 128
SUBLANES = 8
VMEM_LIMIT = 56 * 1024 * 1024


def _cparams(n_axes):
    return pltpu.CompilerParams(dimension_semantics=("arbitrary",) * n_axes, vmem_limit_bytes=VMEM_LIMIT)


def _bdot(a, b):
    return jnp.dot(a.astype(BF16), b.astype(BF16), preferred_element_type=F32)


def _bdot_nt(a, b):
    return lax.dot_general(a.astype(BF16), b.astype(BF16), (((1,), (1,)), ((), ())),
                           preferred_element_type=F32)


def _bdot_tn(a, b):
    return lax.dot_general(a.astype(BF16), b.astype(BF16), (((0,), (0,)), ((), ())),
                           preferred_element_type=F32)


def _split3(x):
    hi = x.astype(BF16)
    r1 = x - hi.astype(F32)
    mid = r1.astype(BF16)
    lo = (r1 - mid.astype(F32)).astype(BF16)
    return hi, mid, lo


def _dot_exact_rhs(x, e):
    e = e.astype(BF16)
    hi, mid, lo = _split3(x)
    return (jnp.dot(hi, e, preferred_element_type=F32) + jnp.dot(mid, e, preferred_element_type=F32)
            + jnp.dot(lo, e, preferred_element_type=F32))


def _dot_exact_lhs(e, x):
    e = e.astype(BF16)
    hi, mid, lo = _split3(x)
    return (jnp.dot(e, hi, preferred_element_type=F32) + jnp.dot(e, mid, preferred_element_type=F32)
            + jnp.dot(e, lo, preferred_element_type=F32))


def _sigmoid(x):
    return 1.0 / (1.0 + jnp.exp(-x))


def _silu(x):
    return x * _sigmoid(x)


def _softplus(x):
    return jnp.maximum(x, 0.0) + jnp.log(1.0 + jnp.exp(-jnp.abs(x)))


def _rms(x, g):
    return x * lax.rsqrt(jnp.mean(x * x, axis=-1, keepdims=True) + RMS_EPS) * g


def _seg_ones(n, seg):
    r = lax.broadcasted_iota(jnp.int32, (n, n), 0) // seg
    c = lax.broadcasted_iota(jnp.int32, (n, n), 1) // seg
    return (r == c).astype(BF16)


def _full_spec(arr):
    nd = arr.ndim
    return pl.BlockSpec(arr.shape, lambda *_: (0,) * nd, pipeline_mode=pl.Buffered(1))


def _tok_spec(tm, w):
    return pl.BlockSpec((None, tm, w), lambda b, i: (b, i, 0))


def _row(a):
    return a.reshape(1, -1)


def _mod_kernel(c_ref, w_ref, b_ref, o_ref):
    o_ref[...] = _bdot(_silu(c_ref[...]), w_ref[...]) + b_ref[...]


def modulation_all(cond, ada_w, ada_b, tn=1024):
    depth, d, n = ada_w.shape
    rows = cond.shape[0]
    return pl.pallas_call(
        _mod_kernel,
        grid=(depth, n // tn),
        in_specs=[pl.BlockSpec((rows, d), lambda l, j: (0, 0)),
                  pl.BlockSpec((None, d, tn), lambda l, j: (l, 0, j)),
                  pl.BlockSpec((None, 1, tn), lambda l, j: (l, 0, j))],
        out_specs=pl.BlockSpec((None, rows, tn), lambda l, j: (l, 0, j)),
        out_shape=jax.ShapeDtypeStruct((depth, rows, n), F32),
        compiler_params=_cparams(2),
        name="adaln_modulation",
    )(cond, ada_w, ada_b.reshape(depth, 1, n))


def _ff_chunk(d_ff):
    return 1408 if d_ff % 1408 == 0 else d_ff


def _swiglu(h_bf, w1_ref, w2_ref, d_ff):
    fc = _ff_chunk(d_ff)
    acc = None
    for c in range(d_ff // fc):
        gate = jnp.dot(h_bf, w1_ref[:, c * fc:(c + 1) * fc], preferred_element_type=F32)
        up = jnp.dot(h_bf, w1_ref[:, d_ff + c * fc:d_ff + (c + 1) * fc], preferred_element_type=F32)
        a = (_silu(gate) * up).astype(BF16)
        part = jnp.dot(a, w2_ref[c * fc:(c + 1) * fc, :], preferred_element_type=F32)
        acc = part if acc is None else acc + part
    return acc


def _mod_specs(mod, ks):
    bc, _, d = mod.shape
    mod4 = mod.reshape(bc, N_MOD, 1, d)

    def spec(k):
        if bc == 1:
            return pl.BlockSpec((None, None, 1, d), lambda b, i: (0, k, 0, 0))
        return pl.BlockSpec((None, None, 1, d), lambda b, i: (b, k, 0, 0))

    return [mod4] * len(ks), [spec(k) for k in ks]


def _pre_kernel(x_ref, sh1_ref, sc1_ref, gt1_ref, sh2_ref, sc2_ref, g1_ref, g2_ref,
                w1_ref, w2_ref, win_ref, x_out_ref, *p_refs, d_ff, splits):
    x = x_ref[...]
    h = _rms(x, g1_ref[...]) * (1.0 + sc1_ref[...]) + sh1_ref[...]
    x1 = x + 0.5 * gt1_ref[...] * _swiglu(h.astype(BF16), w1_ref, w2_ref, d_ff)
    x_out_ref[...] = x1
    h2 = (_rms(x1, g2_ref[...]) * (1.0 + sc2_ref[...]) + sh2_ref[...]).astype(BF16)
    off = 0
    for ref, wdt in zip(p_refs, splits):
        ref[...] = jnp.dot(h2, win_ref[:, off:off + wdt], preferred_element_type=F32)
        off += wdt


def dense_pre(x, mod, g1, g2, w1, w2, w_in, splits, tm):
    b, l, d = x.shape
    margs, mspecs = _mod_specs(mod, (0, 1, 2, 3, 4))
    params = [_row(g1), _row(g2), w1, w2, w_in]
    outs = pl.pallas_call(
        functools.partial(_pre_kernel, d_ff=w2.shape[0], splits=tuple(splits)),
        grid=(b, l // tm),
        in_specs=[_tok_spec(tm, d)] + mspecs + [_full_spec(a) for a in params],
        out_specs=[_tok_spec(tm, d)] + [_tok_spec(tm, w) for w in splits],
        out_shape=[jax.ShapeDtypeStruct((b, l, d), F32)] + [jax.ShapeDtypeStruct((b, l, w), F32) for w in splits],
        compiler_params=_cparams(2),
        name="dense_pre",
    )(x, *margs, *params)
    return outs[0], outs[1:]


def _head_ln(x, width, eps):
    parts = []
    for h in range(x.shape[-1] // width):
        xh = x[:, h * width:(h + 1) * width]
        mu = jnp.mean(xh, axis=-1, keepdims=True)
        xc = xh - mu
        var = jnp.mean(xc * xc, axis=-1, keepdims=True)
        parts.append(xc * lax.rsqrt(var + eps))
    return jnp.concatenate(parts, axis=-1)


def _seg_ln(x, e, width, eps):
    mu = _dot_exact_rhs(x, e) * (1.0 / width)
    xc = x - mu
    var = _dot_exact_rhs(xc * xc, e) * (1.0 / width)
    return xc * lax.rsqrt(var + eps)


def _post_tail(x, mix, gt2_ref, sh3_ref, sc3_ref, gt3_ref, g3_ref, w1_ref, w2_ref, fg_ref, o_ref, d_ff):
    x2 = x + gt2_ref[...] * mix
    h3 = (_rms(x2, g3_ref[...]) * (1.0 + sc3_ref[...]) + sh3_ref[...]).astype(BF16)
    y = x2 + 0.5 * gt3_ref[...] * _swiglu(h3, w1_ref, w2_ref, d_ff)
    if fg_ref is not None:
        y = _rms(y, fg_ref[...])
    o_ref[...] = y


def _post_even_kernel(x_ref, of_ref, ob_ref, ga_ref, y_ref, bon_ref, gb_ref,
                      gt2_ref, sh3_ref, sc3_ref, gt3_ref,
                      gnw_ref, lnw_ref, lnb_ref, seg_ref, g3_ref, wout_ref, w1_ref, w2_ref,
                      *rest, d_ff, dv, hs, final):
    fg_ref = rest[0] if final else None
    o_ref = rest[-1]
    wa = of_ref.shape[-1]
    o = _head_ln(of_ref[...] + ob_ref[...], dv, 1e-5) * gnw_ref[...]
    out_a = _silu(ga_ref[...]) * o
    y = _seg_ln(y_ref[...], seg_ref[...], hs, 64e-5) * lnw_ref[...] + lnb_ref[...]
    out_b = (y + bon_ref[...]) * gb_ref[...]
    mix = _bdot(out_a, wout_ref[0:wa, :]) + _bdot(out_b, wout_ref[wa:, :])
    _post_tail(x_ref[...], mix, gt2_ref, sh3_ref, sc3_ref, gt3_ref, g3_ref, w1_ref, w2_ref, fg_ref, o_ref, d_ff)


def _post_odd_kernel(x_ref, hf_ref, hb_ref, og_ref, od_ref,
                     gt2_ref, sh3_ref, sc3_ref, gt3_ref,
                     nw_ref, g3_ref, wout_ref, w1_ref, w2_ref, *rest, d_ff, dh, final):
    fg_ref = rest[0] if final else None
    o_ref = rest[-1]
    wc = hf_ref.shape[-1]
    h = _head_ln(hf_ref[...] + hb_ref[...], dh, 1e-5) * nw_ref[...]
    out_c = _sigmoid(og_ref[...]) * h
    mix = _bdot(out_c, wout_ref[0:wc, :]) + _bdot(od_ref[...], wout_ref[wc:, :])
    _post_tail(x_ref[...], mix, gt2_ref, sh3_ref, sc3_ref, gt3_ref, g3_ref, w1_ref, w2_ref, fg_ref, o_ref, d_ff)


def _dense_post(kernel_fn, name, x, toks, mod, params, final_g, tm, **kw):
    b, l, d = x.shape
    margs, mspecs = _mod_specs(mod, (5, 6, 7, 8))
    final = final_g is not None
    params = list(params) + ([_row(final_g)] if final else [])
    return pl.pallas_call(
        functools.partial(kernel_fn, final=final, **kw),
        grid=(b, l // tm),
        in_specs=([_tok_spec(tm, d)] + [_tok_spec(tm, a.shape[-1]) for a in toks] + mspecs
                  + [_full_spec(a) for a in params]),
        out_specs=_tok_spec(tm, d),
        out_shape=jax.ShapeDtypeStruct((b, l, d), F32),
        compiler_params=_cparams(2),
        name=name,
    )(x, *toks, *margs, *params)


def _rope_tables(n, d, reps):
    rows = n // GRID_W
    row = jnp.repeat(jnp.arange(rows), GRID_W).astype(F32)
    col = (jnp.arange(rows * GRID_W) % GRID_W).astype(F32)
    nf = d // 4
    inv = ROPE_BASE ** (-jnp.arange(nf, dtype=F32) / nf)
    ang_r = row[:, None] * inv[None, :]
    ang_c = col[:, None] * inv[None, :]
    cos = jnp.concatenate([jnp.cos(ang_r), jnp.cos(ang_r), jnp.cos(ang_c), jnp.cos(ang_c)], axis=-1)
    sin = jnp.concatenate([-jnp.sin(ang_r), jnp.sin(ang_r), -jnp.sin(ang_c), jnp.sin(ang_c)], axis=-1)
    return jnp.tile(cos, (1, reps)), jnp.tile(sin, (1, reps))


def _rope(x, cos, sin, nf):
    w = x.shape[-1]
    lane = lax.broadcasted_iota(jnp.int32, x.shape, 1)
    first = (lane % (2 * nf)) < nf
    partner = jnp.where(first, pltpu.roll(x, w - nf, axis=1), pltpu.roll(x, nf, axis=1))
    return x * cos + partner * sin


def _ret_kernel(ld_ref, *refs, h, dk, dv, c, nc, latent, emit_state):
    it = iter(refs)
    qkf_ref, vf_ref, qkb_ref, vb_ref = next(it), next(it), next(it), next(it)
    if latent:
        cosf_ref, sinf_ref, cosb_ref, sinb_ref, s0_ref = next(it), next(it), next(it), next(it), next(it)
    of_ref, ob_ref = next(it), next(it)
    sfin_ref = next(it) if emit_state else None
    z_scr = next(it)

    ci = pl.program_id(1)
    hk = h * dk

    @pl.when(ci == 0)
    def _():
        z_scr[...] = jnp.zeros_like(z_scr)
        if latent:
            for d in range(2):
                for hh in range(h):
                    z_scr[d, hh, hh * dk:(hh + 1) * dk, :] = s0_ref[d, hh]

    ii = lax.broadcasted_iota(jnp.int32, (c, c), 0)
    jj = lax.broadcasted_iota(jnp.int32, (c, c), 1)
    icol = lax.broadcasted_iota(jnp.int32, (c, 1), 0).astype(F32)
    lane = lax.broadcasted_iota(jnp.int32, (1, hk), 1)

    for d, (qk_ref, v_ref, o_ref) in enumerate(((qkf_ref, vf_ref, of_ref), (qkb_ref, vb_ref, ob_ref))):
        qk = qk_ref[...]
        if latent:
            cos_ref, sin_ref = (cosf_ref, sinf_ref) if d == 0 else (cosb_ref, sinb_ref)
            qk = _rope(qk, cos_ref[...], sin_ref[...], dk // 4)
        q = qk[:, :hk]
        k = qk[:, hk:] * (dk ** -0.5)
        v = v_ref[...]
        diff = (ii - jj) if d == 0 else (jj - ii)
        causal = diff >= 0
        dpos = jnp.where(causal, diff, 0).astype(F32)
        qe = (icol + 1.0) if d == 0 else (c - icol)
        ke = (c - 1.0 - icol) if d == 0 else icol
        outs = []
        for hh in range(h):
            lg = ld_ref[d, hh]
            msk = (lane // dk) == hh
            qh = jnp.where(msk, q, 0.0)
            kh = jnp.where(msk, k, 0.0)
            vh = v[:, hh * dv:(hh + 1) * dv]
            d_intra = jnp.where(causal, jnp.exp(lg * dpos), 0.0)
            att = _bdot_nt(qh, kh) * d_intra
            z = z_scr[d, hh]
            o = _bdot(att, vh) + _bdot(qh * jnp.exp(lg * qe), z)
            c_dec = jnp.exp(lg * jnp.full((1, 1), c, F32))
            z_scr[d, hh] = z * c_dec + _bdot_tn(kh * jnp.exp(lg * ke), vh)
            outs.append(o)
        o_ref[...] = jnp.concatenate(outs, axis=-1)

    if emit_state:
        @pl.when(ci == nc - 1)
        def _():
            for d in range(2):
                for hh in range(h):
                    sfin_ref[d, hh] = z_scr[d, hh, hh * dk:(hh + 1) * dk, :]


def retention_scan(qk, v, log_decay, s0, h, dk, dv, latent, emit_state):
    b, l, _ = qk.shape
    c = CHUNK
    nc = l // c
    hk = h * dk
    fwd = lambda w: pl.BlockSpec((None, c, w), lambda bb, ci: (bb, ci, 0))
    bwd = lambda w: pl.BlockSpec((None, c, w), lambda bb, ci: (bb, nc - 1 - ci, 0))
    args = [log_decay, qk, v, qk, v]
    specs = [pl.BlockSpec(memory_space=pltpu.SMEM), fwd(2 * hk), fwd(h * dv), bwd(2 * hk), bwd(h * dv)]
    if latent:
        cos, sin = _rope_tables(l, dk, 2 * h)
        args += [cos, sin, cos, sin, s0]
        tf = pl.BlockSpec((c, 2 * hk), lambda bb, ci: (ci, 0))
        tb = pl.BlockSpec((c, 2 * hk), lambda bb, ci: (nc - 1 - ci, 0))
        specs += [tf, tf, tb, tb, pl.BlockSpec((None, 2, h, dk, dv), lambda bb, ci: (bb, 0, 0, 0, 0))]
    out_shape = [jax.ShapeDtypeStruct((b, l, h * dv), F32)] * 2
    out_specs = [fwd(h * dv), bwd(h * dv)]
    if emit_state:
        out_shape.append(jax.ShapeDtypeStruct((b, 2, h, dk, dv), F32))
        out_specs.append(pl.BlockSpec((None, 2, h, dk, dv), lambda bb, ci: (bb, 0, 0, 0, 0)))
    return pl.pallas_call(
        functools.partial(_ret_kernel, h=h, dk=dk, dv=dv, c=c, nc=nc, latent=latent, emit_state=emit_state),
        grid=(b, nc),
        in_specs=specs,
        out_specs=out_specs,
        out_shape=out_shape,
        scratch_shapes=[pltpu.VMEM((2, h, hk, dv), F32)],
        compiler_params=_cparams(2),
        name="retention_scan",
    )(*args)


def _rwkv_prep_kernel(p_ref, prev_ref, next_ref, mu_ref, w0_ref, a0_ref, w2_ref, a2_ref, g2_ref,
                      kk_ref, ka_ref, rk_ref, seg_ref, s_out, g_out, bon_out, *, tm, nt, wb):
    i = pl.program_id(1)
    p = p_ref[...]
    row = lax.broadcasted_iota(jnp.int32, p.shape, 0)
    prow = jnp.where(i == 0, 0.0, prev_ref[SUBLANES - 1:SUBLANES, :])
    nrow = jnp.where(i == nt - 1, 0.0, next_ref[0:1, :])
    prev = jnp.where(row == 0, prow, pltpu.roll(p, 1, axis=0))
    nxt = jnp.where(row == tm - 1, nrow, pltpu.roll(p, tm - 1, axis=0))
    ps = p + mu_ref[...] * (0.5 * (prev + nxt) - p)
    r = ps[:, 0:wb]
    k = ps[:, wb:2 * wb]
    v = ps[:, 2 * wb:3 * wb]
    lora = ps[:, 3 * wb:]
    tl = jnp.tanh(lora)
    seg = seg_ref[...]
    kk = k * kk_ref[...]
    nrm = jnp.sqrt(_dot_exact_rhs(kk * kk, seg))
    kk = kk / jnp.maximum(nrm, 1e-12)
    s_out[0] = r
    s_out[1] = v
    s_out[2] = -kk
    g_out[...] = _bdot(_sigmoid(lora), g2_ref[...])
    bon_out[...] = _dot_exact_rhs(r * k * rk_ref[...], seg) * v
    for d in range(2):
        w_log = -_softplus(-(w0_ref[d:d + 1, :] + _bdot(tl, w2_ref[d]))) - 0.5
        s_out[3 + 3 * d] = jnp.exp(-jnp.exp(w_log))
        a = _sigmoid(a0_ref[d:d + 1, :] + _bdot(lora, a2_ref[d]))
        s_out[4 + 3 * d] = k * (1.0 + (a - 1.0) * ka_ref[...])
        s_out[5 + 3 * d] = kk * a


def rwkv_prep(p, mu, w0, w2p, a0, a2p, g2p, k_k, k_a, r_k, seg, wb, tm):
    b, l, pw = p.shape
    nt = l // tm
    r8 = tm // SUBLANES
    nb8 = l // SUBLANES
    params = [_row(mu), w0, a0, w2p, a2p, g2p, _row(k_k), _row(k_a), _row(r_k), seg]
    return pl.pallas_call(
        functools.partial(_rwkv_prep_kernel, tm=tm, nt=nt, wb=wb),
        grid=(b, nt),
        in_specs=[_tok_spec(tm, pw),
                  pl.BlockSpec((None, SUBLANES, pw), lambda bb, i: (bb, jnp.maximum(i * r8 - 1, 0), 0)),
                  pl.BlockSpec((None, SUBLANES, pw), lambda bb, i: (bb, jnp.minimum((i + 1) * r8, nb8 - 1), 0))]
                 + [_full_spec(a) for a in params],
        out_specs=[pl.BlockSpec((9, None, tm, wb), lambda bb, i: (0, bb, i, 0))] + [_tok_spec(tm, wb)] * 2,
        out_shape=[jax.ShapeDtypeStruct((9, b, l, wb), F32)] + [jax.ShapeDtypeStruct((b, l, wb), F32)] * 2,
        compiler_params=_cparams(2),
        name="rwkv_prep",
    )(p, p, p, *params)


ROW_PITCH = 72
CHAIN_TILE = 128
N_CHAIN_ARRAYS = 6


def _chain_groups(b, h):
    per_dir = b * h
    if 2 * per_dir == LANES:
        return 1, True
    assert per_dir % LANES == 0
    return 2 * per_dir // LANES, False


def _to_chains_kernel(lo_ref, hi_ref, o_ref, t_scr, *, n, h, nb, tt):
    for half, ref in enumerate((lo_ref, hi_ref)):
        for bb in range(nb):
            xt = ref[bb].T
            for hh in range(h):
                t_scr[half * nb + bb, hh * ROW_PITCH:hh * ROW_PITCH + n, :] = xt[hh * n:(hh + 1) * n]
    for c in range(n):
        q = jnp.concatenate([t_scr[k, pl.ds(c, h, stride=ROW_PITCH), :] for k in range(2 * nb)], axis=0)
        o_ref[pl.ds(c, tt, stride=ROW_PITCH), :] = q.T
    pad = jnp.zeros((ROW_PITCH - n, LANES), F32)
    for t in range(tt):
        o_ref[t * ROW_PITCH + n:(t + 1) * ROW_PITCH, :] = pad


def to_chains(stack, h):
    _, b, l, w = stack.shape
    n = w // h
    nb = LANES // (2 * h)
    tt = CHAIN_TILE
    ng, mixed = _chain_groups(b, h)
    src1 = lambda a: jnp.where(a < 3, a, a + 3)
    if mixed:
        lo_map = lambda a, g, ti: (a, 0, ti, 0)
        hi_map = lambda a, g, ti: (src1(a), 0, ti, 0)
    else:
        src = lambda a, g: jnp.where(g < ng // 2, a, src1(a))
        blocks_per_dir = ng // 2
        lo_map = lambda a, g, ti: (src(a, g), 2 * (g % blocks_per_dir), ti, 0)
        hi_map = lambda a, g, ti: (src(a, g), 2 * (g % blocks_per_dir) + 1, ti, 0)
    out = pl.pallas_call(
        functools.partial(_to_chains_kernel, n=n, h=h, nb=nb, tt=tt),
        grid=(N_CHAIN_ARRAYS, ng, l // tt),
        in_specs=[pl.BlockSpec((None, nb, tt, w), lo_map), pl.BlockSpec((None, nb, tt, w), hi_map)],
        out_specs=pl.BlockSpec((None, None, tt * ROW_PITCH, LANES), lambda a, g, ti: (a, g, ti, 0)),
        out_shape=jax.ShapeDtypeStruct((N_CHAIN_ARRAYS, ng, l * ROW_PITCH, LANES), F32),
        scratch_shapes=[pltpu.VMEM((2 * nb, h * ROW_PITCH, tt), F32)],
        compiler_params=_cparams(3),
        name="to_chains",
    )(stack, stack)
    return out.reshape(N_CHAIN_ARRAYS, ng, l, ROW_PITCH, LANES)


def _from_chains_kernel(a_ref, b_ref, o_ref, t_scr, *, n, h, nb, tt, mixed):
    lane = lax.broadcasted_iota(jnp.int32, (tt, LANES), 1)
    nk = LANES // h
    for c in range(n):
        va = a_ref[pl.ds(c, tt, stride=ROW_PITCH), :]
        vb = b_ref[pl.ds(c, tt, stride=ROW_PITCH), :]
        tiles = [jnp.where(lane >= LANES // 2, vb, va)] if mixed else [va, vb]
        for idx, v in enumerate(tiles):
            vt = v.T
            for k in range(nk):
                t_scr[idx * nk + k, pl.ds(c, h, stride=ROW_PITCH), :] = vt[k * h:(k + 1) * h]
    for bb in range(nb):
        k0, k1 = (bb, nb + bb) if mixed else (bb, nk + bb)
        parts = [t_scr[k0, hh * ROW_PITCH:hh * ROW_PITCH + n, :] + t_scr[k1, hh * ROW_PITCH:hh * ROW_PITCH + n, :]
                 for hh in range(h)]
        o_ref[bb] = jnp.concatenate(parts, axis=0).T


def from_chains(yf, yb, b, h, n):
    ng, l, _, _ = yf.shape
    _, mixed = _chain_groups(b, h)
    tt = CHAIN_TILE
    nk = LANES // h
    nb = nk // 2 if mixed else nk
    yf2 = yf.reshape(ng, l * ROW_PITCH, LANES)
    yb2 = yb.reshape(ng, l * ROW_PITCH, LANES)
    if mixed:
        a_map = lambda gb, ti: (0, ti, 0)
        b_map = a_map
    else:
        a_map = lambda gb, ti: (gb, ti, 0)
        b_map = lambda gb, ti: (ng // 2 + gb, ti, 0)
    spec = lambda m: pl.BlockSpec((None, tt * ROW_PITCH, LANES), m)
    return pl.pallas_call(
        functools.partial(_from_chains_kernel, n=n, h=h, nb=nb, tt=tt, mixed=mixed),
        grid=(b // nb, l // tt),
        in_specs=[spec(a_map), spec(b_map)],
        out_specs=pl.BlockSpec((nb, tt, h * n), lambda gb, ti: (gb, ti, 0)),
        out_shape=jax.ShapeDtypeStruct((b, l, h * n), F32),
        scratch_shapes=[pltpu.VMEM(((1 if mixed else 2) * nk, h * ROW_PITCH, tt), F32)],
        compiler_params=_cparams(2),
        name="from_chains",
    )(yf2, yb2)


def _rwkv_scan_kernel(*refs, n, tt, nt, ng, mixed, has_state, emit_state):
    it = iter(refs)
    fw = [next(it) for _ in range(N_CHAIN_ARRAYS)]
    bw = [next(it) for _ in range(N_CHAIN_ARRAYS)]
    a_next_ref, a_prev_ref = next(it), next(it)
    s0_ref = next(it) if has_state else None
    yf_ref, yb_ref = next(it), next(it)
    sfin_ref = next(it) if emit_state else None
    s_scr, sa_scr = next(it), next(it)
    r_buf, v_buf, a_buf, w_buf, k_buf, b_buf = bufs = [next(it) for _ in range(N_CHAIN_ARRAYS)]
    g = pl.program_id(0)
    ti = pl.program_id(1)

    lane = lax.broadcasted_iota(jnp.int32, (n, LANES), 1)
    split = LANES // 2 if mixed else jnp.where(g < ng // 2, LANES, 0)
    is_bwd = lane >= split
    for s in range(tt):
        for buf, f_ref, b_ref in zip(bufs, fw, bw):
            buf[s] = jnp.where(is_bwd, b_ref[tt - 1 - s, 0:n, :], f_ref[s, 0:n, :])
    a_after = jnp.where(is_bwd, a_prev_ref[0, 0:n, :], a_next_ref[0, 0:n, :])
    a_buf[tt] = jnp.where(ti == nt - 1, 0.0, a_after)

    @pl.when(ti == 0)
    def _():
        if has_state:
            acc = jnp.zeros(sa_scr.shape, F32)
            for j in range(n):
                sj = s0_ref[j]
                s_scr[j] = sj
                acc = acc + sj * a_buf[0, j:j + 1, :]
            sa_scr[...] = acc
        else:
            s_scr[...] = jnp.zeros_like(s_scr)
            sa_scr[...] = jnp.zeros_like(sa_scr)

    pad = jnp.zeros((ROW_PITCH - n, LANES), F32)

    def step(t, sa):
        vt = v_buf[t]
        y = jnp.zeros_like(sa)
        san = jnp.zeros_like(sa)
        for j in range(n):
            new = (s_scr[j] * w_buf[t, j:j + 1, :] + sa * b_buf[t, j:j + 1, :]) + vt * k_buf[t, j:j + 1, :]
            s_scr[j] = new
            y = y + new * r_buf[t, j:j + 1, :]
            san = san + new * a_buf[t + 1, j:j + 1, :]
        yf_ref[t, 0:n, :] = y
        yf_ref[t, n:, :] = pad
        yb_ref[tt - 1 - t, 0:n, :] = y
        yb_ref[tt - 1 - t, n:, :] = pad
        return san

    sa_scr[...] = lax.fori_loop(0, tt, step, sa_scr[...])

    if emit_state:
        @pl.when(ti == nt - 1)
        def _():
            sfin_ref[...] = s_scr[...]


def rwkv_scan(chains, s0, n, mixed, emit_state, tt=32):
    _, ng, l, rp, lanes = chains.shape
    nt = l // tt
    has_state = s0 is not None

    def seq(a, rev):
        if rev:
            return pl.BlockSpec((None, None, tt, rp, lanes), lambda g, ti: (a, g, nt - 1 - ti, 0, 0))
        return pl.BlockSpec((None, None, tt, rp, lanes), lambda g, ti: (a, g, ti, 0, 0))

    one = lambda m: pl.BlockSpec((None, None, 1, rp, lanes), m)
    st = pl.BlockSpec((None, n, n, lanes), lambda g, ti: (g, 0, 0, 0))
    yspec_f = pl.BlockSpec((None, tt, rp, lanes), lambda g, ti: (g, ti, 0, 0))
    yspec_b = pl.BlockSpec((None, tt, rp, lanes), lambda g, ti: (g, nt - 1 - ti, 0, 0))
    args = [chains] * (2 * N_CHAIN_ARRAYS + 2)
    specs = ([seq(a, False) for a in range(N_CHAIN_ARRAYS)] + [seq(a, True) for a in range(N_CHAIN_ARRAYS)]
             + [one(lambda g, ti: (2, g, jnp.minimum((ti + 1) * tt, l - 1), 0, 0)),
                one(lambda g, ti: (2, g, jnp.maximum((nt - 1 - ti) * tt - 1, 0), 0, 0))])
    if has_state:
        args.append(s0)
        specs.append(st)
    out_shape = [jax.ShapeDtypeStruct((ng, l, rp, lanes), F32)] * 2
    out_specs = [yspec_f, yspec_b]
    if emit_state:
        out_shape.append(jax.ShapeDtypeStruct((ng, n, n, lanes), F32))
        out_specs.append(st)
    seq_buf = pltpu.VMEM((tt, n, lanes), F32)
    return pl.pallas_call(
        functools.partial(_rwkv_scan_kernel, n=n, tt=tt, nt=nt, ng=ng, mixed=mixed,
                          has_state=has_state, emit_state=emit_state),
        grid=(ng, nt),
        in_specs=specs,
        out_specs=out_specs,
        out_shape=out_shape,
        scratch_shapes=[pltpu.VMEM((n, n, lanes), F32), pltpu.VMEM((n, lanes), F32),
                        seq_buf, seq_buf, pltpu.VMEM((tt + 1, n, lanes), F32), seq_buf, seq_buf, seq_buf],
        compiler_params=_cparams(2),
        name="rwkv_scan",
    )(*args)


def _state_to_chains(s):
    b, _, h, n, _ = s.shape
    ng = (2 * b * h) // LANES
    return s.transpose(4, 3, 1, 0, 2).reshape(n, n, ng, LANES).transpose(2, 0, 1, 3)


def _state_from_chains(s, b, h):
    ng, n, _, _ = s.shape
    return s.transpose(1, 2, 0, 3).reshape(n, n, 2, b, h).transpose(3, 2, 4, 1, 0)


def _mlstm_kernel(*refs, h, dh, c, nc, has_state, emit_state):
    it = iter(refs)
    qkvf_ref, gf_ref, qkvb_ref, gb_ref, bias_ref = (next(it) for _ in range(5))
    if has_state:
        c0_ref, n0_ref, m0_ref = next(it), next(it), next(it)
    hf_ref, hb_ref = next(it), next(it)
    if emit_state:
        cfin_ref, nfin_ref, mfin_ref = next(it), next(it), next(it)
    cma_scr, m_scr = next(it), next(it)
    ci = pl.program_id(1)
    row0 = lax.broadcasted_iota(jnp.int32, (dh, dh), 0) == 0

    @pl.when(ci == 0)
    def _():
        if has_state:
            for s in range(2 * h):
                cma_scr[s, :, 0:dh] = c0_ref[s // h, s % h]
                cma_scr[s, :, dh:] = jnp.where(row0, n0_ref[s:s + 1, :], 0.0).T
            m_scr[...] = m0_ref[...]
        else:
            cma_scr[...] = jnp.zeros_like(cma_scr)
            m_scr[...] = jnp.zeros_like(m_scr)

    ii = lax.broadcasted_iota(jnp.int32, (c, c), 0)
    jj = lax.broadcasted_iota(jnp.int32, (c, c), 1)
    gl_lane = lax.broadcasted_iota(jnp.int32, (c, LANES), 1)
    is_fg = (gl_lane >= 2 * h) & (gl_lane < 4 * h)
    ones_col = (lax.broadcasted_iota(jnp.int32, (c, dh), 1) == 0).astype(F32)

    for d, (qkv_ref, g_ref, o_ref) in enumerate(((qkvf_ref, gf_ref, hf_ref), (qkvb_ref, gb_ref, hb_ref))):
        causal = (ii >= jj) if d == 0 else (ii <= jj)
        gx = g_ref[...] + bias_ref[...]
        gl = jnp.where(is_fg, -_softplus(-gx), gx)
        bcum = _dot_exact_lhs(causal.astype(BF16), gl)
        glt = gl.T
        brow = _dot_exact_rhs(glt, (~causal).astype(BF16) + (ii == jj).astype(BF16))
        qkv = qkv_ref[...]
        kt_all = (qkv[:, h * dh:2 * h * dh] * (dh ** -0.5)).T
        end = c - 1 if d == 0 else 0
        outs = []
        for hh in range(h):
            ig_c = d * h + hh
            fg_c = 2 * h + d * h + hh
            sr = d * h + hh
            q = qkv[:, hh * dh:(hh + 1) * dh]
            k = qkv[:, (h + hh) * dh:(h + hh + 1) * dh] * (dh ** -0.5)
            v_aug = jnp.concatenate([qkv[:, (2 * h + hh) * dh:(2 * h + hh + 1) * dh], ones_col], axis=-1)
            kt = kt_all[hh * dh:(hh + 1) * dh, :]
            b_col = bcum[:, fg_c:fg_c + 1]
            b_row = brow[fg_c:fg_c + 1, :]
            c_row = glt[ig_c:ig_c + 1, :] - b_row
            m = m_scr[sr:sr + 1, 0:1]
            cma = cma_scr[sr]
            dmat = jnp.where(causal, c_row, -jnp.inf)
            g_col = jnp.maximum(m, jnp.max(dmat, axis=-1, keepdims=True))
            s = _bdot_nt(q, k) * jnp.exp(dmat - g_col)
            nd = _bdot(s, v_aug) + jnp.exp(m - g_col) * _bdot(q, cma)
            den = nd[:, dh:dh + 1]
            outs.append(nd[:, 0:dh] / jnp.maximum(jnp.abs(den), jnp.exp(-(b_col + g_col))))
            bl = b_row[:, end:end + 1]
            wl_row = bl + c_row
            m_new = jnp.maximum(bl + m, jnp.max(wl_row, axis=-1, keepdims=True))
            cma_scr[sr] = jnp.exp(bl + m - m_new) * cma + _bdot(kt * jnp.exp(wl_row - m_new), v_aug)
            m_scr[sr:sr + 1, :] = jnp.broadcast_to(m_new, (1, LANES))
        o_ref[...] = jnp.concatenate(outs, axis=-1)

    if emit_state:
        @pl.when(ci == nc - 1)
        def _():
            for s in range(2 * h):
                cfin_ref[s // h, s % h] = cma_scr[s, :, 0:dh]
                nfin_ref[s:s + 1, :] = cma_scr[s, :, dh:].T[0:1, :]
            mfin_ref[...] = m_scr[...]


def mlstm_scan(qkv, gates, bias_row, c0, n0, m0, h, dh, emit_state):
    b, l, _ = qkv.shape
    c = CHUNK
    nc = l // c
    has_state = c0 is not None
    w = qkv.shape[-1]
    fwd = lambda ww: pl.BlockSpec((None, c, ww), lambda bb, ci: (bb, ci, 0))
    bwd = lambda ww: pl.BlockSpec((None, c, ww), lambda bb, ci: (bb, nc - 1 - ci, 0))
    cspec = pl.BlockSpec((None, 2, h, dh, dh), lambda bb, ci: (bb, 0, 0, 0, 0))
    rspec = pl.BlockSpec((None, 2 * h, LANES), lambda bb, ci: (bb, 0, 0))
    args = [qkv, gates, qkv, gates, bias_row]
    specs = [fwd(w), fwd(LANES), bwd(w), bwd(LANES), _full_spec(bias_row)]
    if has_state:
        args += [c0, n0.reshape(b, 2 * h, dh), jnp.broadcast_to(m0.reshape(b, 2 * h, 1), (b, 2 * h, LANES))]
        specs += [cspec, rspec, rspec]
    out_shape = [jax.ShapeDtypeStruct((b, l, h * dh), F32)] * 2
    out_specs = [fwd(h * dh), bwd(h * dh)]
    if emit_state:
        out_shape += [jax.ShapeDtypeStruct((b, 2, h, dh, dh), F32),
                      jax.ShapeDtypeStruct((b, 2 * h, dh), F32), jax.ShapeDtypeStruct((b, 2 * h, LANES), F32)]
        out_specs += [cspec, rspec, rspec]
    return pl.pallas_call(
        functools.partial(_mlstm_kernel, h=h, dh=dh, c=c, nc=nc, has_state=has_state, emit_state=emit_state),
        grid=(b, nc),
        in_specs=specs,
        out_specs=out_specs,
        out_shape=out_shape,
        scratch_shapes=[pltpu.VMEM((2 * h, dh, 2 * dh), F32), pltpu.VMEM((2 * h, LANES), F32)],
        compiler_params=_cparams(2),
        name="mlstm_scan",
    )(*args)


def _attn_prep_kernel(*refs, hq, hkv, hd, latent):
    it = iter(refs)
    q_ref, kv_ref, gq_ref, gk_ref, segq_ref, segk_ref = (next(it) for _ in range(6))
    if latent:
        cq_ref, sq_ref, ck_ref, sk_ref = (next(it) for _ in range(4))
    qo_ref, ko_ref, vo_ref = next(it), next(it), next(it)
    q = q_ref[...]
    kv = kv_ref[...]
    kw = hkv * hd
    k = kv[:, :kw]
    v = kv[:, kw:]
    qn = q * lax.rsqrt(_dot_exact_rhs(q * q, segq_ref[...]) * (1.0 / hd) + RMS_EPS) * gq_ref[...]
    kn = k * lax.rsqrt(_dot_exact_rhs(k * k, segk_ref[...]) * (1.0 / hd) + RMS_EPS) * gk_ref[...]
    if latent:
        qn = _rope(qn, cq_ref[...], sq_ref[...], hd // 4)
        kn = _rope(kn, ck_ref[...], sk_ref[...], hd // 4)
    qo_ref[...] = qn * (hd ** -0.5)
    for j in range(hkv):
        ko_ref[j] = kn[:, j * hd:(j + 1) * hd]
        vo_ref[j] = v[:, j * hd:(j + 1) * hd]


def attn_prep(q, kv, qk_gain, hq, hkv, hd, latent, tm):
    b, l, wq = q.shape
    wkv = kv.shape[-1]
    kw = hkv * hd
    params = [jnp.tile(qk_gain[0], hq).reshape(1, wq), jnp.tile(qk_gain[1], hkv).reshape(1, kw),
              _seg_ones(wq, hd), _seg_ones(kw, hd)]
    args = [q, kv] + params
    specs = [_tok_spec(tm, wq), _tok_spec(tm, wkv)] + [_full_spec(a) for a in params]
    if latent:
        cq, sq = _rope_tables(l, hd, hq)
        ck, sk = _rope_tables(l, hd, hkv)
        args += [cq, sq, ck, sk]
        specs += [pl.BlockSpec((tm, wq), lambda bb, i: (i, 0))] * 2 + [pl.BlockSpec((tm, kw), lambda bb, i: (i, 0))] * 2
    kvspec = pl.BlockSpec((None, hkv, tm, hd), lambda bb, i: (bb, 0, i, 0))
    return pl.pallas_call(
        functools.partial(_attn_prep_kernel, hq=hq, hkv=hkv, hd=hd, latent=latent),
        grid=(b, l // tm),
        in_specs=specs,
        out_specs=[_tok_spec(tm, wq), kvspec, kvspec],
        out_shape=[jax.ShapeDtypeStruct((b, l, wq), F32)] + [jax.ShapeDtypeStruct((b, hkv, l, hd), F32)] * 2,
        compiler_params=_cparams(2),
        name="attn_prep",
    )(*args)


def _attn_kernel(*refs, g, hd, tq, kb, has_ctx):
    it = iter(refs)
    q_ref, k_ref, v_ref = next(it), next(it), next(it)
    if has_ctx:
        ck_ref, cv_ref = next(it), next(it)
    o_ref = next(it)
    s_scr = next(it)
    q = q_ref[...]
    qs = jnp.concatenate([q[:, i * hd:(i + 1) * hd] for i in range(g)], axis=0).astype(BF16)
    sources = ([(ck_ref, cv_ref)] if has_ctx else []) + [(k_ref, v_ref)]
    blocks = []
    m = None
    row = 0
    for kr, vr in sources:
        for j in range(kr.shape[0] // kb):
            st = _bdot_nt(kr[j * kb:(j + 1) * kb, :], qs)
            s_scr[row:row + kb, :] = st
            bm = jnp.max(st, axis=0, keepdims=True)
            m = bm if m is None else jnp.maximum(m, bm)
            blocks.append((row, vr, j))
            row += kb
    acc = jnp.zeros((hd, g * tq), F32)
    den = jnp.zeros((1, g * tq), F32)
    for row, vr, j in blocks:
        p = jnp.exp(s_scr[row:row + kb, :] - m)
        den = den + jnp.sum(p, axis=0, keepdims=True)
        acc = acc + _bdot_tn(vr[j * kb:(j + 1) * kb, :], p)
    ot = acc / den
    o = jnp.concatenate([ot, jnp.zeros((LANES - hd, g * tq), F32)], axis=0).T
    for i in range(g):
        o_ref[:, i * hd:(i + 1) * hd] = o[i * tq:(i + 1) * tq, 0:hd]


def attention(q, k, v, ctx_k, ctx_v, layer_o, hq, hkv, hd, tq):
    b, l, wq = q.shape
    g = hq // hkv
    has_ctx = ctx_k is not None
    qspec = pl.BlockSpec((None, tq, g * hd), lambda bb, j, i: (bb, i, j))
    kvspec = pl.BlockSpec((None, None, l, hd), lambda bb, j, i: (bb, j, 0, 0))
    args = [q, k, v]
    specs = [qspec, kvspec, kvspec]
    n_keys = l
    if has_ctx:
        s = ctx_k.shape[3]
        n_keys += s
        cspec = pl.BlockSpec((None, None, None, s, hd), lambda bb, j, i: (bb, layer_o, j, 0, 0))
        args += [ctx_k, ctx_v]
        specs += [cspec, cspec]
    kb = min(256, l)
    return pl.pallas_call(
        functools.partial(_attn_kernel, g=g, hd=hd, tq=tq, kb=kb, has_ctx=has_ctx),
        grid=(b, hkv, l // tq),
        in_specs=specs,
        out_specs=qspec,
        out_shape=jax.ShapeDtypeStruct((b, l, wq), F32),
        scratch_shapes=[pltpu.VMEM((n_keys, g * tq), F32)],
        compiler_params=_cparams(3),
        name="gqa_attention",
    )(*args)


def _pad_rows(w, lo, total):
    return jnp.zeros((total, w.shape[-1]), w.dtype).at[lo:lo + w.shape[0]].set(w)


def kernel(x_prompt, x_sample, state_ret, state_rwkv, state_mlstm_c, state_mlstm_n, state_mlstm_m, cache_k, cache_v, c, c_ctx, ada_w, ada_b, norm_g, ffn_w1, ffn_w2, w_in_even, w_out_even, ret_log_decay, ret_gn_w, rwkv_mu, rwkv_w0, rwkv_w2, rwkv_a0, rwkv_a2, rwkv_g2, rwkv_k_k, rwkv_k_a, rwkv_r_k, rwkv_ln_w, rwkv_ln_b, w_in_odd, w_out_odd, mlstm_i_bias, mlstm_f_bias, mlstm_norm_w, attn_qk_norm, final_norm):
    depth = ada_w.shape[0]
    d_model = x_prompt.shape[-1]
    h_a, dk_a, dv_a = state_ret.shape[3:]
    h_b, hs_b = state_rwkv.shape[3:5]
    h_c, dh_c = state_mlstm_c.shape[3:5]
    hkv_d, hd_d = cache_k.shape[2], cache_k.shape[4]
    wa, wb, wc = h_a * dv_a, h_b * hs_b, h_c * dh_c
    wd = w_out_odd.shape[1] - wc
    hq_d = wd // hd_d
    kvw = hkv_d * hd_d
    n_dec = c.shape[0]
    lora_w, lora_a, lora_g = rwkv_w2.shape[2], rwkv_a2.shape[2], rwkv_g2.shape[1]
    lora_tot = lora_w + lora_a + lora_g

    rows = -(-(n_dec + 1) // SUBLANES) * SUBLANES
    cond = jnp.zeros((rows, d_model), F32).at[:n_dec].set(c).at[n_dec].set(c_ctx)
    mod = modulation_all(cond, ada_w, ada_b).reshape(depth, rows, N_MOD, d_model)

    seg_b = _seg_ones(wb, hs_b)
    streams = {
        "prompt": dict(x=x_prompt, latent=False, tm=x_prompt.shape[1], tq=x_prompt.shape[1]),
        "sample": dict(x=x_sample, latent=True, tm=256, tq=128),
    }
    new_states = {}

    for l in range(depth):
        w1a, w1b = ffn_w1[l, 0].astype(BF16), ffn_w1[l, 1].astype(BF16)
        w2a, w2b = ffn_w2[l, 0].astype(BF16), ffn_w2[l, 1].astype(BF16)
        fin = final_norm if l == depth - 1 else None
        if l % 2 == 0:
            e = l // 2
            w_in = w_in_even[e].astype(BF16)
            w_out = w_out_even[e].astype(BF16)
            splits = (2 * h_a * dk_a, wa, wa, 3 * wb + lora_tot)
            w2p = jnp.stack([_pad_rows(rwkv_w2[e, d], 0, lora_tot) for d in range(2)]).astype(BF16)
            a2p = jnp.stack([_pad_rows(rwkv_a2[e, d], lora_w, lora_tot) for d in range(2)]).astype(BF16)
            g2p = _pad_rows(rwkv_g2[e], lora_w + lora_a, lora_tot).astype(BF16)
        else:
            o = l // 2
            wi = w_in_odd[o]
            g0 = 3 * wc
            p_c = g0 + 4 * h_c + wc
            gates_w = jnp.zeros((d_model, LANES), F32).at[:, :4 * h_c].set(wi[:, g0:g0 + 4 * h_c])
            w_in = jnp.concatenate([wi[:, :g0], wi[:, g0 + 4 * h_c:p_c], wi[:, p_c:p_c + wd],
                                    wi[:, p_c + wd:], gates_w], axis=1).astype(BF16)
            w_out = w_out_odd[o].astype(BF16)
            splits = (3 * wc, wc, wd, 2 * kvw, LANES)
            bias_row = jnp.zeros((1, LANES), F32).at[0, :2 * h_c].set(mlstm_i_bias[o].reshape(-1))
            bias_row = bias_row.at[0, 2 * h_c:4 * h_c].set(mlstm_f_bias[o].reshape(-1))

        for name, st in streams.items():
            x, latent, tm = st["x"], st["latent"], st["tm"]
            b, seq, _ = x.shape
            m = mod[l, :n_dec] if latent else mod[l, n_dec:n_dec + 1]
            emit = not latent
            x1, parts = dense_pre(x, m, norm_g[l, 0], norm_g[l, 1], w1a, w2a, w_in, splits, tm)
            if l % 2 == 0:
                qk_a, v_a, g_a, p_b = parts
                ret = retention_scan(qk_a, v_a, ret_log_decay[e], state_ret[:, e] if latent else None,
                                     h_a, dk_a, dv_a, latent, emit)
                stack, g_b, bonus = rwkv_prep(
                    p_b, rwkv_mu[e], rwkv_w0[e], w2p, rwkv_a0[e], a2p, g2p, rwkv_k_k[e], rwkv_k_a[e],
                    rwkv_r_k[e].reshape(-1), seg_b, wb, min(tm, 256))
                _, mixed = _chain_groups(b, h_b)
                scan = rwkv_scan(to_chains(stack, h_b),
                                 _state_to_chains(state_rwkv[:, e]) if latent else None, hs_b, mixed, emit)
                y = from_chains(scan[0], scan[1], b, h_b, hs_b)
                if emit:
                    new_states.setdefault("ret", []).append(ret[2])
                    new_states.setdefault("rwkv", []).append(_state_from_chains(scan[2], b, h_b))
                x = _dense_post(_post_even_kernel, "dense_post_even", x1,
                                [ret[0], ret[1], g_a, y, bonus, g_b], m,
                                [_row(ret_gn_w[e]), _row(rwkv_ln_w[e]), _row(rwkv_ln_b[e]), seg_b,
                                 _row(norm_g[l, 2]), w_out, w1b, w2b],
                                fin, tm, d_ff=w2b.shape[0], dv=dv_a, hs=hs_b)
            else:
                qkv_c, og, q_d, kv_d, gates = parts
                ml = mlstm_scan(qkv_c, gates, bias_row,
                                state_mlstm_c[:, o] if latent else None,
                                state_mlstm_n[:, o] if latent else None,
                                state_mlstm_m[:, o] if latent else None, h_c, dh_c, emit)
                qn, kn, vn = attn_prep(q_d, kv_d, attn_qk_norm[o], hq_d, hkv_d, hd_d, latent, min(tm, 256))
                out_d = attention(qn, kn, vn, cache_k if latent else None, cache_v if latent else None, o,
                                  hq_d, hkv_d, hd_d, st["tq"])
                if emit:
                    new_states.setdefault("mc", []).append(ml[2])
                    new_states.setdefault("mn", []).append(ml[3].reshape(b, 2, h_c, dh_c))
                    new_states.setdefault("mm", []).append(ml[4][:, :, 0].reshape(b, 2, h_c))
                    new_states.setdefault("k", []).append(kn)
                    new_states.setdefault("v", []).append(vn)
                x = _dense_post(_post_odd_kernel, "dense_post_odd", x1, [ml[0], ml[1], og, out_d], m,
                                [_row(mlstm_norm_w[o]), _row(norm_g[l, 2]), w_out, w1b, w2b],
                                fin, tm, d_ff=w2b.shape[0], dh=dh_c)
            st["x"] = x

    stack = lambda key: jnp.stack(new_states[key], axis=1)
    return (streams["prompt"]["x"], streams["sample"]["x"], stack("ret"), stack("rwkv"), stack("mc"),
            stack("mn"), stack("mm"), stack("k"), stack("v"))
```

```python
import functools

import jax
import jax.numpy as jnp
from jax import lax
from jax.experimental import pallas as pl
from jax.experimental.pallas import tpu as pltpu

F32 = jnp.float32
BF16 = jnp.bfloat16

GRID_W = 64
CHUNK = 128
ROPE_BASE = 10000.0
RMS_EPS = 1e-6
LOG2E = 1.4426950408889634
N_MOD = 9
LANES = 128
SUBLANES = 8
VMEM_LIMIT = 56 * 1024 * 1024


def _cparams(n_axes):
    return pltpu.CompilerParams(dimension_semantics=("arbitrary",) * n_axes, vmem_limit_bytes=VMEM_LIMIT)


def _bdot(a, b):
    return jnp.dot(a.astype(BF16), b.astype(BF16), preferred_element_type=F32)


def _bdot_nt(a, b):
    return lax.dot_general(a.astype(BF16), b.astype(BF16), (((1,), (1,)), ((), ())),
                           preferred_element_type=F32)


def _bdot_tn(a, b):
    return lax.dot_general(a.astype(BF16), b.astype(BF16), (((0,), (0,)), ((), ())),
                           preferred_element_type=F32)


def _split3(x):
    hi = x.astype(BF16)
    r1 = x - hi.astype(F32)
    mid = r1.astype(BF16)
    lo = (r1 - mid.astype(F32)).astype(BF16)
    return hi, mid, lo


def _dot_exact_rhs(x, e):
    e = e.astype(BF16)
    hi, mid, lo = _split3(x)
    return (jnp.dot(hi, e, preferred_element_type=F32) + jnp.dot(mid, e, preferred_element_type=F32)
            + jnp.dot(lo, e, preferred_element_type=F32))


def _dot_exact_lhs(e, x):
    e = e.astype(BF16)
    hi, mid, lo = _split3(x)
    return (jnp.dot(e, hi, preferred_element_type=F32) + jnp.dot(e, mid, preferred_element_type=F32)
            + jnp.dot(e, lo, preferred_element_type=F32))


def _sigmoid(x):
    return 1.0 / (1.0 + jnp.exp(-x))


def _silu(x):
    return x * _sigmoid(x)


def _softplus(x):
    return jnp.maximum(x, 0.0) + jnp.log(1.0 + jnp.exp(-jnp.abs(x)))


def _rms(x, g):
    return x * lax.rsqrt(jnp.mean(x * x, axis=-1, keepdims=True) + RMS_EPS) * g


def _seg_ones(n, seg):
    r = lax.broadcasted_iota(jnp.int32, (n, n), 0) // seg
    c = lax.broadcasted_iota(jnp.int32, (n, n), 1) // seg
    return (r == c).astype(BF16)


def _full_spec(arr):
    nd = arr.ndim
    return pl.BlockSpec(arr.shape, lambda *_: (0,) * nd, pipeline_mode=pl.Buffered(1))


def _tok_spec(tm, w):
    return pl.BlockSpec((None, tm, w), lambda b, i: (b, i, 0))


def _row(a):
    return a.reshape(1, -1)


def _mod_kernel(c_ref, w_ref, b_ref, o_ref):
    o_ref[...] = _bdot(_silu(c_ref[...]), w_ref[...]) + b_ref[...]


def modulation_all(cond, ada_w, ada_b, tn=1024):
    depth, d, n = ada_w.shape
    rows = cond.shape[0]
    return pl.pallas_call(
        _mod_kernel,
        grid=(depth, n // tn),
        in_specs=[pl.BlockSpec((rows, d), lambda l, j: (0, 0)),
                  pl.BlockSpec((None, d, tn), lambda l, j: (l, 0, j)),
                  pl.BlockSpec((None, 1, tn), lambda l, j: (l, 0, j))],
        out_specs=pl.BlockSpec((None, rows, tn), lambda l, j: (l, 0, j)),
        out_shape=jax.ShapeDtypeStruct((depth, rows, n), F32),
        compiler_params=_cparams(2),
        name="adaln_modulation",
    )(cond, ada_w, ada_b.reshape(depth, 1, n))


def _ff_chunk(d_ff):
    return 1408 if d_ff % 1408 == 0 else d_ff


def _swiglu(h_bf, w1_ref, w2_ref, d_ff):
    fc = _ff_chunk(d_ff)
    acc = None
    for c in range(d_ff // fc):
        gate = jnp.dot(h_bf, w1_ref[:, c * fc:(c + 1) * fc], preferred_element_type=F32)
        up = jnp.dot(h_bf, w1_ref[:, d_ff + c * fc:d_ff + (c + 1) * fc], preferred_element_type=F32)
        a = (_silu(gate) * up).astype(BF16)
        part = jnp.dot(a, w2_ref[c * fc:(c + 1) * fc, :], preferred_element_type=F32)
        acc = part if acc is None else acc + part
    return acc


def _mod_specs(mod, ks):
    bc, _, d = mod.shape
    mod4 = mod.reshape(bc, N_MOD, 1, d)

    def spec(k):
        if bc == 1:
            return pl.BlockSpec((None, None, 1, d), lambda b, i: (0, k, 0, 0))
        return pl.BlockSpec((None, None, 1, d), lambda b, i: (b, k, 0, 0))

    return [mod4] * len(ks), [spec(k) for k in ks]


def _pre_kernel(x_ref, sh1_ref, sc1_ref, gt1_ref, sh2_ref, sc2_ref, g1_ref, g2_ref,
                w1_ref, w2_ref, win_ref, x_out_ref, *p_refs, d_ff, splits):
    x = x_ref[...]
    h = _rms(x, g1_ref[...]) * (1.0 + sc1_ref[...]) + sh1_ref[...]
    x1 = x + 0.5 * gt1_ref[...] * _swiglu(h.astype(BF16), w1_ref, w2_ref, d_ff)
    x_out_ref[...] = x1
    h2 = (_rms(x1, g2_ref[...]) * (1.0 + sc2_ref[...]) + sh2_ref[...]).astype(BF16)
    off = 0
    for ref, wdt in zip(p_refs, splits):
        ref[...] = jnp.dot(h2, win_ref[:, off:off + wdt], preferred_element_type=F32)
        off += wdt


def dense_pre(x, mod, g1, g2, w1, w2, w_in, splits, tm):
    b, l, d = x.shape
    margs, mspecs = _mod_specs(mod, (0, 1, 2, 3, 4))
    params = [_row(g1), _row(g2), w1, w2, w_in]
    outs = pl.pallas_call(
        functools.partial(_pre_kernel, d_ff=w2.shape[0], splits=tuple(splits)),
        grid=(b, l // tm),
        in_specs=[_tok_spec(tm, d)] + mspecs + [_full_spec(a) for a in params],
        out_specs=[_tok_spec(tm, d)] + [_tok_spec(tm, w) for w in splits],
        out_shape=[jax.ShapeDtypeStruct((b, l, d), F32)] + [jax.ShapeDtypeStruct((b, l, w), F32) for w in splits],
        compiler_params=_cparams(2),
        name="dense_pre",
    )(x, *margs, *params)
    return outs[0], outs[1:]


def _head_ln(x, width, eps):
    parts = []
    for h in range(x.shape[-1] // width):
        xh = x[:, h * width:(h + 1) * width]
        mu = jnp.mean(xh, axis=-1, keepdims=True)
        xc = xh - mu
        var = jnp.mean(xc * xc, axis=-1, keepdims=True)
        parts.append(xc * lax.rsqrt(var + eps))
    return jnp.concatenate(parts, axis=-1)


def _seg_ln(x, e, width, eps):
    mu = _dot_exact_rhs(x, e) * (1.0 / width)
    xc = x - mu
    var = _dot_exact_rhs(xc * xc, e) * (1.0 / width)
    return xc * lax.rsqrt(var + eps)


def _post_tail(x, mix, gt2_ref, sh3_ref, sc3_ref, gt3_ref, g3_ref, w1_ref, w2_ref, fg_ref, o_ref, d_ff):
    x2 = x + gt2_ref[...] * mix
    h3 = (_rms(x2, g3_ref[...]) * (1.0 + sc3_ref[...]) + sh3_ref[...]).astype(BF16)
    y = x2 + 0.5 * gt3_ref[...] * _swiglu(h3, w1_ref, w2_ref, d_ff)
    if fg_ref is not None:
        y = _rms(y, fg_ref[...])
    o_ref[...] = y


def _post_even_kernel(x_ref, of_ref, ob_ref, ga_ref, y_ref, bon_ref, gb_ref,
                      gt2_ref, sh3_ref, sc3_ref, gt3_ref,
                      gnw_ref, lnw_ref, lnb_ref, seg_ref, g3_ref, wout_ref, w1_ref, w2_ref,
                      *rest, d_ff, dv, hs, final):
    fg_ref = rest[0] if final else None
    o_ref = rest[-1]
    wa = of_ref.shape[-1]
    o = _head_ln(of_ref[...] + ob_ref[...], dv, 1e-5) * gnw_ref[...]
    out_a = _silu(ga_ref[...]) * o
    y = _seg_ln(y_ref[...], seg_ref[...], hs, 64e-5) * lnw_ref[...] + lnb_ref[...]
    out_b = (y + bon_ref[...]) * gb_ref[...]
    mix = _bdot(out_a, wout_ref[0:wa, :]) + _bdot(out_b, wout_ref[wa:, :])
    _post_tail(x_ref[...], mix, gt2_ref, sh3_ref, sc3_ref, gt3_ref, g3_ref, w1_ref, w2_ref, fg_ref, o_ref, d_ff)


def _post_odd_kernel(x_ref, hf_ref, hb_ref, og_ref, od_ref,
                     gt2_ref, sh3_ref, sc3_ref, gt3_ref,
                     nw_ref, g3_ref, wout_ref, w1_ref, w2_ref, *rest, d_ff, dh, final):
    fg_ref = rest[0] if final else None
    o_ref = rest[-1]
    wc = hf_ref.shape[-1]
    h = _head_ln(hf_ref[...] + hb_ref[...], dh, 1e-5) * nw_ref[...]
    out_c = _sigmoid(og_ref[...]) * h
    mix = _bdot(out_c, wout_ref[0:wc, :]) + _bdot(od_ref[...], wout_ref[wc:, :])
    _post_tail(x_ref[...], mix, gt2_ref, sh3_ref, sc3_ref, gt3_ref, g3_ref, w1_ref, w2_ref, fg_ref, o_ref, d_ff)


def _dense_post(kernel_fn, name, x, toks, mod, params, final_g, tm, **kw):
    b, l, d = x.shape
    margs, mspecs = _mod_specs(mod, (5, 6, 7, 8))
    final = final_g is not None
    params = list(params) + ([_row(final_g)] if final else [])
    return pl.pallas_call(
        functools.partial(kernel_fn, final=final, **kw),
        grid=(b, l // tm),
        in_specs=([_tok_spec(tm, d)] + [_tok_spec(tm, a.shape[-1]) for a in toks] + mspecs
                  + [_full_spec(a) for a in params]),
        out_specs=_tok_spec(tm, d),
        out_shape=jax.ShapeDtypeStruct((b, l, d), F32),
        compiler_params=_cparams(2),
        name=name,
    )(x, *toks, *margs, *params)


def _rope_tables(n, d, reps):
    rows = n // GRID_W
    row = jnp.repeat(jnp.arange(rows), GRID_W).astype(F32)
    col = (jnp.arange(rows * GRID_W) % GRID_W).astype(F32)
    nf = d // 4
    inv = ROPE_BASE ** (-jnp.arange(nf, dtype=F32) / nf)
    ang_r = row[:, None] * inv[None, :]
    ang_c = col[:, None] * inv[None, :]
    cos = jnp.concatenate([jnp.cos(ang_r), jnp.cos(ang_r), jnp.cos(ang_c), jnp.cos(ang_c)], axis=-1)
    sin = jnp.concatenate([-jnp.sin(ang_r), jnp.sin(ang_r), -jnp.sin(ang_c), jnp.sin(ang_c)], axis=-1)
    return jnp.tile(cos, (1, reps)), jnp.tile(sin, (1, reps))


def _rope(x, cos, sin, nf):
    w = x.shape[-1]
    lane = lax.broadcasted_iota(jnp.int32, x.shape, 1)
    first = (lane % (2 * nf)) < nf
    partner = jnp.where(first, pltpu.roll(x, w - nf, axis=1), pltpu.roll(x, nf, axis=1))
    return x * cos + partner * sin


def _ret_kernel(ld_ref, *refs, h, dk, dv, c, nc, latent, emit_state):
    it = iter(refs)
    qkf_ref, vf_ref, qkb_ref, vb_ref = next(it), next(it), next(it), next(it)
    if latent:
        cosf_ref, sinf_ref, cosb_ref, sinb_ref, s0_ref = next(it), next(it), next(it), next(it), next(it)
    of_ref, ob_ref = next(it), next(it)
    sfin_ref = next(it) if emit_state else None
    z_scr = next(it)

    ci = pl.program_id(1)
    hk = h * dk

    @pl.when(ci == 0)
    def _():
        z_scr[...] = jnp.zeros_like(z_scr)
        if latent:
            for d in range(2):
                for hh in range(h):
                    z_scr[d, hh, hh * dk:(hh + 1) * dk, :] = s0_ref[d, hh]

    ii = lax.broadcasted_iota(jnp.int32, (c, c), 0)
    jj = lax.broadcasted_iota(jnp.int32, (c, c), 1)
    icol = lax.broadcasted_iota(jnp.int32, (c, 1), 0).astype(F32)
    lane = lax.broadcasted_iota(jnp.int32, (1, hk), 1)

    for d, (qk_ref, v_ref, o_ref) in enumerate(((qkf_ref, vf_ref, of_ref), (qkb_ref, vb_ref, ob_ref))):
        qk = qk_ref[...]
        if latent:
            cos_ref, sin_ref = (cosf_ref, sinf_ref) if d == 0 else (cosb_ref, sinb_ref)
            qk = _rope(qk, cos_ref[...], sin_ref[...], dk // 4)
        q = qk[:, :hk]
        k = qk[:, hk:] * (dk ** -0.5)
        v = v_ref[...]
        diff = (ii - jj) if d == 0 else (jj - ii)
        causal = diff >= 0
        dpos = jnp.where(causal, diff, 0).astype(F32)
        qe = (icol + 1.0) if d == 0 else (c - icol)
        ke = (c - 1.0 - icol) if d == 0 else icol
        outs = []
        for hh in range(h):
            lg = ld_ref[d, hh]
            msk = (lane // dk) == hh
            qh = jnp.where(msk, q, 0.0)
            kh = jnp.where(msk, k, 0.0)
            vh = v[:, hh * dv:(hh + 1) * dv]
            d_intra = jnp.where(causal, jnp.exp(lg * dpos), 0.0)
            att = _bdot_nt(qh, kh) * d_intra
            z = z_scr[d, hh]
            o = _bdot(att, vh) + _bdot(qh * jnp.exp(lg * qe), z)
            c_dec = jnp.exp(lg * jnp.full((1, 1), c, F32))
            z_scr[d, hh] = z * c_dec + _bdot_tn(kh * jnp.exp(lg * ke), vh)
            outs.append(o)
        o_ref[...] = jnp.concatenate(outs, axis=-1)

    if emit_state:
        @pl.when(ci == nc - 1)
        def _():
            for d in range(2):
                for hh in range(h):
                    sfin_ref[d, hh] = z_scr[d, hh, hh * dk:(hh + 1) * dk, :]


def retention_scan(qk, v, log_decay, s0, h, dk, dv, latent, emit_state):
    b, l, _ = qk.shape
    c = CHUNK
    nc = l // c
    hk = h * dk
    fwd = lambda w: pl.BlockSpec((None, c, w), lambda bb, ci: (bb, ci, 0))
    bwd = lambda w: pl.BlockSpec((None, c, w), lambda bb, ci: (bb, nc - 1 - ci, 0))
    args = [log_decay, qk, v, qk, v]
    specs = [pl.BlockSpec(memory_space=pltpu.SMEM), fwd(2 * hk), fwd(h * dv), bwd(2 * hk), bwd(h * dv)]
    if latent:
        cos, sin = _rope_tables(l, dk, 2 * h)
        args += [cos, sin, cos, sin, s0]
        tf = pl.BlockSpec((c, 2 * hk), lambda bb, ci: (ci, 0))
        tb = pl.BlockSpec((c, 2 * hk), lambda bb, ci: (nc - 1 - ci, 0))
        specs += [tf, tf, tb, tb, pl.BlockSpec((None, 2, h, dk, dv), lambda bb, ci: (bb, 0, 0, 0, 0))]
    out_shape = [jax.ShapeDtypeStruct((b, l, h * dv), F32)] * 2
    out_specs = [fwd(h * dv), bwd(h * dv)]
    if emit_state:
        out_shape.append(jax.ShapeDtypeStruct((b, 2, h, dk, dv), F32))
        out_specs.append(pl.BlockSpec((None, 2, h, dk, dv), lambda bb, ci: (bb, 0, 0, 0, 0)))
    return pl.pallas_call(
        functools.partial(_ret_kernel, h=h, dk=dk, dv=dv, c=c, nc=nc, latent=latent, emit_state=emit_state),
        grid=(b, nc),
        in_specs=specs,
        out_specs=out_specs,
        out_shape=out_shape,
        scratch_shapes=[pltpu.VMEM((2, h, hk, dv), F32)],
        compiler_params=_cparams(2),
        name="retention_scan",
    )(*args)


def _rwkv_prep_kernel(p_ref, prev_ref, next_ref, mu_ref, w0_ref, a0_ref, w2_ref, a2_ref, g2_ref,
                      kk_ref, ka_ref, rk_ref, seg_ref, s_out, g_out, bon_out, *, tm, nt, wb):
    i = pl.program_id(1)
    p = p_ref[...]
    row = lax.broadcasted_iota(jnp.int32, p.shape, 0)
    prow = jnp.where(i == 0, 0.0, prev_ref[SUBLANES - 1:SUBLANES, :])
    nrow = jnp.where(i == nt - 1, 0.0, next_ref[0:1, :])
    prev = jnp.where(row == 0, prow, pltpu.roll(p, 1, axis=0))
    nxt = jnp.where(row == tm - 1, nrow, pltpu.roll(p, tm - 1, axis=0))
    ps = p + mu_ref[...] * (0.5 * (prev + nxt) - p)
    r = ps[:, 0:wb]
    k = ps[:, wb:2 * wb]
    v = ps[:, 2 * wb:3 * wb]
    lora = ps[:, 3 * wb:]
    tl = jnp.tanh(lora)
    seg = seg_ref[...]
    kk = k * kk_ref[...]
    nrm = jnp.sqrt(_dot_exact_rhs(kk * kk, seg))
    kk = kk / jnp.maximum(nrm, 1e-12)
    s_out[0] = r
    s_out[1] = v
    s_out[2] = -kk
    g_out[...] = _bdot(_sigmoid(lora), g2_ref[...])
    bon_out[...] = _dot_exact_rhs(r * k * rk_ref[...], seg) * v
    for d in range(2):
        w_log = -_softplus(-(w0_ref[d:d + 1, :] + _bdot(tl, w2_ref[d]))) - 0.5
        s_out[3 + 3 * d] = jnp.exp(-jnp.exp(w_log))
        a = _sigmoid(a0_ref[d:d + 1, :] + _bdot(lora, a2_ref[d]))
        s_out[4 + 3 * d] = k * (1.0 + (a - 1.0) * ka_ref[...])
        s_out[5 + 3 * d] = kk * a


def rwkv_prep(p, mu, w0, w2p, a0, a2p, g2p, k_k, k_a, r_k, seg, wb, tm):
    b, l, pw = p.shape
    nt = l // tm
    r8 = tm // SUBLANES
    nb8 = l // SUBLANES
    params = [_row(mu), w0, a0, w2p, a2p, g2p, _row(k_k), _row(k_a), _row(r_k), seg]
    return pl.pallas_call(
        functools.partial(_rwkv_prep_kernel, tm=tm, nt=nt, wb=wb),
        grid=(b, nt),
        in_specs=[_tok_spec(tm, pw),
                  pl.BlockSpec((None, SUBLANES, pw), lambda bb, i: (bb, jnp.maximum(i * r8 - 1, 0), 0)),
                  pl.BlockSpec((None, SUBLANES, pw), lambda bb, i: (bb, jnp.minimum((i + 1) * r8, nb8 - 1), 0))]
                 + [_full_spec(a) for a in params],
        out_specs=[pl.BlockSpec((9, None, tm, wb), lambda bb, i: (0, bb, i, 0))] + [_tok_spec(tm, wb)] * 2,
        out_shape=[jax.ShapeDtypeStruct((9, b, l, wb), F32)] + [jax.ShapeDtypeStruct((b, l, wb), F32)] * 2,
        compiler_params=_cparams(2),
        name="rwkv_prep",
    )(p, p, p, *params)


ROW_PITCH = 72
CHAIN_TILE = 128
N_CHAIN_ARRAYS = 6


def _chain_groups(b, h):
    per_dir = b * h
    if 2 * per_dir == LANES:
        return 1, True
    assert per_dir % LANES == 0
    return 2 * per_dir // LANES, False


def _to_chains_kernel(lo_ref, hi_ref, o_ref, t_scr, *, n, h, nb, tt):
    for half, ref in enumerate((lo_ref, hi_ref)):
        for bb in range(nb):
            xt = ref[bb].T
            for hh in range(h):
                t_scr[half * nb + bb, hh * ROW_PITCH:hh * ROW_PITCH + n, :] = xt[hh * n:(hh + 1) * n]
    for c in range(n):
        q = jnp.concatenate([t_scr[k, pl.ds(c, h, stride=ROW_PITCH), :] for k in range(2 * nb)], axis=0)
        o_ref[pl.ds(c, tt, stride=ROW_PITCH), :] = q.T
    pad = jnp.zeros((ROW_PITCH - n, LANES), F32)
    for t in range(tt):
        o_ref[t * ROW_PITCH + n:(t + 1) * ROW_PITCH, :] = pad


def to_chains(stack, h):
    _, b, l, w = stack.shape
    n = w // h
    nb = LANES // (2 * h)
    tt = CHAIN_TILE
    ng, mixed = _chain_groups(b, h)
    src1 = lambda a: jnp.where(a < 3, a, a + 3)
    if mixed:
        lo_map = lambda a, g, ti: (a, 0, ti, 0)
        hi_map = lambda a, g, ti: (src1(a), 0, ti, 0)
    else:
        src = lambda a, g: jnp.where(g < ng // 2, a, src1(a))
        blocks_per_dir = ng // 2
        lo_map = lambda a, g, ti: (src(a, g), 2 * (g % blocks_per_dir), ti, 0)
        hi_map = lambda a, g, ti: (src(a, g), 2 * (g % blocks_per_dir) + 1, ti, 0)
    out = pl.pallas_call(
        functools.partial(_to_chains_kernel, n=n, h=h, nb=nb, tt=tt),
        grid=(N_CHAIN_ARRAYS, ng, l // tt),
        in_specs=[pl.BlockSpec((None, nb, tt, w), lo_map), pl.BlockSpec((None, nb, tt, w), hi_map)],
        out_specs=pl.BlockSpec((None, None, tt * ROW_PITCH, LANES), lambda a, g, ti: (a, g, ti, 0)),
        out_shape=jax.ShapeDtypeStruct((N_CHAIN_ARRAYS, ng, l * ROW_PITCH, LANES), F32),
        scratch_shapes=[pltpu.VMEM((2 * nb, h * ROW_PITCH, tt), F32)],
        compiler_params=_cparams(3),
        name="to_chains",
    )(stack, stack)
    return out.reshape(N_CHAIN_ARRAYS, ng, l, ROW_PITCH, LANES)


def _from_chains_kernel(a_ref, b_ref, o_ref, t_scr, *, n, h, nb, tt, mixed):
    lane = lax.broadcasted_iota(jnp.int32, (tt, LANES), 1)
    nk = LANES // h
    for c in range(n):
        va = a_ref[pl.ds(c, tt, stride=ROW_PITCH), :]
        vb = b_ref[pl.ds(c, tt, stride=ROW_PITCH), :]
        tiles = [jnp.where(lane >= LANES // 2, vb, va)] if mixed else [va, vb]
        for idx, v in enumerate(tiles):
            vt = v.T
            for k in range(nk):
                t_scr[idx * nk + k, pl.ds(c, h, stride=ROW_PITCH), :] = vt[k * h:(k + 1) * h]
    for bb in range(nb):
        k0, k1 = (bb, nb + bb) if mixed else (bb, nk + bb)
        parts = [t_scr[k0, hh * ROW_PITCH:hh * ROW_PITCH + n, :] + t_scr[k1, hh * ROW_PITCH:hh * ROW_PITCH + n, :]
                 for hh in range(h)]
        o_ref[bb] = jnp.concatenate(parts, axis=0).T


def from_chains(yf, yb, b, h, n):
    ng, l, _, _ = yf.shape
    _, mixed = _chain_groups(b, h)
    tt = CHAIN_TILE
    nk = LANES // h
    nb = nk // 2 if mixed else nk
    yf2 = yf.reshape(ng, l * ROW_PITCH, LANES)
    yb2 = yb.reshape(ng, l * ROW_PITCH, LANES)
    if mixed:
        a_map = lambda gb, ti: (0, ti, 0)
        b_map = a_map
    else:
        a_map = lambda gb, ti: (gb, ti, 0)
        b_map = lambda gb, ti: (ng // 2 + gb, ti, 0)
    spec = lambda m: pl.BlockSpec((None, tt * ROW_PITCH, LANES), m)
    return pl.pallas_call(
        functools.partial(_from_chains_kernel, n=n, h=h, nb=nb, tt=tt, mixed=mixed),
        grid=(b // nb, l // tt),
        in_specs=[spec(a_map), spec(b_map)],
        out_specs=pl.BlockSpec((nb, tt, h * n), lambda gb, ti: (gb, ti, 0)),
        out_shape=jax.ShapeDtypeStruct((b, l, h * n), F32),
        scratch_shapes=[pltpu.VMEM(((1 if mixed else 2) * nk, h * ROW_PITCH, tt), F32)],
        compiler_params=_cparams(2),
        name="from_chains",
    )(yf2, yb2)


def _rwkv_scan_kernel(*refs, n, tt, nt, ng, mixed, has_state, emit_state):
    it = iter(refs)
    fw = [next(it) for _ in range(N_CHAIN_ARRAYS)]
    bw = [next(it) for _ in range(N_CHAIN_ARRAYS)]
    a_next_ref, a_prev_ref = next(it), next(it)
    s0_ref = next(it) if has_state else None
    yf_ref, yb_ref = next(it), next(it)
    sfin_ref = next(it) if emit_state else None
    s_scr, sa_scr, gam_scr = next(it), next(it), next(it)
    r_buf, v_buf, a_buf, an_buf, k_buf, b_buf = (next(it) for _ in range(N_CHAIN_ARRAYS))
    g = pl.program_id(0)
    ti = pl.program_id(1)

    lane = lax.broadcasted_iota(jnp.int32, (n, LANES), 1)
    split = LANES // 2 if mixed else jnp.where(g < ng // 2, LANES, 0)
    is_bwd = lane >= split

    def sel(idx, s):
        return jnp.where(is_bwd, bw[idx][tt - 1 - s, 0:n, :], fw[idx][s, 0:n, :])

    for s in range(tt):
        a_buf[s] = sel(2, s)
    a_after = jnp.where(is_bwd, a_prev_ref[0, 0:n, :], a_next_ref[0, 0:n, :])
    a_buf[tt] = jnp.where(ti == nt - 1, 0.0, a_after)
    gam = jnp.ones((n, LANES), F32)
    for s in range(tt):
        gam = gam * sel(3, s)
        inv = 1.0 / gam
        r_buf[s] = sel(0, s) * gam
        v_buf[s] = sel(1, s)
        k_buf[s] = sel(4, s) * inv
        b_buf[s] = sel(5, s) * inv
        an_buf[s] = a_buf[s + 1] * gam
    gam_scr[...] = gam

    @pl.when(ti == 0)
    def _():
        if has_state:
            acc = jnp.zeros(sa_scr.shape, F32)
            for j in range(n):
                sj = s0_ref[j]
                s_scr[j] = sj
                acc = acc + sj * a_buf[0, j:j + 1, :]
            sa_scr[...] = acc
        else:
            s_scr[...] = jnp.zeros_like(s_scr)
            sa_scr[...] = jnp.zeros_like(sa_scr)

    pad = jnp.zeros((ROW_PITCH - n, LANES), F32)

    def step(t, sa):
        vt = v_buf[t]
        y = jnp.zeros_like(sa)
        san = jnp.zeros_like(sa)
        for j in range(n):
            new = s_scr[j] + (sa * b_buf[t, j:j + 1, :] + vt * k_buf[t, j:j + 1, :])
            s_scr[j] = new
            y = y + new * r_buf[t, j:j + 1, :]
            san = san + new * an_buf[t, j:j + 1, :]
        yf_ref[t, 0:n, :] = y
        yf_ref[t, n:, :] = pad
        yb_ref[tt - 1 - t, 0:n, :] = y
        yb_ref[tt - 1 - t, n:, :] = pad
        return san

    sa_scr[...] = lax.fori_loop(0, tt, step, sa_scr[...])
    for j in range(n):
        s_scr[j] = s_scr[j] * gam_scr[j:j + 1, :]

    if emit_state:
        @pl.when(ti == nt - 1)
        def _():
            sfin_ref[...] = s_scr[...]


def rwkv_scan(chains, s0, n, mixed, emit_state, tt=32):
    _, ng, l, rp, lanes = chains.shape
    nt = l // tt
    has_state = s0 is not None

    def seq(a, rev):
        if rev:
            return pl.BlockSpec((None, None, tt, rp, lanes), lambda g, ti: (a, g, nt - 1 - ti, 0, 0))
        return pl.BlockSpec((None, None, tt, rp, lanes), lambda g, ti: (a, g, ti, 0, 0))

    one = lambda m: pl.BlockSpec((None, None, 1, rp, lanes), m)
    st = pl.BlockSpec((None, n, n, lanes), lambda g, ti: (g, 0, 0, 0))
    yspec_f = pl.BlockSpec((None, tt, rp, lanes), lambda g, ti: (g, ti, 0, 0))
    yspec_b = pl.BlockSpec((None, tt, rp, lanes), lambda g, ti: (g, nt - 1 - ti, 0, 0))
    args = [chains] * (2 * N_CHAIN_ARRAYS + 2)
    specs = ([seq(a, False) for a in range(N_CHAIN_ARRAYS)] + [seq(a, True) for a in range(N_CHAIN_ARRAYS)]
             + [one(lambda g, ti: (2, g, jnp.minimum((ti + 1) * tt, l - 1), 0, 0)),
                one(lambda g, ti: (2, g, jnp.maximum((nt - 1 - ti) * tt - 1, 0), 0, 0))])
    if has_state:
        args.append(s0)
        specs.append(st)
    out_shape = [jax.ShapeDtypeStruct((ng, l, rp, lanes), F32)] * 2
    out_specs = [yspec_f, yspec_b]
    if emit_state:
        out_shape.append(jax.ShapeDtypeStruct((ng, n, n, lanes), F32))
        out_specs.append(st)
    seq_buf = pltpu.VMEM((tt, n, lanes), F32)
    return pl.pallas_call(
        functools.partial(_rwkv_scan_kernel, n=n, tt=tt, nt=nt, ng=ng, mixed=mixed,
                          has_state=has_state, emit_state=emit_state),
        grid=(ng, nt),
        in_specs=specs,
        out_specs=out_specs,
        out_shape=out_shape,
        scratch_shapes=[pltpu.VMEM((n, n, lanes), F32), pltpu.VMEM((n, lanes), F32), pltpu.VMEM((n, lanes), F32),
                        seq_buf, seq_buf, pltpu.VMEM((tt + 1, n, lanes), F32), seq_buf, seq_buf, seq_buf],
        compiler_params=_cparams(2),
        name="rwkv_scan",
    )(*args)


def _state_to_chains(s):
    b, _, h, n, _ = s.shape
    ng = (2 * b * h) // LANES
    return s.transpose(4, 3, 1, 0, 2).reshape(n, n, ng, LANES).transpose(2, 0, 1, 3)


def _state_from_chains(s, b, h):
    ng, n, _, _ = s.shape
    return s.transpose(1, 2, 0, 3).reshape(n, n, 2, b, h).transpose(3, 2, 4, 1, 0)


def _mlstm_kernel(*refs, h, dh, c, nc, has_state, emit_state):
    it = iter(refs)
    qkvf_ref, gf_ref, qkvb_ref, gb_ref, bias_ref = (next(it) for _ in range(5))
    if has_state:
        c0_ref, n0_ref, m0_ref = next(it), next(it), next(it)
    hf_ref, hb_ref = next(it), next(it)
    if emit_state:
        cfin_ref, nfin_ref, mfin_ref = next(it), next(it), next(it)
    cma_scr, m_scr = next(it), next(it)
    ci = pl.program_id(1)
    row0 = lax.broadcasted_iota(jnp.int32, (dh, dh), 0) == 0

    @pl.when(ci == 0)
    def _():
        if has_state:
            for s in range(2 * h):
                cma_scr[s, :, 0:dh] = c0_ref[s // h, s % h]
                cma_scr[s, :, dh:] = jnp.where(row0, n0_ref[s:s + 1, :], 0.0).T
            m_scr[...] = m0_ref[...]
        else:
            cma_scr[...] = jnp.zeros_like(cma_scr)
            m_scr[...] = jnp.zeros_like(m_scr)

    ii = lax.broadcasted_iota(jnp.int32, (c, c), 0)
    jj = lax.broadcasted_iota(jnp.int32, (c, c), 1)
    gl_lane = lax.broadcasted_iota(jnp.int32, (c, LANES), 1)
    is_fg = (gl_lane >= 2 * h) & (gl_lane < 4 * h)
    ones_col = (lax.broadcasted_iota(jnp.int32, (c, dh), 1) == 0).astype(F32)

    for d, (qkv_ref, g_ref, o_ref) in enumerate(((qkvf_ref, gf_ref, hf_ref), (qkvb_ref, gb_ref, hb_ref))):
        causal = (ii >= jj) if d == 0 else (ii <= jj)
        gx = g_ref[...] + bias_ref[...]
        gl = jnp.where(is_fg, -_softplus(-gx), gx)
        bcum = _dot_exact_lhs(causal.astype(BF16), gl)
        glt = gl.T
        brow = _dot_exact_rhs(glt, (~causal).astype(BF16) + (ii == jj).astype(BF16))
        qkv = qkv_ref[...]
        kt_all = (qkv[:, h * dh:2 * h * dh] * (dh ** -0.5)).T
        end = c - 1 if d == 0 else 0
        outs = []
        for hh in range(h):
            ig_c = d * h + hh
            fg_c = 2 * h + d * h + hh
            sr = d * h + hh
            q = qkv[:, hh * dh:(hh + 1) * dh]
            k = qkv[:, (h + hh) * dh:(h + hh + 1) * dh] * (dh ** -0.5)
            v_aug = jnp.concatenate([qkv[:, (2 * h + hh) * dh:(2 * h + hh + 1) * dh], ones_col], axis=-1)
            kt = kt_all[hh * dh:(hh + 1) * dh, :]
            b_col = bcum[:, fg_c:fg_c + 1]
            b_row = brow[fg_c:fg_c + 1, :]
            c_row = glt[ig_c:ig_c + 1, :] - b_row
            m = m_scr[sr:sr + 1, 0:1]
            cma = cma_scr[sr]
            dmat = jnp.where(causal, c_row, -jnp.inf)
            g_col = jnp.maximum(m, jnp.max(dmat, axis=-1, keepdims=True))
            s = _bdot_nt(q, k) * jnp.exp(dmat - g_col)
            nd = _bdot(s, v_aug) + jnp.exp(m - g_col) * _bdot(q, cma)
            den = nd[:, dh:dh + 1]
            outs.append(nd[:, 0:dh] / jnp.maximum(jnp.abs(den), jnp.exp(-(b_col + g_col))))
            bl = b_row[:, end:end + 1]
            wl_row = bl + c_row
            m_new = jnp.maximum(bl + m, jnp.max(wl_row, axis=-1, keepdims=True))
            cma_scr[sr] = jnp.exp(bl + m - m_new) * cma + _bdot(kt * jnp.exp(wl_row - m_new), v_aug)
            m_scr[sr:sr + 1, :] = jnp.broadcast_to(m_new, (1, LANES))
        o_ref[...] = jnp.concatenate(outs, axis=-1)

    if emit_state:
        @pl.when(ci == nc - 1)
        def _():
            for s in range(2 * h):
                cfin_ref[s // h, s % h] = cma_scr[s, :, 0:dh]
                nfin_ref[s:s + 1, :] = cma_scr[s, :, dh:].T[0:1, :]
            mfin_ref[...] = m_scr[...]


def mlstm_scan(qkv, gates, bias_row, c0, n0, m0, h, dh, emit_state):
    b, l, _ = qkv.shape
    c = CHUNK
    nc = l // c
    has_state = c0 is not None
    w = qkv.shape[-1]
    fwd = lambda ww: pl.BlockSpec((None, c, ww), lambda bb, ci: (bb, ci, 0))
    bwd = lambda ww: pl.BlockSpec((None, c, ww), lambda bb, ci: (bb, nc - 1 - ci, 0))
    cspec = pl.BlockSpec((None, 2, h, dh, dh), lambda bb, ci: (bb, 0, 0, 0, 0))
    rspec = pl.BlockSpec((None, 2 * h, LANES), lambda bb, ci: (bb, 0, 0))
    args = [qkv, gates, qkv, gates, bias_row]
    specs = [fwd(w), fwd(LANES), bwd(w), bwd(LANES), _full_spec(bias_row)]
    if has_state:
        args += [c0, n0.reshape(b, 2 * h, dh), jnp.broadcast_to(m0.reshape(b, 2 * h, 1), (b, 2 * h, LANES))]
        specs += [cspec, rspec, rspec]
    out_shape = [jax.ShapeDtypeStruct((b, l, h * dh), F32)] * 2
    out_specs = [fwd(h * dh), bwd(h * dh)]
    if emit_state:
        out_shape += [jax.ShapeDtypeStruct((b, 2, h, dh, dh), F32),
                      jax.ShapeDtypeStruct((b, 2 * h, dh), F32), jax.ShapeDtypeStruct((b, 2 * h, LANES), F32)]
        out_specs += [cspec, rspec, rspec]
    return pl.pallas_call(
        functools.partial(_mlstm_kernel, h=h, dh=dh, c=c, nc=nc, has_state=has_state, emit_state=emit_state),
        grid=(b, nc),
        in_specs=specs,
        out_specs=out_specs,
        out_shape=out_shape,
        scratch_shapes=[pltpu.VMEM((2 * h, dh, 2 * dh), F32), pltpu.VMEM((2 * h, LANES), F32)],
        compiler_params=_cparams(2),
        name="mlstm_scan",
    )(*args)


def _attn_prep_kernel(*refs, hq, hkv, hd, latent):
    it = iter(refs)
    q_ref, kv_ref, gq_ref, gk_ref, segq_ref, segk_ref = (next(it) for _ in range(6))
    if latent:
        cq_ref, sq_ref, ck_ref, sk_ref = (next(it) for _ in range(4))
    qo_ref, ko_ref, vo_ref = next(it), next(it), next(it)
    q = q_ref[...]
    kv = kv_ref[...]
    kw = hkv * hd
    k = kv[:, :kw]
    v = kv[:, kw:]
    qn = q * lax.rsqrt(_dot_exact_rhs(q * q, segq_ref[...]) * (1.0 / hd) + RMS_EPS) * gq_ref[...]
    kn = k * lax.rsqrt(_dot_exact_rhs(k * k, segk_ref[...]) * (1.0 / hd) + RMS_EPS) * gk_ref[...]
    if latent:
        qn = _rope(qn, cq_ref[...], sq_ref[...], hd // 4)
        kn = _rope(kn, ck_ref[...], sk_ref[...], hd // 4)
    qo_ref[...] = qn * (hd ** -0.5 * LOG2E)
    for j in range(hkv):
        ko_ref[j] = kn[:, j * hd:(j + 1) * hd]
        vo_ref[j] = v[:, j * hd:(j + 1) * hd]


def attn_prep(q, kv, qk_gain, hq, hkv, hd, latent, tm):
    b, l, wq = q.shape
    wkv = kv.shape[-1]
    kw = hkv * hd
    params = [jnp.tile(qk_gain[0], hq).reshape(1, wq), jnp.tile(qk_gain[1], hkv).reshape(1, kw),
              _seg_ones(wq, hd), _seg_ones(kw, hd)]
    args = [q, kv] + params
    specs = [_tok_spec(tm, wq), _tok_spec(tm, wkv)] + [_full_spec(a) for a in params]
    if latent:
        cq, sq = _rope_tables(l, hd, hq)
        ck, sk = _rope_tables(l, hd, hkv)
        args += [cq, sq, ck, sk]
        specs += [pl.BlockSpec((tm, wq), lambda bb, i: (i, 0))] * 2 + [pl.BlockSpec((tm, kw), lambda bb, i: (i, 0))] * 2
    kvspec = pl.BlockSpec((None, hkv, tm, hd), lambda bb, i: (bb, 0, i, 0))
    return pl.pallas_call(
        functools.partial(_attn_prep_kernel, hq=hq, hkv=hkv, hd=hd, latent=latent),
        grid=(b, l // tm),
        in_specs=specs,
        out_specs=[_tok_spec(tm, wq), kvspec, kvspec],
        out_shape=[jax.ShapeDtypeStruct((b, l, wq), F32)] + [jax.ShapeDtypeStruct((b, hkv, l, hd), F32)] * 2,
        compiler_params=_cparams(2),
        name="attn_prep",
    )(*args)


def _attn_kernel(*refs, g, hd, tq, nsub, kb, has_ctx):
    it = iter(refs)
    q_ref, k_ref, v_ref = next(it), next(it), next(it)
    if has_ctx:
        ck_ref, cv_ref = next(it), next(it)
    o_ref = next(it)
    s_scr = next(it)
    sources =([(ck_ref, cv_ref)] if has_ctx else []) + [(k_ref, v_ref)]
    maxes = []
    for u in range(nsub):
        q = q_ref[u * tq:(u + 1) * tq, :]
        qs = jnp.concatenate([q[:, i * hd:(i + 1) * hd] for i in range(g)], axis=0).astype(BF16)
        m = None
        row = 0
        for kr, _ in sources:
            for j in range(kr.shape[0] // kb):
                st = _bdot_nt(kr[j * kb:(j + 1) * kb, :], qs)
                s_scr[u, row:row + kb, :] = st
                bm = jnp.max(st, axis=0, keepdims=True)
                m = bm if m is None else jnp.maximum(m, bm)
                row += kb
        maxes.append(m)
    for u in range(nsub):
        acc = jnp.zeros((hd, g * tq), F32)
        den = jnp.zeros((1, g * tq), F32)
        row = 0
        for _, vr in sources:
            for j in range(vr.shape[0] // kb):
                p = jnp.exp2(s_scr[u, row:row + kb, :] - maxes[u])
                den = den + jnp.sum(p, axis=0, keepdims=True)
                acc = acc + _bdot_tn(vr[j * kb:(j + 1) * kb, :], p)
                row += kb
        ot = acc / den
        o = jnp.concatenate([ot, jnp.zeros((LANES - hd, g * tq), F32)], axis=0).T
        for i in range(g):
            o_ref[u * tq:(u + 1) * tq, i * hd:(i + 1) * hd] = o[i * tq:(i + 1) * tq, 0:hd]


def attention(q, k, v, ctx_k, ctx_v, layer_o, hq, hkv, hd, tq, nsub):
    b, l, wq = q.shape
    g = hq // hkv
    has_ctx = ctx_k is not None
    qspec = pl.BlockSpec((None, nsub * tq, g * hd), lambda bb, j, i: (bb, i, j))
    kvspec = pl.BlockSpec((None, None, l, hd), lambda bb, j, i: (bb, j, 0, 0))
    args = [q, k, v]
    specs = [qspec, kvspec, kvspec]
    n_keys = l
    if has_ctx:
        s = ctx_k.shape[3]
        n_keys += s
        cspec = pl.BlockSpec((None, None, None, s, hd), lambda bb, j, i: (bb, layer_o, j, 0, 0))
        args += [ctx_k, ctx_v]
        specs += [cspec, cspec]
    kb = min(256, l)
    return pl.pallas_call(
        functools.partial(_attn_kernel, g=g, hd=hd, tq=tq, nsub=nsub, kb=kb, has_ctx=has_ctx),
        grid=(b, hkv, l // (nsub * tq)),
        in_specs=specs,
        out_specs=qspec,
        out_shape=jax.ShapeDtypeStruct((b, l, wq), F32),
        scratch_shapes=[pltpu.VMEM((nsub, n_keys, g * tq), F32)],
        compiler_params=_cparams(3),
        name="gqa_attention",
    )(*args)


def _pad_rows(w, lo, total):
    return jnp.zeros((total, w.shape[-1]), w.dtype).at[lo:lo + w.shape[0]].set(w)


def kernel(x_prompt, x_sample, state_ret, state_rwkv, state_mlstm_c, state_mlstm_n, state_mlstm_m, cache_k, cache_v, c, c_ctx, ada_w, ada_b, norm_g, ffn_w1, ffn_w2, w_in_even, w_out_even, ret_log_decay, ret_gn_w, rwkv_mu, rwkv_w0, rwkv_w2, rwkv_a0, rwkv_a2, rwkv_g2, rwkv_k_k, rwkv_k_a, rwkv_r_k, rwkv_ln_w, rwkv_ln_b, w_in_odd, w_out_odd, mlstm_i_bias, mlstm_f_bias, mlstm_norm_w, attn_qk_norm, final_norm):
    depth = ada_w.shape[0]
    d_model = x_prompt.shape[-1]
    h_a, dk_a, dv_a = state_ret.shape[3:]
    h_b, hs_b = state_rwkv.shape[3:5]
    h_c, dh_c = state_mlstm_c.shape[3:5]
    hkv_d, hd_d = cache_k.shape[2], cache_k.shape[4]
    wa, wb, wc = h_a * dv_a, h_b * hs_b, h_c * dh_c
    wd = w_out_odd.shape[1] - wc
    hq_d = wd // hd_d
    kvw = hkv_d * hd_d
    n_dec = c.shape[0]
    lora_w, lora_a, lora_g = rwkv_w2.shape[2], rwkv_a2.shape[2], rwkv_g2.shape[1]
    lora_tot = lora_w + lora_a + lora_g

    rows = -(-(n_dec + 1) // SUBLANES) * SUBLANES
    cond = jnp.zeros((rows, d_model), F32).at[:n_dec].set(c).at[n_dec].set(c_ctx)
    mod = modulation_all(cond, ada_w, ada_b).reshape(depth, rows, N_MOD, d_model)

    seg_b = _seg_ones(wb, hs_b)
    streams = {
        "prompt": dict(x=x_prompt, latent=False, tm=x_prompt.shape[1], tq=128, nsub=2),
        "sample": dict(x=x_sample, latent=True, tm=256, tq=128, nsub=2),
    }
    new_states = {}

    for l in range(depth):
        w1a, w1b = ffn_w1[l, 0].astype(BF16), ffn_w1[l, 1].astype(BF16)
        w2a, w2b = ffn_w2[l, 0].astype(BF16), ffn_w2[l, 1].astype(BF16)
        fin = final_norm if l == depth - 1 else None
        if l % 2 == 0:
            e = l // 2
            w_in = w_in_even[e].astype(BF16)
            w_out = w_out_even[e].astype(BF16)
            splits = (2 * h_a * dk_a, wa, wa, 3 * wb + lora_tot)
            w2p = jnp.stack([_pad_rows(rwkv_w2[e, d], 0, lora_tot) for d in range(2)]).astype(BF16)
            a2p = jnp.stack([_pad_rows(rwkv_a2[e, d], lora_w, lora_tot) for d in range(2)]).astype(BF16)
            g2p = _pad_rows(rwkv_g2[e], lora_w + lora_a, lora_tot).astype(BF16)
        else:
            o = l // 2
            wi = w_in_odd[o]
            g0 = 3 * wc
            p_c = g0 + 4 * h_c + wc
            gates_w = jnp.zeros((d_model, LANES), F32).at[:, :4 * h_c].set(wi[:, g0:g0 + 4 * h_c])
            w_in = jnp.concatenate([wi[:, :g0], wi[:, g0 + 4 * h_c:p_c], wi[:, p_c:p_c + wd],
                                    wi[:, p_c + wd:], gates_w], axis=1).astype(BF16)
            w_out = w_out_odd[o].astype(BF16)
            splits = (3 * wc, wc, wd, 2 * kvw, LANES)
            bias_row = jnp.zeros((1, LANES), F32).at[0, :2 * h_c].set(mlstm_i_bias[o].reshape(-1))
            bias_row = bias_row.at[0, 2 * h_c:4 * h_c].set(mlstm_f_bias[o].reshape(-1))

        for name, st in streams.items():
            x, latent, tm = st["x"], st["latent"], st["tm"]
            b, seq, _ = x.shape
            m = mod[l, :n_dec] if latent else mod[l, n_dec:n_dec + 1]
            emit = not latent
            x1, parts = dense_pre(x, m, norm_g[l, 0], norm_g[l, 1], w1a, w2a, w_in, splits, tm)
            if l % 2 == 0:
                qk_a, v_a, g_a, p_b = parts
                ret = retention_scan(qk_a, v_a, ret_log_decay[e], state_ret[:, e] if latent else None,
                                     h_a, dk_a, dv_a, latent, emit)
                stack, g_b, bonus = rwkv_prep(
                    p_b, rwkv_mu[e], rwkv_w0[e], w2p, rwkv_a0[e], a2p, g2p, rwkv_k_k[e], rwkv_k_a[e],
                    rwkv_r_k[e].reshape(-1), seg_b, wb, min(tm, 256))
                _, mixed = _chain_groups(b, h_b)
                scan = rwkv_scan(to_chains(stack, h_b),
                                 _state_to_chains(state_rwkv[:, e]) if latent else None, hs_b, mixed, emit)
                y = from_chains(scan[0], scan[1], b, h_b, hs_b)
                if emit:
                    new_states.setdefault("ret", []).append(ret[2])
                    new_states.setdefault("rwkv", []).append(_state_from_chains(scan[2], b, h_b))
                x = _dense_post(_post_even_kernel, "dense_post_even", x1,
                                [ret[0], ret[1], g_a, y, bonus, g_b], m,
                                [_row(ret_gn_w[e]), _row(rwkv_ln_w[e]), _row(rwkv_ln_b[e]), seg_b,
                                 _row(norm_g[l, 2]), w_out, w1b, w2b],
                                fin, tm, d_ff=w2b.shape[0], dv=dv_a, hs=hs_b)
            else:
                qkv_c, og, q_d, kv_d, gates = parts
                ml = mlstm_scan(qkv_c, gates, bias_row,
                                state_mlstm_c[:, o] if latent else None,
                                state_mlstm_n[:, o] if latent else None,
                                state_mlstm_m[:, o] if latent else None, h_c, dh_c, emit)
                qn, kn, vn = attn_prep(q_d, kv_d, attn_qk_norm[o], hq_d, hkv_d, hd_d, latent, min(tm, 256))
                out_d = attention(qn, kn, vn, cache_k if latent else None, cache_v if latent else None, o,
                                  hq_d, hkv_d, hd_d, st["tq"], st["nsub"])
                if emit:
                    new_states.setdefault("mc", []).append(ml[2])
                    new_states.setdefault("mn", []).append(ml[3].reshape(b, 2, h_c, dh_c))
                    new_states.setdefault("mm", []).append(ml[4][:, :, 0].reshape(b, 2, h_c))
                    new_states.setdefault("k", []).append(kn)
                    new_states.setdefault("v", []).append(vn)
                x = _dense_post(_post_odd_kernel, "dense_post_odd", x1, [ml[0], ml[1], og, out_d], m,
                                [_row(mlstm_norm_w[o]), _row(norm_g[l, 2]), w_out, w1b, w2b],
                                fin, tm, d_ff=w2b.shape[0], dh=dh_c)
            st["x"] = x

    stack = lambda key: jnp.stack(new_states[key], axis=1)
    return (streams["prompt"]["x"], streams["sample"]["x"], stack("ret"), stack("rwkv"), stack("mc"),
            stack("mn"), stack("mm"), stack("k"), stack("v"))
```

```python
import functools

import jax
import jax.numpy as jnp
from jax import lax
from jax.experimental import pallas as pl
from jax.experimental.pallas import tpu as pltpu

F32 = jnp.float32
BF16 = jnp.bfloat16

GRID_W = 64
CHUNK = 128
ROPE_BASE = 10000.0
RMS_EPS = 1e-6
LOG2E = 1.4426950408889634
N_MOD = 9
LANES = 128
SUBLANES = 8
VMEM_LIMIT = 56 * 1024 * 1024


def _cparams(n_axes):
    return pltpu.CompilerParams(dimension_semantics=("arbitrary",) * n_axes, vmem_limit_bytes=VMEM_LIMIT)


def _bdot(a, b):
    return jnp.dot(a.astype(BF16), b.astype(BF16), preferred_element_type=F32)


def _bdot_nt(a, b):
    return lax.dot_general(a.astype(BF16), b.astype(BF16), (((1,), (1,)), ((), ())),
                           preferred_element_type=F32)


def _bdot_tn(a, b):
    return lax.dot_general(a.astype(BF16), b.astype(BF16), (((0,), (0,)), ((), ())),
                           preferred_element_type=F32)


def _split3(x):
    hi = x.astype(BF16)
    r1 = x - hi.astype(F32)
    mid = r1.astype(BF16)
    lo = (r1 - mid.astype(F32)).astype(BF16)
    return hi, mid, lo


def _dot_exact_rhs(x, e):
    e = e.astype(BF16)
    hi, mid, lo = _split3(x)
    return (jnp.dot(hi, e, preferred_element_type=F32) + jnp.dot(mid, e, preferred_element_type=F32)
            + jnp.dot(lo, e, preferred_element_type=F32))


def _dot_exact_lhs(e, x):
    e = e.astype(BF16)
    hi, mid, lo = _split3(x)
    return (jnp.dot(e, hi, preferred_element_type=F32) + jnp.dot(e, mid, preferred_element_type=F32)
            + jnp.dot(e, lo, preferred_element_type=F32))


def _sigmoid(x):
    return 1.0 / (1.0 + jnp.exp(-x))


def _silu(x):
    return x * _sigmoid(x)


def _softplus(x):
    return jnp.maximum(x, 0.0) + jnp.log(1.0 + jnp.exp(-jnp.abs(x)))


def _rms(x, g):
    return x * lax.rsqrt(jnp.mean(x * x, axis=-1, keepdims=True) + RMS_EPS) * g


def _seg_ones(n, seg):
    r = lax.broadcasted_iota(jnp.int32, (n, n), 0) // seg
    c = lax.broadcasted_iota(jnp.int32, (n, n), 1) // seg
    return (r == c).astype(BF16)


def _full_spec(arr):
    nd = arr.ndim
    return pl.BlockSpec(arr.shape, lambda *_: (0,) * nd, pipeline_mode=pl.Buffered(1))


def _tok_spec(tm, w):
    return pl.BlockSpec((None, tm, w), lambda b, i: (b, i, 0))


def _row(a):
    return a.reshape(1, -1)


def _mod_kernel(c_ref, w_ref, b_ref, o_ref):
    o_ref[...] = _bdot(_silu(c_ref[...]), w_ref[...]) + b_ref[...]


def modulation_all(cond, ada_w, ada_b, tn=1024):
    depth, d, n = ada_w.shape
    rows = cond.shape[0]
    return pl.pallas_call(
        _mod_kernel,
        grid=(depth, n // tn),
        in_specs=[pl.BlockSpec((rows, d), lambda l, j: (0, 0)),
                  pl.BlockSpec((None, d, tn), lambda l, j: (l, 0, j)),
                  pl.BlockSpec((None, 1, tn), lambda l, j: (l, 0, j))],
        out_specs=pl.BlockSpec((None, rows, tn), lambda l, j: (l, 0, j)),
        out_shape=jax.ShapeDtypeStruct((depth, rows, n), F32),
        compiler_params=_cparams(2),
        name="adaln_modulation",
    )(cond, ada_w, ada_b.reshape(depth, 1, n))


def _ff_chunk(d_ff):
    return 1408 if d_ff % 1408 == 0 else d_ff


def _swiglu(h_bf, w1_ref, w2_ref, d_ff):
    fc = _ff_chunk(d_ff)
    acc = None
    for c in range(d_ff // fc):
        gate = jnp.dot(h_bf, w1_ref[:, c * fc:(c + 1) * fc], preferred_element_type=F32)
        up = jnp.dot(h_bf, w1_ref[:, d_ff + c * fc:d_ff + (c + 1) * fc], preferred_element_type=F32)
        a = (_silu(gate) * up).astype(BF16)
        part = jnp.dot(a, w2_ref[c * fc:(c + 1) * fc, :], preferred_element_type=F32)
        acc = part if acc is None else acc + part
    return acc


def _mod_specs(mod, ks):
    bc, _, d = mod.shape
    mod4 = mod.reshape(bc, N_MOD, 1, d)

    def spec(k):
        if bc == 1:
            return pl.BlockSpec((None, None, 1, d), lambda b, i: (0, k, 0, 0))
        return pl.BlockSpec((None, None, 1, d), lambda b, i: (b, k, 0, 0))

    return [mod4] * len(ks), [spec(k) for k in ks]


def _pre_kernel(x_ref, sh1_ref, sc1_ref, gt1_ref, sh2_ref, sc2_ref, g1_ref, g2_ref,
                w1_ref, w2_ref, win_ref, x_out_ref, *p_refs, d_ff, splits):
    x = x_ref[...]
    h = _rms(x, g1_ref[...]) * (1.0 + sc1_ref[...]) + sh1_ref[...]
    x1 = x + 0.5 * gt1_ref[...] * _swiglu(h.astype(BF16), w1_ref, w2_ref, d_ff)
    x_out_ref[...] = x1
    h2 = (_rms(x1, g2_ref[...]) * (1.0 + sc2_ref[...]) + sh2_ref[...]).astype(BF16)
    off = 0
    for ref, wdt in zip(p_refs, splits):
        ref[...] = jnp.dot(h2, win_ref[:, off:off + wdt], preferred_element_type=F32)
        off += wdt


def dense_pre(x, mod, g1, g2, w1, w2, w_in, splits, tm):
    b, l, d = x.shape
    margs, mspecs = _mod_specs(mod, (0, 1, 2, 3, 4))
    params = [_row(g1), _row(g2), w1, w2, w_in]
    outs = pl.pallas_call(
        functools.partial(_pre_kernel, d_ff=w2.shape[0], splits=tuple(splits)),
        grid=(b, l // tm),
        in_specs=[_tok_spec(tm, d)] + mspecs + [_full_spec(a) for a in params],
        out_specs=[_tok_spec(tm, d)] + [_tok_spec(tm, w) for w in splits],
        out_shape=[jax.ShapeDtypeStruct((b, l, d), F32)] + [jax.ShapeDtypeStruct((b, l, w), F32) for w in splits],
        compiler_params=_cparams(2),
        name="dense_pre",
    )(x, *margs, *params)
    return outs[0], outs[1:]


def _head_ln(x, width, eps):
    parts = []
    for h in range(x.shape[-1] // width):
        xh = x[:, h * width:(h + 1) * width]
        mu = jnp.mean(xh, axis=-1, keepdims=True)
        xc = xh - mu
        var = jnp.mean(xc * xc, axis=-1, keepdims=True)
        parts.append(xc * lax.rsqrt(var + eps))
    return jnp.concatenate(parts, axis=-1)


def _seg_ln(x, e, width, eps):
    mu = _dot_exact_rhs(x, e) * (1.0 / width)
    xc = x - mu
    var = _dot_exact_rhs(xc * xc, e) * (1.0 / width)
    return xc * lax.rsqrt(var + eps)


def _post_tail(x, mix, gt2_ref, sh3_ref, sc3_ref, gt3_ref, g3_ref, w1_ref, w2_ref, fg_ref, o_ref, d_ff):
    x2 = x + gt2_ref[...] * mix
    h3 = (_rms(x2, g3_ref[...]) * (1.0 + sc3_ref[...]) + sh3_ref[...]).astype(BF16)
    y = x2 + 0.5 * gt3_ref[...] * _swiglu(h3, w1_ref, w2_ref, d_ff)
    if fg_ref is not None:
        y = _rms(y, fg_ref[...])
    o_ref[...] = y


def _post_even_kernel(x_ref, of_ref, ob_ref, ga_ref, y_ref, bon_ref, gb_ref,
                      gt2_ref, sh3_ref, sc3_ref, gt3_ref,
                      gnw_ref, lnw_ref, lnb_ref, seg_ref, g3_ref, wout_ref, w1_ref, w2_ref,
                      *rest, d_ff, dv, hs, final):
    fg_ref = rest[0] if final else None
    o_ref = rest[-1]
    wa = of_ref.shape[-1]
    o = _head_ln(of_ref[...] + ob_ref[...], dv, 1e-5) * gnw_ref[...]
    out_a = _silu(ga_ref[...]) * o
    y = _seg_ln(y_ref[...], seg_ref[...], hs, 64e-5) * lnw_ref[...] + lnb_ref[...]
    out_b = (y + bon_ref[...]) * gb_ref[...]
    mix = _bdot(out_a, wout_ref[0:wa, :]) + _bdot(out_b, wout_ref[wa:, :])
    _post_tail(x_ref[...], mix, gt2_ref, sh3_ref, sc3_ref, gt3_ref, g3_ref, w1_ref, w2_ref, fg_ref, o_ref, d_ff)


def _post_odd_kernel(x_ref, hf_ref, hb_ref, og_ref, od_ref,
                     gt2_ref, sh3_ref, sc3_ref, gt3_ref,
                     nw_ref, g3_ref, wout_ref, w1_ref, w2_ref, *rest, d_ff, dh, final):
    fg_ref = rest[0] if final else None
    o_ref = rest[-1]
    wc = hf_ref.shape[-1]
    h = _head_ln(hf_ref[...] + hb_ref[...], dh, 1e-5) * nw_ref[...]
    out_c = _sigmoid(og_ref[...]) * h
    mix = _bdot(out_c, wout_ref[0:wc, :]) + _bdot(od_ref[...], wout_ref[wc:, :])
    _post_tail(x_ref[...], mix, gt2_ref, sh3_ref, sc3_ref, gt3_ref, g3_ref, w1_ref, w2_ref, fg_ref, o_ref, d_ff)


def _dense_post(kernel_fn, name, x, toks, mod, params, final_g, tm, **kw):
    b, l, d = x.shape
    margs, mspecs = _mod_specs(mod, (5, 6, 7, 8))
    final = final_g is not None
    params = list(params) + ([_row(final_g)] if final else [])
    return pl.pallas_call(
        functools.partial(kernel_fn, final=final, **kw),
        grid=(b, l // tm),
        in_specs=([_tok_spec(tm, d)] + [_tok_spec(tm, a.shape[-1]) for a in toks] + mspecs
                  + [_full_spec(a) for a in params]),
        out_specs=_tok_spec(tm, d),
        out_shape=jax.ShapeDtypeStruct((b, l, d), F32),
        compiler_params=_cparams(2),
        name=name,
    )(x, *toks, *margs, *params)


def _rope_tables(n, d, reps):
    rows = n // GRID_W
    row = jnp.repeat(jnp.arange(rows), GRID_W).astype(F32)
    col = (jnp.arange(rows * GRID_W) % GRID_W).astype(F32)
    nf = d // 4
    inv = ROPE_BASE ** (-jnp.arange(nf, dtype=F32) / nf)
    ang_r = row[:, None] * inv[None, :]
    ang_c = col[:, None] * inv[None, :]
    cos = jnp.concatenate([jnp.cos(ang_r), jnp.cos(ang_r), jnp.cos(ang_c), jnp.cos(ang_c)], axis=-1)
    sin = jnp.concatenate([-jnp.sin(ang_r), jnp.sin(ang_r), -jnp.sin(ang_c), jnp.sin(ang_c)], axis=-1)
    return jnp.tile(cos, (1, reps)), jnp.tile(sin, (1, reps))


def _rope(x, cos, sin, nf):
    w = x.shape[-1]
    lane = lax.broadcasted_iota(jnp.int32, x.shape, 1)
    first = (lane % (2 * nf)) < nf
    partner = jnp.where(first, pltpu.roll(x, w - nf, axis=1), pltpu.roll(x, nf, axis=1))
    return x * cos + partner * sin


def _ret_kernel(ld_ref, *refs, nr, h, dk, dv, c, nc, latent, emit_state):
    it = iter(refs)
    qkf_ref, vf_ref, qkb_ref, vb_ref = next(it), next(it), next(it), next(it)
    if latent:
        cosf_ref, sinf_ref, cosb_ref, sinb_ref, s0_ref = next(it), next(it), next(it), next(it), next(it)
    of_ref, ob_ref = next(it), next(it)
    sfin_ref = next(it) if emit_state else None
    z_scr = next(it)

    ci = pl.program_id(1)
    hk = h * dk

    @pl.when(ci == 0)
    def _():
        z_scr[...] = jnp.zeros_like(z_scr)
        if latent:
            for r in range(nr):
                for d in range(2):
                    for hh in range(h):
                        z_scr[r, d, hh, hh * dk:(hh + 1) * dk, :] = s0_ref[r, d, hh]

    ii = lax.broadcasted_iota(jnp.int32, (c, c), 0)
    jj = lax.broadcasted_iota(jnp.int32, (c, c), 1)
    icol = lax.broadcasted_iota(jnp.int32, (c, 1), 0).astype(F32)
    lane = lax.broadcasted_iota(jnp.int32, (1, hk), 1)

    for d, (qk_ref, v_ref, o_ref) in enumerate(((qkf_ref, vf_ref, of_ref), (qkb_ref, vb_ref, ob_ref))):
        diff = (ii - jj) if d == 0 else (jj - ii)
        causal = diff >= 0
        dpos = jnp.where(causal, diff, 0).astype(F32)
        qe = (icol + 1.0) if d == 0 else (c - icol)
        ke = (c - 1.0 - icol) if d == 0 else icol
        tables = []
        for hh in range(h):
            lg = ld_ref[d, hh]
            tables.append((jnp.where(causal, jnp.exp(lg * dpos), 0.0), jnp.exp(lg * qe), jnp.exp(lg * ke),
                           jnp.exp(lg * jnp.full((1, 1), c, F32))))
        for r in range(nr):
            qk = qk_ref[r]
            if latent:
                cos_ref, sin_ref = (cosf_ref, sinf_ref) if d == 0 else (cosb_ref, sinb_ref)
                qk = _rope(qk, cos_ref[...], sin_ref[...], dk // 4)
            q = qk[:, :hk]
            k = qk[:, hk:] * (dk ** -0.5)
            v = v_ref[r]
            outs = []
            for hh in range(h):
                d_intra, q_dec, k_dec, c_dec = tables[hh]
                msk = (lane // dk) == hh
                qh = jnp.where(msk, q, 0.0)
                kh = jnp.where(msk, k, 0.0)
                vh = v[:, hh * dv:(hh + 1) * dv]
                att = _bdot_nt(qh, kh) * d_intra
                z = z_scr[r, d, hh]
                outs.append(_bdot(att, vh) + _bdot(qh * q_dec, z))
                z_scr[r, d, hh] = z * c_dec + _bdot_tn(kh * k_dec, vh)
            o_ref[r] = jnp.concatenate(outs, axis=-1)

    if emit_state:
        @pl.when(ci == nc - 1)
        def _():
            for r in range(nr):
                for d in range(2):
                    for hh in range(h):
                        sfin_ref[r, d, hh] = z_scr[r, d, hh, hh * dk:(hh + 1) * dk, :]


def retention_scan(qk, v, log_decay, s0, h, dk, dv, latent, emit_state, nr=4):
    b, l, _ = qk.shape
    c = CHUNK
    nc = l // c
    hk = h * dk
    fwd = lambda w: pl.BlockSpec((nr, c, w), lambda bb, ci: (bb, ci, 0))
    bwd = lambda w: pl.BlockSpec((nr, c, w), lambda bb, ci: (bb, nc - 1 - ci, 0))
    args = [log_decay, qk, v, qk, v]
    specs = [pl.BlockSpec(memory_space=pltpu.SMEM), fwd(2 * hk), fwd(h * dv), bwd(2 * hk), bwd(h * dv)]
    if latent:
        cos, sin = _rope_tables(l, dk, 2 * h)
        args += [cos, sin, cos, sin, s0]
        tf = pl.BlockSpec((c, 2 * hk), lambda bb, ci: (ci, 0))
        tb = pl.BlockSpec((c, 2 * hk), lambda bb, ci: (nc - 1 - ci, 0))
        specs += [tf, tf, tb, tb, pl.BlockSpec((nr, 2, h, dk, dv), lambda bb, ci: (bb, 0, 0, 0, 0))]
    out_shape = [jax.ShapeDtypeStruct((b, l, h * dv), F32)] * 2
    out_specs = [fwd(h * dv), bwd(h * dv)]
    if emit_state:
        out_shape.append(jax.ShapeDtypeStruct((b, 2, h, dk, dv), F32))
        out_specs.append(pl.BlockSpec((nr, 2, h, dk, dv), lambda bb, ci: (bb, 0, 0, 0, 0)))
    return pl.pallas_call(
        functools.partial(_ret_kernel, nr=nr, h=h, dk=dk, dv=dv, c=c, nc=nc, latent=latent,
                          emit_state=emit_state),
        grid=(b // nr, nc),
        in_specs=specs,
        out_specs=out_specs,
        out_shape=out_shape,
        scratch_shapes=[pltpu.VMEM((nr, 2, h, hk, dv), F32)],
        compiler_params=_cparams(2),
        name="retention_scan",
    )(*args)


def _rwkv_prep_kernel(p_ref, prev_ref, next_ref, mu_ref, w0_ref, a0_ref, w2_ref, a2_ref, g2_ref,
                      kk_ref, ka_ref, rk_ref, seg_ref, s_out, g_out, bon_out, *, tm, nt, wb):
    i = pl.program_id(1)
    p = p_ref[...]
    row = lax.broadcasted_iota(jnp.int32, p.shape, 0)
    prow = jnp.where(i == 0, 0.0, prev_ref[SUBLANES - 1:SUBLANES, :])
    nrow = jnp.where(i == nt - 1, 0.0, next_ref[0:1, :])
    prev = jnp.where(row == 0, prow, pltpu.roll(p, 1, axis=0))
    nxt = jnp.where(row == tm - 1, nrow, pltpu.roll(p, tm - 1, axis=0))
    ps = p + mu_ref[...] * (0.5 * (prev + nxt) - p)
    r = ps[:, 0:wb]
    k = ps[:, wb:2 * wb]
    v = ps[:, 2 * wb:3 * wb]
    lora = ps[:, 3 * wb:]
    tl = jnp.tanh(lora)
    seg = seg_ref[...]
    kk = k * kk_ref[...]
    nrm = jnp.sqrt(_dot_exact_rhs(kk * kk, seg))
    kk = kk / jnp.maximum(nrm, 1e-12)
    s_out[0] = r
    s_out[1] = v
    s_out[2] = -kk
    g_out[...] = _bdot(_sigmoid(lora), g2_ref[...])
    bon_out[...] = _dot_exact_rhs(r * k * rk_ref[...], seg) * v
    for d in range(2):
        w_log = -_softplus(-(w0_ref[d:d + 1, :] + _bdot(tl, w2_ref[d]))) - 0.5
        s_out[3 + 3 * d] = jnp.exp(-jnp.exp(w_log))
        a = _sigmoid(a0_ref[d:d + 1, :] + _bdot(lora, a2_ref[d]))
        s_out[4 + 3 * d] = k * (1.0 + (a - 1.0) * ka_ref[...])
        s_out[5 + 3 * d] = kk * a


def rwkv_prep(p, mu, w0, w2p, a0, a2p, g2p, k_k, k_a, r_k, seg, wb, tm):
    b, l, pw = p.shape
    nt = l // tm
    r8 = tm // SUBLANES
    nb8 = l // SUBLANES
    params = [_row(mu), w0, a0, w2p, a2p, g2p, _row(k_k), _row(k_a), _row(r_k), seg]
    return pl.pallas_call(
        functools.partial(_rwkv_prep_kernel, tm=tm, nt=nt, wb=wb),
        grid=(b, nt),
        in_specs=[_tok_spec(tm, pw),
                  pl.BlockSpec((None, SUBLANES, pw), lambda bb, i: (bb, jnp.maximum(i * r8 - 1, 0), 0)),
                  pl.BlockSpec((None, SUBLANES, pw), lambda bb, i: (bb, jnp.minimum((i + 1) * r8, nb8 - 1), 0))]
                 + [_full_spec(a) for a in params],
        out_specs=[pl.BlockSpec((9, None, tm, wb), lambda bb, i: (0, bb, i, 0))] + [_tok_spec(tm, wb)] * 2,
        out_shape=[jax.ShapeDtypeStruct((9, b, l, wb), F32)] + [jax.ShapeDtypeStruct((b, l, wb), F32)] * 2,
        compiler_params=_cparams(2),
        name="rwkv_prep",
    )(p, p, p, *params)


ROW_PITCH = 72
CHAIN_TILE = 128
N_CHAIN_ARRAYS = 6


def _chain_groups(b, h):
    per_dir = b * h
    if 2 * per_dir == LANES:
        return 1, True
    assert per_dir % LANES == 0
    return 2 * per_dir // LANES, False


def _to_chains_kernel(lo_ref, hi_ref, o_ref, t_scr, *, n, h, nb, tt):
    for half, ref in enumerate((lo_ref, hi_ref)):
        for bb in range(nb):
            xt = ref[bb].T
            for hh in range(h):
                t_scr[half * nb + bb, hh * ROW_PITCH:hh * ROW_PITCH + n, :] = xt[hh * n:(hh + 1) * n]
    for c in range(n):
        q = jnp.concatenate([t_scr[k, pl.ds(c, h, stride=ROW_PITCH), :] for k in range(2 * nb)], axis=0)
        o_ref[pl.ds(c, tt, stride=ROW_PITCH), :] = q.T
    pad = jnp.zeros((ROW_PITCH - n, LANES), F32)
    for t in range(tt):
        o_ref[t * ROW_PITCH + n:(t + 1) * ROW_PITCH, :] = pad


def to_chains(stack, h):
    _, b, l, w = stack.shape
    n = w // h
    nb = LANES // (2 * h)
    tt = CHAIN_TILE
    ng, mixed = _chain_groups(b, h)
    src1 = lambda a: jnp.where(a < 3, a, a + 3)
    if mixed:
        lo_map = lambda a, g, ti: (a, 0, ti, 0)
        hi_map = lambda a, g, ti: (src1(a), 0, ti, 0)
    else:
        src = lambda a, g: jnp.where(g < ng // 2, a, src1(a))
        blocks_per_dir = ng // 2
        lo_map = lambda a, g, ti: (src(a, g), 2 * (g % blocks_per_dir), ti, 0)
        hi_map = lambda a, g, ti: (src(a, g), 2 * (g % blocks_per_dir) + 1, ti, 0)
    out = pl.pallas_call(
        functools.partial(_to_chains_kernel, n=n, h=h, nb=nb, tt=tt),
        grid=(N_CHAIN_ARRAYS, ng, l // tt),
        in_specs=[pl.BlockSpec((None, nb, tt, w), lo_map), pl.BlockSpec((None, nb, tt, w), hi_map)],
        out_specs=pl.BlockSpec((None, None, tt * ROW_PITCH, LANES), lambda a, g, ti: (a, g, ti, 0)),
        out_shape=jax.ShapeDtypeStruct((N_CHAIN_ARRAYS, ng, l * ROW_PITCH, LANES), F32),
        scratch_shapes=[pltpu.VMEM((2 * nb, h * ROW_PITCH, tt), F32)],
        compiler_params=_cparams(3),
        name="to_chains",
    )(stack, stack)
    return out.reshape(N_CHAIN_ARRAYS, ng, l, ROW_PITCH, LANES)


def _from_chains_kernel(a_ref, b_ref, o_ref, t_scr, *, n, h, nb, tt, mixed):
    lane = lax.broadcasted_iota(jnp.int32, (tt, LANES), 1)
    nk = LANES // h
    for c in range(n):
        va = a_ref[pl.ds(c, tt, stride=ROW_PITCH), :]
        vb = b_ref[pl.ds(c, tt, stride=ROW_PITCH), :]
        tiles = [jnp.where(lane >= LANES // 2, vb, va)] if mixed else [va, vb]
        for idx, v in enumerate(tiles):
            vt = v.T
            for k in range(nk):
                t_scr[idx * nk + k, pl.ds(c, h, stride=ROW_PITCH), :] = vt[k * h:(k + 1) * h]
    for bb in range(nb):
        k0, k1 = (bb, nb + bb) if mixed else (bb, nk + bb)
        parts = [t_scr[k0, hh * ROW_PITCH:hh * ROW_PITCH + n, :] + t_scr[k1, hh * ROW_PITCH:hh * ROW_PITCH + n, :]
                 for hh in range(h)]
        o_ref[bb] = jnp.concatenate(parts, axis=0).T


def from_chains(yf, yb, b, h, n):
    ng, l, _, _ = yf.shape
    _, mixed = _chain_groups(b, h)
    tt = CHAIN_TILE
    nk = LANES // h
    nb = nk // 2 if mixed else nk
    yf2 = yf.reshape(ng, l * ROW_PITCH, LANES)
    yb2 = yb.reshape(ng, l * ROW_PITCH, LANES)
    if mixed:
        a_map = lambda gb, ti: (0, ti, 0)
        b_map = a_map
    else:
        a_map = lambda gb, ti: (gb, ti, 0)
        b_map = lambda gb, ti: (ng // 2 + gb, ti, 0)
    spec = lambda m: pl.BlockSpec((None, tt * ROW_PITCH, LANES), m)
    return pl.pallas_call(
        functools.partial(_from_chains_kernel, n=n, h=h, nb=nb, tt=tt, mixed=mixed),
        grid=(b // nb, l // tt),
        in_specs=[spec(a_map), spec(b_map)],
        out_specs=pl.BlockSpec((nb, tt, h * n), lambda gb, ti: (gb, ti, 0)),
        out_shape=jax.ShapeDtypeStruct((b, l, h * n), F32),
        scratch_shapes=[pltpu.VMEM(((1 if mixed else 2) * nk, h * ROW_PITCH, tt), F32)],
        compiler_params=_cparams(2),
        name="from_chains",
    )(yf2, yb2)


def _rwkv_scan_kernel(*refs, n, tt, nt, ng, mixed, has_state, emit_state):
    it = iter(refs)
    fw = [next(it) for _ in range(N_CHAIN_ARRAYS)]
    bw = [next(it) for _ in range(N_CHAIN_ARRAYS)]
    a_next_ref, a_prev_ref = next(it), next(it)
    s0_ref = next(it) if has_state else None
    yf_ref, yb_ref = next(it), next(it)
    sfin_ref = next(it) if emit_state else None
    s_scr, sa_scr, gam_scr = next(it), next(it), next(it)
    r_buf, v_buf, a_buf, an_buf, k_buf, b_buf = (next(it) for _ in range(N_CHAIN_ARRAYS))
    g = pl.program_id(0)
    ti = pl.program_id(1)

    lane = lax.broadcasted_iota(jnp.int32, (n, LANES), 1)
    split = LANES // 2 if mixed else jnp.where(g < ng // 2, LANES, 0)
    is_bwd = lane >= split

    def sel(idx, s):
        return jnp.where(is_bwd, bw[idx][tt - 1 - s, 0:n, :], fw[idx][s, 0:n, :])

    for s in range(tt):
        a_buf[s] = sel(2, s)
    a_after = jnp.where(is_bwd, a_prev_ref[0, 0:n, :], a_next_ref[0, 0:n, :])
    a_buf[tt] = jnp.where(ti == nt - 1, 0.0, a_after)
    gam = jnp.ones((n, LANES), F32)
    for s in range(tt):
        gam = gam * sel(3, s)
        inv = 1.0 / gam
        r_buf[s] = sel(0, s) * gam
        v_buf[s] = sel(1, s)
        k_buf[s] = sel(4, s) * inv
        b_buf[s] = sel(5, s) * inv
        an_buf[s] = a_buf[s + 1] * gam
    gam_scr[...] = gam

    @pl.when(ti == 0)
    def _():
        if has_state:
            acc = jnp.zeros(sa_scr.shape, F32)
            for j in range(n):
                sj = s0_ref[j]
                s_scr[j] = sj
                acc = acc + sj * a_buf[0, j:j + 1, :]
            sa_scr[...] = acc
        else:
            s_scr[...] = jnp.zeros_like(s_scr)
            sa_scr[...] = jnp.zeros_like(sa_scr)

    pad = jnp.zeros((ROW_PITCH - n, LANES), F32)

    def step(t, sa):
        vt = v_buf[t]
        y = jnp.zeros_like(sa)
        san = jnp.zeros_like(sa)
        for j in range(n):
            new = s_scr[j] + (sa * b_buf[t, j:j + 1, :] + vt * k_buf[t, j:j + 1, :])
            s_scr[j] = new
            y = y + new * r_buf[t, j:j + 1, :]
            san = san + new * an_buf[t, j:j + 1, :]
        yf_ref[t, 0:n, :] = y
        yf_ref[t, n:, :] = pad
        yb_ref[tt - 1 - t, 0:n, :] = y
        yb_ref[tt - 1 - t, n:, :] = pad
        return san

    sa_scr[...] = lax.fori_loop(0, tt, step, sa_scr[...])
    for j in range(n):
        s_scr[j] = s_scr[j] * gam_scr[j:j + 1, :]

    if emit_state:
        @pl.when(ti == nt - 1)
        def _():
            sfin_ref[...] = s_scr[...]


def rwkv_scan(chains, s0, n, mixed, emit_state, tt=32):
    _, ng, l, rp, lanes = chains.shape
    nt = l // tt
    has_state = s0 is not None

    def seq(a, rev):
        if rev:
            return pl.BlockSpec((None, None, tt, rp, lanes), lambda g, ti: (a, g, nt - 1 - ti, 0, 0))
        return pl.BlockSpec((None, None, tt, rp, lanes), lambda g, ti: (a, g, ti, 0, 0))

    one = lambda m: pl.BlockSpec((None, None, 1, rp, lanes), m)
    st = pl.BlockSpec((None, n, n, lanes), lambda g, ti: (g, 0, 0, 0))
    yspec_f = pl.BlockSpec((None, tt, rp, lanes), lambda g, ti: (g, ti, 0, 0))
    yspec_b = pl.BlockSpec((None, tt, rp, lanes), lambda g, ti: (g, nt - 1 - ti, 0, 0))
    args = [chains] * (2 * N_CHAIN_ARRAYS + 2)
    specs = ([seq(a, False) for a in range(N_CHAIN_ARRAYS)] + [seq(a, True) for a in range(N_CHAIN_ARRAYS)]
             + [one(lambda g, ti: (2, g, jnp.minimum((ti + 1) * tt, l - 1), 0, 0)),
                one(lambda g, ti: (2, g, jnp.maximum((nt - 1 - ti) * tt - 1, 0), 0, 0))])
    if has_state:
        args.append(s0)
        specs.append(st)
    out_shape = [jax.ShapeDtypeStruct((ng, l, rp, lanes), F32)] * 2
    out_specs = [yspec_f, yspec_b]
    if emit_state:
        out_shape.append(jax.ShapeDtypeStruct((ng, n, n, lanes), F32))
        out_specs.append(st)
    seq_buf = pltpu.VMEM((tt, n, lanes), F32)
    return pl.pallas_call(
        functools.partial(_rwkv_scan_kernel, n=n, tt=tt, nt=nt, ng=ng, mixed=mixed,
                          has_state=has_state, emit_state=emit_state),
        grid=(ng, nt),
        in_specs=specs,
        out_specs=out_specs,
        out_shape=out_shape,
        scratch_shapes=[pltpu.VMEM((n, n, lanes), F32), pltpu.VMEM((n, lanes), F32), pltpu.VMEM((n, lanes), F32),
                        seq_buf, seq_buf, pltpu.VMEM((tt + 1, n, lanes), F32), seq_buf, seq_buf, seq_buf],
        compiler_params=_cparams(2),
        name="rwkv_scan",
    )(*args)


def _state_to_chains(s):
    b, _, h, n, _ = s.shape
    ng = (2 * b * h) // LANES
    return s.transpose(4, 3, 1, 0, 2).reshape(n, n, ng, LANES).transpose(2, 0, 1, 3)


def _state_from_chains(s, b, h):
    ng, n, _, _ = s.shape
    return s.transpose(1, 2, 0, 3).reshape(n, n, 2, b, h).transpose(3, 2, 4, 1, 0)


MLSTM_BIAS_ARG = 4


def _mlstm_kernel(*refs, nr, nc, emit_state, **kw):
    ci = pl.program_id(1)
    rows = [[ref if i == MLSTM_BIAS_ARG else ref.at[r] for i, ref in enumerate(refs)] for r in range(nr)]

    @pl.when(ci == 0)
    def _():
        for row in rows:
            _mlstm_row(*row, phase="load", emit_state=emit_state, **kw)

    for row in rows:
        _mlstm_row(*row, phase="chunk", emit_state=emit_state, **kw)

    if emit_state:
        @pl.when(ci == nc - 1)
        def _():
            for row in rows:
                _mlstm_row(*row, phase="store", emit_state=emit_state, **kw)


def _mlstm_row(*refs, phase, h, dh, c, has_state, emit_state):
    it = iter(refs)
    qkvf_ref, gf_ref, qkvb_ref, gb_ref, bias_ref = (next(it) for _ in range(5))
    if has_state:
        c0_ref, n0_ref, m0_ref = next(it), next(it), next(it)
    hf_ref, hb_ref = next(it), next(it)
    if emit_state:
        cfin_ref, nfin_ref, mfin_ref = next(it), next(it), next(it)
    cma_scr, m_scr = next(it), next(it)

    if phase == "load":
        if has_state:
            row0 = lax.broadcasted_iota(jnp.int32, (dh, dh), 0) == 0
            for s in range(2 * h):
                cma_scr[s, :, 0:dh] = c0_ref[s // h, s % h]
                cma_scr[s, :, dh:] = jnp.where(row0, n0_ref[s:s + 1, :], 0.0).T
            m_scr[...] = m0_ref[...]
        else:
            cma_scr[...] = jnp.zeros_like(cma_scr)
            m_scr[...] = jnp.zeros_like(m_scr)
        return
    if phase == "store":
        for s in range(2 * h):
            cfin_ref[s // h, s % h] = cma_scr[s, :, 0:dh]
            nfin_ref[s:s + 1, :] = cma_scr[s, :, dh:].T[0:1, :]
        mfin_ref[...] = m_scr[...]
        return

    ii = lax.broadcasted_iota(jnp.int32, (c, c), 0)
    jj = lax.broadcasted_iota(jnp.int32, (c, c), 1)
    gl_lane = lax.broadcasted_iota(jnp.int32, (c, LANES), 1)
    is_fg = (gl_lane >= 2 * h) & (gl_lane < 4 * h)
    ones_col = (lax.broadcasted_iota(jnp.int32, (c, dh), 1) == 0).astype(F32)

    for d, (qkv_ref, g_ref, o_ref) in enumerate(((qkvf_ref, gf_ref, hf_ref), (qkvb_ref, gb_ref, hb_ref))):
        causal = (ii >= jj) if d == 0 else (ii <= jj)
        gx = g_ref[...] + bias_ref[...]
        gl = jnp.where(is_fg, -_softplus(-gx), gx)
        bcum = _dot_exact_lhs(causal.astype(BF16), gl)
        glt = gl.T
        brow = _dot_exact_rhs(glt, (~causal).astype(BF16) + (ii == jj).astype(BF16))
        qkv = qkv_ref[...]
        kt_all = (qkv[:, h * dh:2 * h * dh] * (dh ** -0.5)).T
        end = c - 1 if d == 0 else 0
        outs = []
        for hh in range(h):
            ig_c = d * h + hh
            fg_c = 2 * h + d * h + hh
            sr = d * h + hh
            q = qkv[:, hh * dh:(hh + 1) * dh]
            k = qkv[:, (h + hh) * dh:(h + hh + 1) * dh] * (dh ** -0.5)
            v_aug = jnp.concatenate([qkv[:, (2 * h + hh) * dh:(2 * h + hh + 1) * dh], ones_col], axis=-1)
            kt = kt_all[hh * dh:(hh + 1) * dh, :]
            b_col = bcum[:, fg_c:fg_c + 1]
            b_row = brow[fg_c:fg_c + 1, :]
            c_row = glt[ig_c:ig_c + 1, :] - b_row
            m = m_scr[sr:sr + 1, 0:1]
            cma = cma_scr[sr]
            dmat = jnp.where(causal, c_row, -jnp.inf)
            g_col = jnp.maximum(m, jnp.max(dmat, axis=-1, keepdims=True))
            s = _bdot_nt(q, k) * jnp.exp(dmat - g_col)
            nd = _bdot(s, v_aug) + jnp.exp(m - g_col) * _bdot(q, cma)
            den = nd[:, dh:dh + 1]
            outs.append(nd[:, 0:dh] / jnp.maximum(jnp.abs(den), jnp.exp(-(b_col + g_col))))
            bl = b_row[:, end:end + 1]
            wl_row = bl + c_row
            m_new = jnp.maximum(bl + m, jnp.max(wl_row, axis=-1, keepdims=True))
            cma_scr[sr] = jnp.exp(bl + m - m_new) * cma + _bdot(kt * jnp.exp(wl_row - m_new), v_aug)
            m_scr[sr:sr + 1, :] = jnp.broadcast_to(m_new, (1, LANES))
        o_ref[...] = jnp.concatenate(outs, axis=-1)


def mlstm_scan(qkv, gates, bias_row, c0, n0, m0, h, dh, emit_state, nr=4):
    b, l, _ = qkv.shape
    c = CHUNK
    nc = l // c
    has_state = c0 is not None
    w = qkv.shape[-1]
    fwd = lambda ww: pl.BlockSpec((nr, c, ww), lambda bb, ci: (bb, ci, 0))
    bwd = lambda ww: pl.BlockSpec((nr, c, ww), lambda bb, ci: (bb, nc - 1 - ci, 0))
    cspec = pl.BlockSpec((nr, 2, h, dh, dh), lambda bb, ci: (bb, 0, 0, 0, 0))
    rspec = pl.BlockSpec((nr, 2 * h, LANES), lambda bb, ci: (bb, 0, 0))
    args = [qkv, gates, qkv, gates, bias_row]
    specs = [fwd(w), fwd(LANES), bwd(w), bwd(LANES), _full_spec(bias_row)]
    if has_state:
        args += [c0, n0.reshape(b, 2 * h, dh), jnp.broadcast_to(m0.reshape(b, 2 * h, 1), (b, 2 * h, LANES))]
        specs += [cspec, rspec, rspec]
    out_shape = [jax.ShapeDtypeStruct((b, l, h * dh), F32)] * 2
    out_specs = [fwd(h * dh), bwd(h * dh)]
    if emit_state:
        out_shape += [jax.ShapeDtypeStruct((b, 2, h, dh, dh), F32),
                      jax.ShapeDtypeStruct((b, 2 * h, dh), F32), jax.ShapeDtypeStruct((b, 2 * h, LANES), F32)]
        out_specs += [cspec, rspec, rspec]
    return pl.pallas_call(
        functools.partial(_mlstm_kernel, nr=nr, nc=nc, emit_state=emit_state, h=h, dh=dh, c=c,
                          has_state=has_state),
        grid=(b // nr, nc),
        in_specs=specs,
        out_specs=out_specs,
        out_shape=out_shape,
        scratch_shapes=[pltpu.VMEM((nr, 2 * h, dh, 2 * dh), F32), pltpu.VMEM((nr, 2 * h, LANES), F32)],
        compiler_params=_cparams(2),
        name="mlstm_scan",
    )(*args)


def _attn_prep_kernel(*refs, hq, hkv, hd, latent):
    it = iter(refs)
    q_ref, kv_ref, gq_ref, gk_ref, segq_ref, segk_ref = (next(it) for _ in range(6))
    if latent:
        cq_ref, sq_ref, ck_ref, sk_ref = (next(it) for _ in range(4))
    qo_ref, ko_ref, vo_ref = next(it), next(it), next(it)
    q = q_ref[...]
    kv = kv_ref[...]
    kw = hkv * hd
    k = kv[:, :kw]
    v = kv[:, kw:]
    qn = q * lax.rsqrt(_dot_exact_rhs(q * q, segq_ref[...]) * (1.0 / hd) + RMS_EPS) * gq_ref[...]
    kn = k * lax.rsqrt(_dot_exact_rhs(k * k, segk_ref[...]) * (1.0 / hd) + RMS_EPS) * gk_ref[...]
    if latent:
        qn = _rope(qn, cq_ref[...], sq_ref[...], hd // 4)
        kn = _rope(kn, ck_ref[...], sk_ref[...], hd // 4)
    qo_ref[...] = qn * (hd ** -0.5 * LOG2E)
    for j in range(hkv):
        ko_ref[j] = kn[:, j * hd:(j + 1) * hd]
        vo_ref[j] = v[:, j * hd:(j + 1) * hd]


def attn_prep(q, kv, qk_gain, hq, hkv, hd, latent, tm):
    b, l, wq = q.shape
    wkv = kv.shape[-1]
    kw = hkv * hd
    params = [jnp.tile(qk_gain[0], hq).reshape(1, wq), jnp.tile(qk_gain[1], hkv).reshape(1, kw),
              _seg_ones(wq, hd), _seg_ones(kw, hd)]
    args = [q, kv] + params
    specs = [_tok_spec(tm, wq), _tok_spec(tm, wkv)] + [_full_spec(a) for a in params]
    if latent:
        cq, sq = _rope_tables(l, hd, hq)
        ck, sk = _rope_tables(l, hd, hkv)
        args += [cq, sq, ck, sk]
        specs += [pl.BlockSpec((tm, wq), lambda bb, i: (i, 0))] * 2 + [pl.BlockSpec((tm, kw), lambda bb, i: (i, 0))] * 2
    kvspec = pl.BlockSpec((None, hkv, tm, hd), lambda bb, i: (bb, 0, i, 0))
    return pl.pallas_call(
        functools.partial(_attn_prep_kernel, hq=hq, hkv=hkv, hd=hd, latent=latent),
        grid=(b, l // tm),
        in_specs=specs,
        out_specs=[_tok_spec(tm, wq), kvspec, kvspec],
        out_shape=[jax.ShapeDtypeStruct((b, l, wq), F32)] + [jax.ShapeDtypeStruct((b, hkv, l, hd), F32)] * 2,
        compiler_params=_cparams(2),
        name="attn_prep",
    )(*args)


def _attn_kernel(*refs, g, hd, tq, nsub, kb, has_ctx):
    it = iter(refs)
    q_ref, k_ref, v_ref = next(it), next(it), next(it)
    if has_ctx:
        ck_ref, cv_ref = next(it), next(it)
    o_ref = next(it)
    s_scr = next(it)
    sources =([(ck_ref, cv_ref)] if has_ctx else []) + [(k_ref, v_ref)]
    maxes = []
    for u in range(nsub):
        q = q_ref[u * tq:(u + 1) * tq, :]
        qs = jnp.concatenate([q[:, i * hd:(i + 1) * hd] for i in range(g)], axis=0).astype(BF16)
        m = None
        row = 0
        for kr, _ in sources:
            for j in range(kr.shape[0] // kb):
                st = _bdot_nt(kr[j * kb:(j + 1) * kb, :], qs)
                s_scr[u, row:row + kb, :] = st
                bm = jnp.max(st, axis=0, keepdims=True)
                m = bm if m is None else jnp.maximum(m, bm)
                row += kb
        maxes.append(m)
    for u in range(nsub):
        acc = jnp.zeros((hd, g * tq), F32)
        den = jnp.zeros((1, g * tq), F32)
        row = 0
        for _, vr in sources:
            for j in range(vr.shape[0] // kb):
                p = jnp.exp2(s_scr[u, row:row + kb, :] - maxes[u])
                den = den + jnp.sum(p, axis=0, keepdims=True)
                acc = acc + _bdot_tn(vr[j * kb:(j + 1) * kb, :], p)
                row += kb
        ot = acc / den
        o = jnp.concatenate([ot, jnp.zeros((LANES - hd, g * tq), F32)], axis=0).T
        for i in range(g):
            o_ref[u * tq:(u + 1) * tq, i * hd:(i + 1) * hd] = o[i * tq:(i + 1) * tq, 0:hd]


def attention(q, k, v, ctx_k, ctx_v, layer_o, hq, hkv, hd, tq, nsub):
    b, l, wq = q.shape
    g = hq // hkv
    has_ctx = ctx_k is not None
    qspec = pl.BlockSpec((None, nsub * tq, g * hd), lambda bb, j, i: (bb, i, j))
    kvspec = pl.BlockSpec((None, None, l, hd), lambda bb, j, i: (bb, j, 0, 0))
    args = [q, k, v]
    specs = [qspec, kvspec, kvspec]
    n_keys = l
    if has_ctx:
        s = ctx_k.shape[3]
        n_keys += s
        cspec = pl.BlockSpec((None, None, None, s, hd), lambda bb, j, i: (bb, layer_o, j, 0, 0))
        args += [ctx_k, ctx_v]
        specs += [cspec, cspec]
    kb = min(256, l)
    return pl.pallas_call(
        functools.partial(_attn_kernel, g=g, hd=hd, tq=tq, nsub=nsub, kb=kb, has_ctx=has_ctx),
        grid=(b, hkv, l // (nsub * tq)),
        in_specs=specs,
        out_specs=qspec,
        out_shape=jax.ShapeDtypeStruct((b, l, wq), F32),
        scratch_shapes=[pltpu.VMEM((nsub, n_keys, g * tq), F32)],
        compiler_params=_cparams(3),
        name="gqa_attention",
    )(*args)


DENSE_TILE = 512
PREP_TILE = 256
QUERY_TILE = 128
QUERY_SUBTILES = 2


def _tile_plan(seq_len):
    return dict(tm=min(seq_len, DENSE_TILE), tp=min(seq_len, PREP_TILE), tq=QUERY_TILE, nsub=QUERY_SUBTILES)


def _pad_rows(w, lo, total):
    return jnp.zeros((total, w.shape[-1]), w.dtype).at[lo:lo + w.shape[0]].set(w)


def kernel(x_prompt, x_sample, state_ret, state_rwkv, state_mlstm_c, state_mlstm_n, state_mlstm_m, cache_k, cache_v, c, c_ctx, ada_w, ada_b, norm_g, ffn_w1, ffn_w2, w_in_even, w_out_even, ret_log_decay, ret_gn_w, rwkv_mu, rwkv_w0, rwkv_w2, rwkv_a0, rwkv_a2, rwkv_g2, rwkv_k_k, rwkv_k_a, rwkv_r_k, rwkv_ln_w, rwkv_ln_b, w_in_odd, w_out_odd, mlstm_i_bias, mlstm_f_bias, mlstm_norm_w, attn_qk_norm, final_norm):
    depth = ada_w.shape[0]
    d_model = x_prompt.shape[-1]
    h_a, dk_a, dv_a = state_ret.shape[3:]
    h_b, hs_b = state_rwkv.shape[3:5]
    h_c, dh_c = state_mlstm_c.shape[3:5]
    hkv_d, hd_d = cache_k.shape[2], cache_k.shape[4]
    wa, wb, wc = h_a * dv_a, h_b * hs_b, h_c * dh_c
    wd = w_out_odd.shape[1] - wc
    hq_d = wd // hd_d
    kvw = hkv_d * hd_d
    n_dec = c.shape[0]
    lora_w, lora_a, lora_g = rwkv_w2.shape[2], rwkv_a2.shape[2], rwkv_g2.shape[1]
    lora_tot = lora_w + lora_a + lora_g

    rows = -(-(n_dec + 1) // SUBLANES) * SUBLANES
    cond = jnp.zeros((rows, d_model), F32).at[:n_dec].set(c).at[n_dec].set(c_ctx)
    mod = modulation_all(cond, ada_w, ada_b).reshape(depth, rows, N_MOD, d_model)

    seg_b = _seg_ones(wb, hs_b)
    streams = {
        "prompt": dict(x=x_prompt, latent=False, **_tile_plan(x_prompt.shape[1])),
        "sample": dict(x=x_sample, latent=True, **_tile_plan(x_sample.shape[1])),
    }
    new_states = {}

    for l in range(depth):
        w1a, w1b = ffn_w1[l, 0].astype(BF16), ffn_w1[l, 1].astype(BF16)
        w2a, w2b = ffn_w2[l, 0].astype(BF16), ffn_w2[l, 1].astype(BF16)
        fin = final_norm if l == depth - 1 else None
        if l % 2 == 0:
            e = l // 2
            w_in = w_in_even[e].astype(BF16)
            w_out = w_out_even[e].astype(BF16)
            splits = (2 * h_a * dk_a, wa, wa, 3 * wb + lora_tot)
            w2p = jnp.stack([_pad_rows(rwkv_w2[e, d], 0, lora_tot) for d in range(2)]).astype(BF16)
            a2p = jnp.stack([_pad_rows(rwkv_a2[e, d], lora_w, lora_tot) for d in range(2)]).astype(BF16)
            g2p = _pad_rows(rwkv_g2[e], lora_w + lora_a, lora_tot).astype(BF16)
        else:
            o = l // 2
            wi = w_in_odd[o]
            g0 = 3 * wc
            p_c = g0 + 4 * h_c + wc
            gates_w = jnp.zeros((d_model, LANES), F32).at[:, :4 * h_c].set(wi[:, g0:g0 + 4 * h_c])
            w_in = jnp.concatenate([wi[:, :g0], wi[:, g0 + 4 * h_c:p_c], wi[:, p_c:p_c + wd],
                                    wi[:, p_c + wd:], gates_w], axis=1).astype(BF16)
            w_out = w_out_odd[o].astype(BF16)
            splits = (3 * wc, wc, wd, 2 * kvw, LANES)
            bias_row = jnp.zeros((1, LANES), F32).at[0, :2 * h_c].set(mlstm_i_bias[o].reshape(-1))
            bias_row = bias_row.at[0, 2 * h_c:4 * h_c].set(mlstm_f_bias[o].reshape(-1))

        for name, st in streams.items():
            x, latent, tm = st["x"], st["latent"], st["tm"]
            b, seq, _ = x.shape
            m = mod[l, :n_dec] if latent else mod[l, n_dec:n_dec + 1]
            emit = not latent
            x1, parts = dense_pre(x, m, norm_g[l, 0], norm_g[l, 1], w1a, w2a, w_in, splits, tm)
            if l % 2 == 0:
                qk_a, v_a, g_a, p_b = parts
                ret = retention_scan(qk_a, v_a, ret_log_decay[e], state_ret[:, e] if latent else None,
                                     h_a, dk_a, dv_a, latent, emit)
                stack, g_b, bonus = rwkv_prep(
                    p_b, rwkv_mu[e], rwkv_w0[e], w2p, rwkv_a0[e], a2p, g2p, rwkv_k_k[e], rwkv_k_a[e],
                    rwkv_r_k[e].reshape(-1), seg_b, wb, st["tp"])
                _, mixed = _chain_groups(b, h_b)
                scan = rwkv_scan(to_chains(stack, h_b),
                                 _state_to_chains(state_rwkv[:, e]) if latent else None, hs_b, mixed, emit)
                y = from_chains(scan[0], scan[1], b, h_b, hs_b)
                if emit:
                    new_states.setdefault("ret", []).append(ret[2])
                    new_states.setdefault("rwkv", []).append(_state_from_chains(scan[2], b, h_b))
                x = _dense_post(_post_even_kernel, "dense_post_even", x1,
                                [ret[0], ret[1], g_a, y, bonus, g_b], m,
                                [_row(ret_gn_w[e]), _row(rwkv_ln_w[e]), _row(rwkv_ln_b[e]), seg_b,
                                 _row(norm_g[l, 2]), w_out, w1b, w2b],
                                fin, tm, d_ff=w2b.shape[0], dv=dv_a, hs=hs_b)
            else:
                qkv_c, og, q_d, kv_d, gates = parts
                ml = mlstm_scan(qkv_c, gates, bias_row,
                                state_mlstm_c[:, o] if latent else None,
                                state_mlstm_n[:, o] if latent else None,
                                state_mlstm_m[:, o] if latent else None, h_c, dh_c, emit)
                qn, kn, vn = attn_prep(q_d, kv_d, attn_qk_norm[o], hq_d, hkv_d, hd_d, latent, st["tp"])
                out_d = attention(qn, kn, vn, cache_k if latent else None, cache_v if latent else None, o,
                                  hq_d, hkv_d, hd_d, st["tq"], st["nsub"])
                if emit:
                    new_states.setdefault("mc", []).append(ml[2])
                    new_states.setdefault("mn", []).append(ml[3].reshape(b, 2, h_c, dh_c))
                    new_states.setdefault("mm", []).append(ml[4][:, :, 0].reshape(b, 2, h_c))
                    new_states.setdefault("k", []).append(kn)
                    new_states.setdefault("v", []).append(vn)
                x = _dense_post(_post_odd_kernel, "dense_post_odd", x1, [ml[0], ml[1], og, out_d], m,
                                [_row(mlstm_norm_w[o]), _row(norm_g[l, 2]), w_out, w1b, w2b],
                                fin, tm, d_ff=w2b.shape[0], dh=dh_c)
            st["x"] = x

    stack = lambda key: jnp.stack(new_states[key], axis=1)
    return (streams["prompt"]["x"], streams["sample"]["x"], stack("ret"), stack("rwkv"), stack("mc"),
            stack("mn"), stack("mm"), stack("k"), stack("v"))
```

```python
import functools

import jax
import jax.numpy as jnp
from jax import lax
from jax.experimental import pallas as pl
from jax.experimental.pallas import tpu as pltpu

F32 = jnp.float32
BF16 = jnp.bfloat16

GRID_W = 64
CHUNK = 128
ROPE_BASE = 10000.0
RMS_EPS = 1e-6
LOG2E = 1.4426950408889634
N_MOD = 9
LANES = 128
SUBLANES = 8
VMEM_LIMIT = 56 * 1024 * 1024


def _cparams(n_axes):
    return pltpu.CompilerParams(dimension_semantics=("arbitrary",) * n_axes, vmem_limit_bytes=VMEM_LIMIT)


def _bdot(a, b):
    return jnp.dot(a.astype(BF16), b.astype(BF16), preferred_element_type=F32)


def _bdot_nt(a, b):
    return lax.dot_general(a.astype(BF16), b.astype(BF16), (((1,), (1,)), ((), ())),
                           preferred_element_type=F32)


def _bdot_tn(a, b):
    return lax.dot_general(a.astype(BF16), b.astype(BF16), (((0,), (0,)), ((), ())),
                           preferred_element_type=F32)


def _split3(x):
    hi = x.astype(BF16)
    r1 = x - hi.astype(F32)
    mid = r1.astype(BF16)
    lo = (r1 - mid.astype(F32)).astype(BF16)
    return hi, mid, lo


def _dot_exact_rhs(x, e):
    e = e.astype(BF16)
    hi, mid, lo = _split3(x)
    return (jnp.dot(hi, e, preferred_element_type=F32) + jnp.dot(mid, e, preferred_element_type=F32)
            + jnp.dot(lo, e, preferred_element_type=F32))


def _dot_exact_lhs(e, x):
    e = e.astype(BF16)
    hi, mid, lo = _split3(x)
    return (jnp.dot(e, hi, preferred_element_type=F32) + jnp.dot(e, mid, preferred_element_type=F32)
            + jnp.dot(e, lo, preferred_element_type=F32))


def _sigmoid(x):
    return 1.0 / (1.0 + jnp.exp(-x))


def _silu(x):
    return x * _sigmoid(x)


def _softplus(x):
    return jnp.maximum(x, 0.0) + jnp.log(1.0 + jnp.exp(-jnp.abs(x)))


def _rms(x, g):
    return x * lax.rsqrt(jnp.mean(x * x, axis=-1, keepdims=True) + RMS_EPS) * g


def _seg_ones(n, seg):
    r = lax.broadcasted_iota(jnp.int32, (n, n), 0) // seg
    c = lax.broadcasted_iota(jnp.int32, (n, n), 1) // seg
    return (r == c).astype(BF16)


def _full_spec(arr):
    nd = arr.ndim
    return pl.BlockSpec(arr.shape, lambda *_: (0,) * nd, pipeline_mode=pl.Buffered(1))


def _tok_spec(tm, w):
    return pl.BlockSpec((None, tm, w), lambda b, i: (b, i, 0))


def _row(a):
    return a.reshape(1, -1)


def _mod_kernel(c_ref, w_ref, b_ref, o_ref):
    o_ref[...] = _bdot(_silu(c_ref[...]), w_ref[...]) + b_ref[...]


def modulation_all(cond, ada_w, ada_b, tn=1024):
    depth, d, n = ada_w.shape
    rows = cond.shape[0]
    return pl.pallas_call(
        _mod_kernel,
        grid=(depth, n // tn),
        in_specs=[pl.BlockSpec((rows, d), lambda l, j: (0, 0)),
                  pl.BlockSpec((None, d, tn), lambda l, j: (l, 0, j)),
                  pl.BlockSpec((None, 1, tn), lambda l, j: (l, 0, j))],
        out_specs=pl.BlockSpec((None, rows, tn), lambda l, j: (l, 0, j)),
        out_shape=jax.ShapeDtypeStruct((depth, rows, n), F32),
        compiler_params=_cparams(2),
        name="adaln_modulation",
    )(cond, ada_w, ada_b.reshape(depth, 1, n))


def _ff_chunk(d_ff):
    return 1408 if d_ff % 1408 == 0 else d_ff


def _swiglu(h_bf, w1_ref, w2_ref, d_ff):
    fc = _ff_chunk(d_ff)
    acc = None
    for c in range(d_ff // fc):
        gate = jnp.dot(h_bf, w1_ref[:, c * fc:(c + 1) * fc], preferred_element_type=F32)
        up = jnp.dot(h_bf, w1_ref[:, d_ff + c * fc:d_ff + (c + 1) * fc], preferred_element_type=F32)
        a = (_silu(gate) * up).astype(BF16)
        part = jnp.dot(a, w2_ref[c * fc:(c + 1) * fc, :], preferred_element_type=F32)
        acc = part if acc is None else acc + part
    return acc


def _fold_rows(x, mod):
    if mod.shape[0] == 1:
        return x.reshape(1, -1, x.shape[-1])
    return x


def _mod_specs(mod, ks):
    bc, _, d = mod.shape
    mod4 = mod.reshape(bc, N_MOD, 1, d)

    def spec(k):
        if bc == 1:
            return pl.BlockSpec((None, None, 1, d), lambda b, i: (0, k, 0, 0))
        return pl.BlockSpec((None, None, 1, d), lambda b, i: (b, k, 0, 0))

    return [mod4] * len(ks), [spec(k) for k in ks]


def _pre_kernel(x_ref, sh1_ref, sc1_ref, gt1_ref, sh2_ref, sc2_ref, g1_ref, g2_ref,
                w1_ref, w2_ref, win_ref, x_out_ref, *p_refs, d_ff, splits):
    x = x_ref[...]
    h = _rms(x, g1_ref[...]) * (1.0 + sc1_ref[...]) + sh1_ref[...]
    x1 = x + 0.5 * gt1_ref[...] * _swiglu(h.astype(BF16), w1_ref, w2_ref, d_ff)
    x_out_ref[...] = x1
    h2 = (_rms(x1, g2_ref[...]) * (1.0 + sc2_ref[...]) + sh2_ref[...]).astype(BF16)
    off = 0
    for ref, wdt in zip(p_refs, splits):
        ref[...] = jnp.dot(h2, win_ref[:, off:off + wdt], preferred_element_type=F32)
        off += wdt


def dense_pre(x, mod, g1, g2, w1, w2, w_in, splits, tm):
    shape = x.shape
    x = _fold_rows(x, mod)
    b, l, d = x.shape
    tm = min(l, tm)
    margs, mspecs = _mod_specs(mod, (0, 1, 2, 3, 4))
    params = [_row(g1), _row(g2), w1, w2, w_in]
    outs = pl.pallas_call(
        functools.partial(_pre_kernel, d_ff=w2.shape[0], splits=tuple(splits)),
        grid=(b, l // tm),
        in_specs=[_tok_spec(tm, d)] + mspecs + [_full_spec(a) for a in params],
        out_specs=[_tok_spec(tm, d)] + [_tok_spec(tm, w) for w in splits],
        out_shape=[jax.ShapeDtypeStruct((b, l, d), F32)] + [jax.ShapeDtypeStruct((b, l, w), F32) for w in splits],
        compiler_params=_cparams(2),
        name="dense_pre",
    )(x, *margs, *params)
    outs = [o.reshape(shape[:-1] + o.shape[-1:]) for o in outs]
    return outs[0], outs[1:]


def _head_ln(x, width, eps):
    parts = []
    for h in range(x.shape[-1] // width):
        xh = x[:, h * width:(h + 1) * width]
        mu = jnp.mean(xh, axis=-1, keepdims=True)
        xc = xh - mu
        var = jnp.mean(xc * xc, axis=-1, keepdims=True)
        parts.append(xc * lax.rsqrt(var + eps))
    return jnp.concatenate(parts, axis=-1)


def _seg_ln(x, e, width, eps):
    mu = _dot_exact_rhs(x, e) * (1.0 / width)
    xc = x - mu
    var = _dot_exact_rhs(xc * xc, e) * (1.0 / width)
    return xc * lax.rsqrt(var + eps)


def _post_tail(x, mix, gt2_ref, sh3_ref, sc3_ref, gt3_ref, g3_ref, w1_ref, w2_ref, fg_ref, o_ref, d_ff):
    x2 = x + gt2_ref[...] * mix
    h3 = (_rms(x2, g3_ref[...]) * (1.0 + sc3_ref[...]) + sh3_ref[...]).astype(BF16)
    y = x2 + 0.5 * gt3_ref[...] * _swiglu(h3, w1_ref, w2_ref, d_ff)
    if fg_ref is not None:
        y = _rms(y, fg_ref[...])
    o_ref[...] = y


def _post_even_kernel(x_ref, of_ref, ob_ref, ga_ref, y_ref, bon_ref, gb_ref,
                      gt2_ref, sh3_ref, sc3_ref, gt3_ref,
                      gnw_ref, lnw_ref, lnb_ref, seg_ref, g3_ref, wout_ref, w1_ref, w2_ref,
                      *rest, d_ff, dv, hs, final):
    fg_ref = rest[0] if final else None
    o_ref = rest[-1]
    wa = of_ref.shape[-1]
    o = _head_ln(of_ref[...] + ob_ref[...], dv, 1e-5) * gnw_ref[...]
    out_a = _silu(ga_ref[...]) * o
    y = _seg_ln(y_ref[...], seg_ref[...], hs, 64e-5) * lnw_ref[...] + lnb_ref[...]
    out_b = (y + bon_ref[...]) * gb_ref[...]
    mix = _bdot(out_a, wout_ref[0:wa, :]) + _bdot(out_b, wout_ref[wa:, :])
    _post_tail(x_ref[...], mix, gt2_ref, sh3_ref, sc3_ref, gt3_ref, g3_ref, w1_ref, w2_ref, fg_ref, o_ref, d_ff)


def _post_odd_kernel(x_ref, hf_ref, hb_ref, og_ref, od_ref,
                     gt2_ref, sh3_ref, sc3_ref, gt3_ref,
                     nw_ref, g3_ref, wout_ref, w1_ref, w2_ref, *rest, d_ff, dh, final):
    fg_ref = rest[0] if final else None
    o_ref = rest[-1]
    wc = hf_ref.shape[-1]
    h = _head_ln(hf_ref[...] + hb_ref[...], dh, 1e-5) * nw_ref[...]
    out_c = _sigmoid(og_ref[...]) * h
    mix = _bdot(out_c, wout_ref[0:wc, :]) + _bdot(od_ref[...], wout_ref[wc:, :])
    _post_tail(x_ref[...], mix, gt2_ref, sh3_ref, sc3_ref, gt3_ref, g3_ref, w1_ref, w2_ref, fg_ref, o_ref, d_ff)


def _dense_post(kernel_fn, name, x, toks, mod, params, final_g, tm, **kw):
    shape = x.shape
    x = _fold_rows(x, mod)
    toks = [_fold_rows(t, mod) for t in toks]
    b, l, d = x.shape
    tm = min(l, tm)
    margs, mspecs = _mod_specs(mod, (5, 6, 7, 8))
    final = final_g is not None
    params = list(params) + ([_row(final_g)] if final else [])
    return pl.pallas_call(
        functools.partial(kernel_fn, final=final, **kw),
        grid=(b, l // tm),
        in_specs=([_tok_spec(tm, d)] + [_tok_spec(tm, a.shape[-1]) for a in toks] + mspecs
                  + [_full_spec(a) for a in params]),
        out_specs=_tok_spec(tm, d),
        out_shape=jax.ShapeDtypeStruct((b, l, d), F32),
        compiler_params=_cparams(2),
        name=name,
    )(x, *toks, *margs, *params).reshape(shape)


def _rope_tables(n, d, reps):
    rows = n // GRID_W
    row = jnp.repeat(jnp.arange(rows), GRID_W).astype(F32)
    col = (jnp.arange(rows * GRID_W) % GRID_W).astype(F32)
    nf = d // 4
    inv = ROPE_BASE ** (-jnp.arange(nf, dtype=F32) / nf)
    ang_r = row[:, None] * inv[None, :]
    ang_c = col[:, None] * inv[None, :]
    cos = jnp.concatenate([jnp.cos(ang_r), jnp.cos(ang_r), jnp.cos(ang_c), jnp.cos(ang_c)], axis=-1)
    sin = jnp.concatenate([-jnp.sin(ang_r), jnp.sin(ang_r), -jnp.sin(ang_c), jnp.sin(ang_c)], axis=-1)
    return jnp.tile(cos, (1, reps)), jnp.tile(sin, (1, reps))


def _rope(x, cos, sin, nf):
    w = x.shape[-1]
    lane = lax.broadcasted_iota(jnp.int32, x.shape, 1)
    first = (lane % (2 * nf)) < nf
    partner = jnp.where(first, pltpu.roll(x, w - nf, axis=1), pltpu.roll(x, nf, axis=1))
    return x * cos + partner * sin


def _ret_kernel(ld_ref, *refs, nr, h, dk, dv, c, nc, latent, emit_state):
    it = iter(refs)
    qkf_ref, vf_ref, qkb_ref, vb_ref = next(it), next(it), next(it), next(it)
    if latent:
        cosf_ref, sinf_ref, cosb_ref, sinb_ref, s0_ref = next(it), next(it), next(it), next(it), next(it)
    of_ref, ob_ref = next(it), next(it)
    sfin_ref = next(it) if emit_state else None
    z_scr = next(it)

    ci = pl.program_id(1)
    hk = h * dk

    @pl.when(ci == 0)
    def _():
        z_scr[...] = jnp.zeros_like(z_scr)
        if latent:
            for r in range(nr):
                for d in range(2):
                    for hh in range(h):
                        z_scr[r, d, hh, hh * dk:(hh + 1) * dk, :] = s0_ref[r, d, hh]

    ii = lax.broadcasted_iota(jnp.int32, (c, c), 0)
    jj = lax.broadcasted_iota(jnp.int32, (c, c), 1)
    icol = lax.broadcasted_iota(jnp.int32, (c, 1), 0).astype(F32)
    lane = lax.broadcasted_iota(jnp.int32, (1, hk), 1)

    for d, (qk_ref, v_ref, o_ref) in enumerate(((qkf_ref, vf_ref, of_ref), (qkb_ref, vb_ref, ob_ref))):
        diff = (ii - jj) if d == 0 else (jj - ii)
        causal = diff >= 0
        dpos = jnp.where(causal, diff, 0).astype(F32)
        qe = (icol + 1.0) if d == 0 else (c - icol)
        ke = (c - 1.0 - icol) if d == 0 else icol
        tables = []
        for hh in range(h):
            lg = ld_ref[d, hh]
            tables.append((jnp.where(causal, jnp.exp(lg * dpos), 0.0), jnp.exp(lg * qe), jnp.exp(lg * ke),
                           jnp.exp(lg * jnp.full((1, 1), c, F32))))
        for r in range(nr):
            qk = qk_ref[r]
            if latent:
                cos_ref, sin_ref = (cosf_ref, sinf_ref) if d == 0 else (cosb_ref, sinb_ref)
                qk = _rope(qk, cos_ref[...], sin_ref[...], dk // 4)
            q = qk[:, :hk]
            k = qk[:, hk:] * (dk ** -0.5)
            v = v_ref[r]
            outs = []
            for hh in range(h):
                d_intra, q_dec, k_dec, c_dec = tables[hh]
                msk = (lane // dk) == hh
                qh = jnp.where(msk, q, 0.0)
                kh = jnp.where(msk, k, 0.0)
                vh = v[:, hh * dv:(hh + 1) * dv]
                att = _bdot_nt(qh, kh) * d_intra
                z = z_scr[r, d, hh]
                outs.append(_bdot(att, vh) + _bdot(qh * q_dec, z))
                z_scr[r, d, hh] = z * c_dec + _bdot_tn(kh * k_dec, vh)
            o_ref[r] = jnp.concatenate(outs, axis=-1)

    if emit_state:
        @pl.when(ci == nc - 1)
        def _():
            for r in range(nr):
                for d in range(2):
                    for hh in range(h):
                        sfin_ref[r, d, hh] = z_scr[r, d, hh, hh * dk:(hh + 1) * dk, :]


def retention_scan(qk, v, log_decay, s0, h, dk, dv, latent, emit_state, nr=4):
    b, l, _ = qk.shape
    c = CHUNK
    nc = l // c
    hk = h * dk
    fwd = lambda w: pl.BlockSpec((nr, c, w), lambda bb, ci: (bb, ci, 0))
    bwd = lambda w: pl.BlockSpec((nr, c, w), lambda bb, ci: (bb, nc - 1 - ci, 0))
    args = [log_decay, qk, v, qk, v]
    specs = [pl.BlockSpec(memory_space=pltpu.SMEM), fwd(2 * hk), fwd(h * dv), bwd(2 * hk), bwd(h * dv)]
    if latent:
        cos, sin = _rope_tables(l, dk, 2 * h)
        args += [cos, sin, cos, sin, s0]
        tf = pl.BlockSpec((c, 2 * hk), lambda bb, ci: (ci, 0))
        tb = pl.BlockSpec((c, 2 * hk), lambda bb, ci: (nc - 1 - ci, 0))
        specs += [tf, tf, tb, tb, pl.BlockSpec((nr, 2, h, dk, dv), lambda bb, ci: (bb, 0, 0, 0, 0))]
    out_shape = [jax.ShapeDtypeStruct((b, l, h * dv), F32)] * 2
    out_specs = [fwd(h * dv), bwd(h * dv)]
    if emit_state:
        out_shape.append(jax.ShapeDtypeStruct((b, 2, h, dk, dv), F32))
        out_specs.append(pl.BlockSpec((nr, 2, h, dk, dv), lambda bb, ci: (bb, 0, 0, 0, 0)))
    return pl.pallas_call(
        functools.partial(_ret_kernel, nr=nr, h=h, dk=dk, dv=dv, c=c, nc=nc, latent=latent,
                          emit_state=emit_state),
        grid=(b // nr, nc),
        in_specs=specs,
        out_specs=out_specs,
        out_shape=out_shape,
        scratch_shapes=[pltpu.VMEM((nr, 2, h, hk, dv), F32)],
        compiler_params=_cparams(2),
        name="retention_scan",
    )(*args)


def _rwkv_prep_kernel(p_ref, prev_ref, next_ref, mu_ref, w0_ref, a0_ref, w2_ref, a2_ref, g2_ref,
                      kk_ref, ka_ref, rk_ref, seg_ref, s_out, g_out, bon_out, *, tm, nt, wb):
    i = pl.program_id(1)
    p = p_ref[...]
    row = lax.broadcasted_iota(jnp.int32, p.shape, 0)
    prow = jnp.where(i == 0, 0.0, prev_ref[SUBLANES - 1:SUBLANES, :])
    nrow = jnp.where(i == nt - 1, 0.0, next_ref[0:1, :])
    prev = jnp.where(row == 0, prow, pltpu.roll(p, 1, axis=0))
    nxt = jnp.where(row == tm - 1, nrow, pltpu.roll(p, tm - 1, axis=0))
    ps = p + mu_ref[...] * (0.5 * (prev + nxt) - p)
    r = ps[:, 0:wb]
    k = ps[:, wb:2 * wb]
    v = ps[:, 2 * wb:3 * wb]
    lora = ps[:, 3 * wb:]
    tl = jnp.tanh(lora)
    seg = seg_ref[...]
    kk = k * kk_ref[...]
    nrm = jnp.sqrt(_dot_exact_rhs(kk * kk, seg))
    kk = kk / jnp.maximum(nrm, 1e-12)
    s_out[0] = r
    s_out[1] = v
    s_out[2] = -kk
    g_out[...] = _bdot(_sigmoid(lora), g2_ref[...])
    bon_out[...] = _dot_exact_rhs(r * k * rk_ref[...], seg) * v
    for d in range(2):
        w_log = -_softplus(-(w0_ref[d:d + 1, :] + _bdot(tl, w2_ref[d]))) - 0.5
        s_out[3 + 3 * d] = jnp.exp(-jnp.exp(w_log))
        a = _sigmoid(a0_ref[d:d + 1, :] + _bdot(lora, a2_ref[d]))
        s_out[4 + 3 * d] = k * (1.0 + (a - 1.0) * ka_ref[...])
        s_out[5 + 3 * d] = kk * a


def rwkv_prep(p, mu, w0, w2p, a0, a2p, g2p, k_k, k_a, r_k, seg, wb, tm):
    b, l, pw = p.shape
    nt = l // tm
    r8 = tm // SUBLANES
    nb8 = l // SUBLANES
    params = [_row(mu), w0, a0, w2p, a2p, g2p, _row(k_k), _row(k_a), _row(r_k), seg]
    return pl.pallas_call(
        functools.partial(_rwkv_prep_kernel, tm=tm, nt=nt, wb=wb),
        grid=(b, nt),
        in_specs=[_tok_spec(tm, pw),
                  pl.BlockSpec((None, SUBLANES, pw), lambda bb, i: (bb, jnp.maximum(i * r8 - 1, 0), 0)),
                  pl.BlockSpec((None, SUBLANES, pw), lambda bb, i: (bb, jnp.minimum((i + 1) * r8, nb8 - 1), 0))]
                 + [_full_spec(a) for a in params],
        out_specs=[pl.BlockSpec((9, None, tm, wb), lambda bb, i: (0, bb, i, 0))] + [_tok_spec(tm, wb)] * 2,
        out_shape=[jax.ShapeDtypeStruct((9, b, l, wb), F32)] + [jax.ShapeDtypeStruct((b, l, wb), F32)] * 2,
        compiler_params=_cparams(2),
        name="rwkv_prep",
    )(p, p, p, *params)


ROW_PITCH = 72
CHAIN_TILE = 128
N_CHAIN_ARRAYS = 6


def _chain_groups(b, h):
    per_dir = b * h
    if 2 * per_dir == LANES:
        return 1, True
    assert per_dir % LANES == 0
    return 2 * per_dir // LANES, False


def _to_chains_kernel(lo_ref, hi_ref, o_ref, t_scr, *, n, h, nb, tt):
    for half, ref in enumerate((lo_ref, hi_ref)):
        for bb in range(nb):
            xt = ref[bb].T
            for hh in range(h):
                t_scr[half * nb + bb, hh * ROW_PITCH:hh * ROW_PITCH + n, :] = xt[hh * n:(hh + 1) * n]
    for c in range(n):
        q = jnp.concatenate([t_scr[k, pl.ds(c, h, stride=ROW_PITCH), :] for k in range(2 * nb)], axis=0)
        o_ref[pl.ds(c, tt, stride=ROW_PITCH), :] = q.T
    pad = jnp.zeros((ROW_PITCH - n, LANES), F32)
    for t in range(tt):
        o_ref[t * ROW_PITCH + n:(t + 1) * ROW_PITCH, :] = pad


def to_chains(stack, h):
    _, b, l, w = stack.shape
    n = w // h
    nb = LANES // (2 * h)
    tt = CHAIN_TILE
    ng, mixed = _chain_groups(b, h)
    src1 = lambda a: jnp.where(a < 3, a, a + 3)
    if mixed:
        lo_map = lambda a, g, ti: (a, 0, ti, 0)
        hi_map = lambda a, g, ti: (src1(a), 0, ti, 0)
    else:
        src = lambda a, g: jnp.where(g < ng // 2, a, src1(a))
        blocks_per_dir = ng // 2
        lo_map = lambda a, g, ti: (src(a, g), 2 * (g % blocks_per_dir), ti, 0)
        hi_map = lambda a, g, ti: (src(a, g), 2 * (g % blocks_per_dir) + 1, ti, 0)
    out = pl.pallas_call(
        functools.partial(_to_chains_kernel, n=n, h=h, nb=nb, tt=tt),
        grid=(N_CHAIN_ARRAYS, ng, l // tt),
        in_specs=[pl.BlockSpec((None, nb, tt, w), lo_map), pl.BlockSpec((None, nb, tt, w), hi_map)],
        out_specs=pl.BlockSpec((None, None, tt * ROW_PITCH, LANES), lambda a, g, ti: (a, g, ti, 0)),
        out_shape=jax.ShapeDtypeStruct((N_CHAIN_ARRAYS, ng, l * ROW_PITCH, LANES), F32),
        scratch_shapes=[pltpu.VMEM((2 * nb, h * ROW_PITCH, tt), F32)],
        compiler_params=_cparams(3),
        name="to_chains",
    )(stack, stack)
    return out.reshape(N_CHAIN_ARRAYS, ng, l, ROW_PITCH, LANES)


def _from_chains_kernel(a_ref, b_ref, o_ref, t_scr, *, n, h, nb, tt, mixed):
    lane = lax.broadcasted_iota(jnp.int32, (tt, LANES), 1)
    nk = LANES // h
    for c in range(n):
        va = a_ref[pl.ds(c, tt, stride=ROW_PITCH), :]
        vb = b_ref[pl.ds(c, tt, stride=ROW_PITCH), :]
        tiles = [jnp.where(lane >= LANES // 2, vb, va)] if mixed else [va, vb]
        for idx, v in enumerate(tiles):
            vt = v.T
            for k in range(nk):
                t_scr[idx * nk + k, pl.ds(c, h, stride=ROW_PITCH), :] = vt[k * h:(k + 1) * h]
    for bb in range(nb):
        k0, k1 = (bb, nb + bb) if mixed else (bb, nk + bb)
        parts = [t_scr[k0, hh * ROW_PITCH:hh * ROW_PITCH + n, :] + t_scr[k1, hh * ROW_PITCH:hh * ROW_PITCH + n, :]
                 for hh in range(h)]
        o_ref[bb] = jnp.concatenate(parts, axis=0).T


def from_chains(yf, yb, b, h, n):
    ng, l, _, _ = yf.shape
    _, mixed = _chain_groups(b, h)
    tt = CHAIN_TILE
    nk = LANES // h
    nb = nk // 2 if mixed else nk
    yf2 = yf.reshape(ng, l * ROW_PITCH, LANES)
    yb2 = yb.reshape(ng, l * ROW_PITCH, LANES)
    if mixed:
        a_map = lambda gb, ti: (0, ti, 0)
        b_map = a_map
    else:
        a_map = lambda gb, ti: (gb, ti, 0)
        b_map = lambda gb, ti: (ng // 2 + gb, ti, 0)
    spec = lambda m: pl.BlockSpec((None, tt * ROW_PITCH, LANES), m)
    return pl.pallas_call(
        functools.partial(_from_chains_kernel, n=n, h=h, nb=nb, tt=tt, mixed=mixed),
        grid=(b // nb, l // tt),
        in_specs=[spec(a_map), spec(b_map)],
        out_specs=pl.BlockSpec((nb, tt, h * n), lambda gb, ti: (gb, ti, 0)),
        out_shape=jax.ShapeDtypeStruct((b, l, h * n), F32),
        scratch_shapes=[pltpu.VMEM(((1 if mixed else 2) * nk, h * ROW_PITCH, tt), F32)],
        compiler_params=_cparams(2),
        name="from_chains",
    )(yf2, yb2)


def _rwkv_scan_kernel(*refs, n, tt, nt, ng, mixed, has_state, emit_state):
    it = iter(refs)
    fw = [next(it) for _ in range(N_CHAIN_ARRAYS)]
    bw = [next(it) for _ in range(N_CHAIN_ARRAYS)]
    a_next_ref, a_prev_ref = next(it), next(it)
    s0_ref = next(it) if has_state else None
    yf_ref, yb_ref = next(it), next(it)
    sfin_ref = next(it) if emit_state else None
    s_scr, sa_scr, gam_scr = next(it), next(it), next(it)
    r_buf, v_buf, a_buf, an_buf, k_buf, b_buf = (next(it) for _ in range(N_CHAIN_ARRAYS))
    g = pl.program_id(0)
    ti = pl.program_id(1)

    lane = lax.broadcasted_iota(jnp.int32, (n, LANES), 1)
    split = LANES // 2 if mixed else jnp.where(g < ng // 2, LANES, 0)
    is_bwd = lane >= split

    def sel(idx, s):
        return jnp.where(is_bwd, bw[idx][tt - 1 - s, 0:n, :], fw[idx][s, 0:n, :])

    for s in range(tt):
        a_buf[s] = sel(2, s)
    a_after = jnp.where(is_bwd, a_prev_ref[0, 0:n, :], a_next_ref[0, 0:n, :])
    a_buf[tt] = jnp.where(ti == nt - 1, 0.0, a_after)
    gam = jnp.ones((n, LANES), F32)
    for s in range(tt):
        gam = gam * sel(3, s)
        inv = 1.0 / gam
        r_buf[s] = sel(0, s) * gam
        v_buf[s] = sel(1, s)
        k_buf[s] = sel(4, s) * inv
        b_buf[s] = sel(5, s) * inv
        an_buf[s] = a_buf[s + 1] * gam
    gam_scr[...] = gam

    @pl.when(ti == 0)
    def _():
        if has_state:
            acc = jnp.zeros(sa_scr.shape, F32)
            for j in range(n):
                sj = s0_ref[j]
                s_scr[j] = sj
                acc = acc + sj * a_buf[0, j:j + 1, :]
            sa_scr[...] = acc
        else:
            s_scr[...] = jnp.zeros_like(s_scr)
            sa_scr[...] = jnp.zeros_like(sa_scr)

    pad = jnp.zeros((ROW_PITCH - n, LANES), F32)

    def step(t, sa):
        vt = v_buf[t]
        y = jnp.zeros_like(sa)
        san = jnp.zeros_like(sa)
        for j in range(n):
            new = s_scr[j] + (sa * b_buf[t, j:j + 1, :] + vt * k_buf[t, j:j + 1, :])
            s_scr[j] = new
            y = y + new * r_buf[t, j:j + 1, :]
            san = san + new * an_buf[t, j:j + 1, :]
        yf_ref[t, 0:n, :] = y
        yf_ref[t, n:, :] = pad
        yb_ref[tt - 1 - t, 0:n, :] = y
        yb_ref[tt - 1 - t, n:, :] = pad
        return san

    sa_scr[...] = lax.fori_loop(0, tt, step, sa_scr[...])
    for j in range(n):
        s_scr[j] = s_scr[j] * gam_scr[j:j + 1, :]

    if emit_state:
        @pl.when(ti == nt - 1)
        def _():
            sfin_ref[...] = s_scr[...]


def rwkv_scan(chains, s0, n, mixed, emit_state, tt=32):
    _, ng, l, rp, lanes = chains.shape
    nt = l // tt
    has_state = s0 is not None

    def seq(a, rev):
        if rev:
            return pl.BlockSpec((None, None, tt, rp, lanes), lambda g, ti: (a, g, nt - 1 - ti, 0, 0))
        return pl.BlockSpec((None, None, tt, rp, lanes), lambda g, ti: (a, g, ti, 0, 0))

    one = lambda m: pl.BlockSpec((None, None, 1, rp, lanes), m)
    st = pl.BlockSpec((None, n, n, lanes), lambda g, ti: (g, 0, 0, 0))
    yspec_f = pl.BlockSpec((None, tt, rp, lanes), lambda g, ti: (g, ti, 0, 0))
    yspec_b = pl.BlockSpec((None, tt, rp, lanes), lambda g, ti: (g, nt - 1 - ti, 0, 0))
    args = [chains] * (2 * N_CHAIN_ARRAYS + 2)
    specs = ([seq(a, False) for a in range(N_CHAIN_ARRAYS)] + [seq(a, True) for a in range(N_CHAIN_ARRAYS)]
             + [one(lambda g, ti: (2, g, jnp.minimum((ti + 1) * tt, l - 1), 0, 0)),
                one(lambda g, ti: (2, g, jnp.maximum((nt - 1 - ti) * tt - 1, 0), 0, 0))])
    if has_state:
        args.append(s0)
        specs.append(st)
    out_shape = [jax.ShapeDtypeStruct((ng, l, rp, lanes), F32)] * 2
    out_specs = [yspec_f, yspec_b]
    if emit_state:
        out_shape.append(jax.ShapeDtypeStruct((ng, n, n, lanes), F32))
        out_specs.append(st)
    seq_buf = pltpu.VMEM((tt, n, lanes), F32)
    return pl.pallas_call(
        functools.partial(_rwkv_scan_kernel, n=n, tt=tt, nt=nt, ng=ng, mixed=mixed,
                          has_state=has_state, emit_state=emit_state),
        grid=(ng, nt),
        in_specs=specs,
        out_specs=out_specs,
        out_shape=out_shape,
        scratch_shapes=[pltpu.VMEM((n, n, lanes), F32), pltpu.VMEM((n, lanes), F32), pltpu.VMEM((n, lanes), F32),
                        seq_buf, seq_buf, pltpu.VMEM((tt + 1, n, lanes), F32), seq_buf, seq_buf, seq_buf],
        compiler_params=_cparams(2),
        name="rwkv_scan",
    )(*args)


def _state_to_chains(s):
    b, _, h, n, _ = s.shape
    ng = (2 * b * h) // LANES
    return s.transpose(4, 3, 1, 0, 2).reshape(n, n, ng, LANES).transpose(2, 0, 1, 3)


def _state_from_chains(s, b, h):
    ng, n, _, _ = s.shape
    return s.transpose(1, 2, 0, 3).reshape(n, n, 2, b, h).transpose(3, 2, 4, 1, 0)


MLSTM_BIAS_ARG = 4


def _mlstm_kernel(*refs, nr, nc, emit_state, **kw):
    ci = pl.program_id(1)
    rows = [[ref if i == MLSTM_BIAS_ARG else ref.at[r] for i, ref in enumerate(refs)] for r in range(nr)]

    @pl.when(ci == 0)
    def _():
        for row in rows:
            _mlstm_row(*row, phase="load", emit_state=emit_state, **kw)

    for row in rows:
        _mlstm_row(*row, phase="chunk", emit_state=emit_state, **kw)

    if emit_state:
        @pl.when(ci == nc - 1)
        def _():
            for row in rows:
                _mlstm_row(*row, phase="store", emit_state=emit_state, **kw)


def _mlstm_row(*refs, phase, h, dh, c, has_state, emit_state):
    it = iter(refs)
    qkvf_ref, gf_ref, qkvb_ref, gb_ref, bias_ref = (next(it) for _ in range(5))
    if has_state:
        c0_ref, n0_ref, m0_ref = next(it), next(it), next(it)
    hf_ref, hb_ref = next(it), next(it)
    if emit_state:
        cfin_ref, nfin_ref, mfin_ref = next(it), next(it), next(it)
    cma_scr, m_scr = next(it), next(it)

    if phase == "load":
        if has_state:
            row0 = lax.broadcasted_iota(jnp.int32, (dh, dh), 0) == 0
            for s in range(2 * h):
                cma_scr[s, :, 0:dh] = c0_ref[s // h, s % h]
                cma_scr[s, :, dh:] = jnp.where(row0, n0_ref[s:s + 1, :], 0.0).T
            m_scr[...] = m0_ref[...]
        else:
            cma_scr[...] = jnp.zeros_like(cma_scr)
            m_scr[...] = jnp.zeros_like(m_scr)
        return
    if phase == "store":
        for s in range(2 * h):
            cfin_ref[s // h, s % h] = cma_scr[s, :, 0:dh]
            nfin_ref[s:s + 1, :] = cma_scr[s, :, dh:].T[0:1, :]
        mfin_ref[...] = m_scr[...]
        return

    ii = lax.broadcasted_iota(jnp.int32, (c, c), 0)
    jj = lax.broadcasted_iota(jnp.int32, (c, c), 1)
    gl_lane = lax.broadcasted_iota(jnp.int32, (c, LANES), 1)
    is_fg = (gl_lane >= 2 * h) & (gl_lane < 4 * h)
    ones_col = (lax.broadcasted_iota(jnp.int32, (c, dh), 1) == 0).astype(F32)

    for d, (qkv_ref, g_ref, o_ref) in enumerate(((qkvf_ref, gf_ref, hf_ref), (qkvb_ref, gb_ref, hb_ref))):
        causal = (ii >= jj) if d == 0 else (ii <= jj)
        gx = g_ref[...] + bias_ref[...]
        gl = jnp.where(is_fg, -_softplus(-gx), gx)
        bcum = _dot_exact_lhs(causal.astype(BF16), gl)
        glt = gl.T
        brow = _dot_exact_rhs(glt, (~causal).astype(BF16) + (ii == jj).astype(BF16))
        qkv = qkv_ref[...]
        kt_all = (qkv[:, h * dh:2 * h * dh] * (dh ** -0.5)).T
        end = c - 1 if d == 0 else 0
        outs = []
        for hh in range(h):
            ig_c = d * h + hh
            fg_c = 2 * h + d * h + hh
            sr = d * h + hh
            q = qkv[:, hh * dh:(hh + 1) * dh]
            k = qkv[:, (h + hh) * dh:(h + hh + 1) * dh] * (dh ** -0.5)
            v_aug = jnp.concatenate([qkv[:, (2 * h + hh) * dh:(2 * h + hh + 1) * dh], ones_col], axis=-1)
            kt = kt_all[hh * dh:(hh + 1) * dh, :]
            b_col = bcum[:, fg_c:fg_c + 1]
            b_row = brow[fg_c:fg_c + 1, :]
            c_row = glt[ig_c:ig_c + 1, :] - b_row
            m = m_scr[sr:sr + 1, 0:1]
            cma = cma_scr[sr]
            dmat = jnp.where(causal, c_row, -jnp.inf)
            g_col = jnp.maximum(m, jnp.max(dmat, axis=-1, keepdims=True))
            s = _bdot_nt(q, k) * jnp.exp(dmat - g_col)
            nd = _bdot(s, v_aug) + jnp.exp(m - g_col) * _bdot(q, cma)
            den = nd[:, dh:dh + 1]
            outs.append(nd[:, 0:dh] / jnp.maximum(jnp.abs(den), jnp.exp(-(b_col + g_col))))
            bl = b_row[:, end:end + 1]
            wl_row = bl + c_row
            m_new = jnp.maximum(bl + m, jnp.max(wl_row, axis=-1, keepdims=True))
            cma_scr[sr] = jnp.exp(bl + m - m_new) * cma + _bdot(kt * jnp.exp(wl_row - m_new), v_aug)
            m_scr[sr:sr + 1, :] = jnp.broadcast_to(m_new, (1, LANES))
        o_ref[...] = jnp.concatenate(outs, axis=-1)


def mlstm_scan(qkv, gates, bias_row, c0, n0, m0, h, dh, emit_state, nr=4):
    b, l, _ = qkv.shape
    c = CHUNK
    nc = l // c
    has_state = c0 is not None
    w = qkv.shape[-1]
    fwd = lambda ww: pl.BlockSpec((nr, c, ww), lambda bb, ci: (bb, ci, 0))
    bwd = lambda ww: pl.BlockSpec((nr, c, ww), lambda bb, ci: (bb, nc - 1 - ci, 0))
    cspec = pl.BlockSpec((nr, 2, h, dh, dh), lambda bb, ci: (bb, 0, 0, 0, 0))
    rspec = pl.BlockSpec((nr, 2 * h, LANES), lambda bb, ci: (bb, 0, 0))
    args = [qkv, gates, qkv, gates, bias_row]
    specs = [fwd(w), fwd(LANES), bwd(w), bwd(LANES), _full_spec(bias_row)]
    if has_state:
        args += [c0, n0.reshape(b, 2 * h, dh), jnp.broadcast_to(m0.reshape(b, 2 * h, 1), (b, 2 * h, LANES))]
        specs += [cspec, rspec, rspec]
    out_shape = [jax.ShapeDtypeStruct((b, l, h * dh), F32)] * 2
    out_specs = [fwd(h * dh), bwd(h * dh)]
    if emit_state:
        out_shape += [jax.ShapeDtypeStruct((b, 2, h, dh, dh), F32),
                      jax.ShapeDtypeStruct((b, 2 * h, dh), F32), jax.ShapeDtypeStruct((b, 2 * h, LANES), F32)]
        out_specs += [cspec, rspec, rspec]
    return pl.pallas_call(
        functools.partial(_mlstm_kernel, nr=nr, nc=nc, emit_state=emit_state, h=h, dh=dh, c=c,
                          has_state=has_state),
        grid=(b // nr, nc),
        in_specs=specs,
        out_specs=out_specs,
        out_shape=out_shape,
        scratch_shapes=[pltpu.VMEM((nr, 2 * h, dh, 2 * dh), F32), pltpu.VMEM((nr, 2 * h, LANES), F32)],
        compiler_params=_cparams(2),
        name="mlstm_scan",
    )(*args)


def _attn_prep_kernel(*refs, hq, hkv, hd, latent):
    it = iter(refs)
    q_ref, kv_ref, gq_ref, gk_ref, segq_ref, segk_ref = (next(it) for _ in range(6))
    if latent:
        cq_ref, sq_ref, ck_ref, sk_ref = (next(it) for _ in range(4))
    qo_ref, ko_ref, vo_ref = next(it), next(it), next(it)
    q = q_ref[...]
    kv = kv_ref[...]
    kw = hkv * hd
    k = kv[:, :kw]
    v = kv[:, kw:]
    qn = q * lax.rsqrt(_dot_exact_rhs(q * q, segq_ref[...]) * (1.0 / hd) + RMS_EPS) * gq_ref[...]
    kn = k * lax.rsqrt(_dot_exact_rhs(k * k, segk_ref[...]) * (1.0 / hd) + RMS_EPS) * gk_ref[...]
    if latent:
        qn = _rope(qn, cq_ref[...], sq_ref[...], hd // 4)
        kn = _rope(kn, ck_ref[...], sk_ref[...], hd // 4)
    qo_ref[...] = qn * (hd ** -0.5 * LOG2E)
    for j in range(hkv):
        ko_ref[j] = kn[:, j * hd:(j + 1) * hd]
        vo_ref[j] = v[:, j * hd:(j + 1) * hd]


def attn_prep(q, kv, qk_gain, hq, hkv, hd, latent, tm):
    b, l, wq = q.shape
    wkv = kv.shape[-1]
    kw = hkv * hd
    params = [jnp.tile(qk_gain[0], hq).reshape(1, wq), jnp.tile(qk_gain[1], hkv).reshape(1, kw),
              _seg_ones(wq, hd), _seg_ones(kw, hd)]
    args = [q, kv] + params
    specs = [_tok_spec(tm, wq), _tok_spec(tm, wkv)] + [_full_spec(a) for a in params]
    if latent:
        cq, sq = _rope_tables(l, hd, hq)
        ck, sk = _rope_tables(l, hd, hkv)
        args += [cq, sq, ck, sk]
        specs += [pl.BlockSpec((tm, wq), lambda bb, i: (i, 0))] * 2 + [pl.BlockSpec((tm, kw), lambda bb, i: (i, 0))] * 2
    kvspec = pl.BlockSpec((None, hkv, tm, hd), lambda bb, i: (bb, 0, i, 0))
    return pl.pallas_call(
        functools.partial(_attn_prep_kernel, hq=hq, hkv=hkv, hd=hd, latent=latent),
        grid=(b, l // tm),
        in_specs=specs,
        out_specs=[_tok_spec(tm, wq), kvspec, kvspec],
        out_shape=[jax.ShapeDtypeStruct((b, l, wq), F32)] + [jax.ShapeDtypeStruct((b, hkv, l, hd), F32)] * 2,
        compiler_params=_cparams(2),
        name="attn_prep",
    )(*args)


def _attn_kernel(*refs, g, hd, tq, nsub, kb, has_ctx):
    it = iter(refs)
    q_ref, k_ref, v_ref = next(it), next(it), next(it)
    if has_ctx:
        ck_ref, cv_ref = next(it), next(it)
    o_ref = next(it)
    s_scr = next(it)
    sources =([(ck_ref, cv_ref)] if has_ctx else []) + [(k_ref, v_ref)]
    maxes = []
    for u in range(nsub):
        q = q_ref[u * tq:(u + 1) * tq, :]
        qs = jnp.concatenate([q[:, i * hd:(i + 1) * hd] for i in range(g)], axis=0).astype(BF16)
        m = None
        row = 0
        for kr, _ in sources:
            for j in range(kr.shape[0] // kb):
                st = _bdot_nt(kr[j * kb:(j + 1) * kb, :], qs)
                s_scr[u, row:row + kb, :] = st
                bm = jnp.max(st, axis=0, keepdims=True)
                m = bm if m is None else jnp.maximum(m, bm)
                row += kb
        maxes.append(m)
    for u in range(nsub):
        acc = jnp.zeros((hd, g * tq), F32)
        den = jnp.zeros((1, g * tq), F32)
        row = 0
        for _, vr in sources:
            for j in range(vr.shape[0] // kb):
                p = jnp.exp2(s_scr[u, row:row + kb, :] - maxes[u])
                den = den + jnp.sum(p, axis=0, keepdims=True)
                acc = acc + _bdot_tn(vr[j * kb:(j + 1) * kb, :], p)
                row += kb
        ot = acc / den
        o = jnp.concatenate([ot, jnp.zeros((LANES - hd, g * tq), F32)], axis=0).T
        for i in range(g):
            o_ref[u * tq:(u + 1) * tq, i * hd:(i + 1) * hd] = o[i * tq:(i + 1) * tq, 0:hd]


def attention(q, k, v, ctx_k, ctx_v, layer_o, hq, hkv, hd, tq, nsub):
    b, l, wq = q.shape
    g = hq // hkv
    has_ctx = ctx_k is not None
    qspec = pl.BlockSpec((None, nsub * tq, g * hd), lambda bb, j, i: (bb, i, j))
    kvspec = pl.BlockSpec((None, None, l, hd), lambda bb, j, i: (bb, j, 0, 0))
    args = [q, k, v]
    specs = [qspec, kvspec, kvspec]
    n_keys = l
    if has_ctx:
        s = ctx_k.shape[3]
        n_keys += s
        cspec = pl.BlockSpec((None, None, None, s, hd), lambda bb, j, i: (bb, layer_o, j, 0, 0))
        args += [ctx_k, ctx_v]
        specs += [cspec, cspec]
    kb = min(256, l)
    return pl.pallas_call(
        functools.partial(_attn_kernel, g=g, hd=hd, tq=tq, nsub=nsub, kb=kb, has_ctx=has_ctx),
        grid=(b, hkv, l // (nsub * tq)),
        in_specs=specs,
        out_specs=qspec,
        out_shape=jax.ShapeDtypeStruct((b, l, wq), F32),
        scratch_shapes=[pltpu.VMEM((nsub, n_keys, g * tq), F32)],
        compiler_params=_cparams(3),
        name="gqa_attention",
    )(*args)


DENSE_TILE = 512
PREP_TILE = 512
QUERY_TILE = 128
QUERY_SUBTILES = 2


def _tile_plan(seq_len):
    return dict(tm=DENSE_TILE, tp=min(seq_len, PREP_TILE), tq=QUERY_TILE, nsub=QUERY_SUBTILES)


def _pad_rows(w, lo, total):
    return jnp.zeros((total, w.shape[-1]), w.dtype).at[lo:lo + w.shape[0]].set(w)


def kernel(x_prompt, x_sample, state_ret, state_rwkv, state_mlstm_c, state_mlstm_n, state_mlstm_m, cache_k, cache_v, c, c_ctx, ada_w, ada_b, norm_g, ffn_w1, ffn_w2, w_in_even, w_out_even, ret_log_decay, ret_gn_w, rwkv_mu, rwkv_w0, rwkv_w2, rwkv_a0, rwkv_a2, rwkv_g2, rwkv_k_k, rwkv_k_a, rwkv_r_k, rwkv_ln_w, rwkv_ln_b, w_in_odd, w_out_odd, mlstm_i_bias, mlstm_f_bias, mlstm_norm_w, attn_qk_norm, final_norm):
    depth = ada_w.shape[0]
    d_model = x_prompt.shape[-1]
    h_a, dk_a, dv_a = state_ret.shape[3:]
    h_b, hs_b = state_rwkv.shape[3:5]
    h_c, dh_c = state_mlstm_c.shape[3:5]
    hkv_d, hd_d = cache_k.shape[2], cache_k.shape[4]
    wa, wb, wc = h_a * dv_a, h_b * hs_b, h_c * dh_c
    wd = w_out_odd.shape[1] - wc
    hq_d = wd // hd_d
    kvw = hkv_d * hd_d
    n_dec = c.shape[0]
    lora_w, lora_a, lora_g = rwkv_w2.shape[2], rwkv_a2.shape[2], rwkv_g2.shape[1]
    lora_tot = lora_w + lora_a + lora_g

    rows = -(-(n_dec + 1) // SUBLANES) * SUBLANES
    cond = jnp.zeros((rows, d_model), F32).at[:n_dec].set(c).at[n_dec].set(c_ctx)
    mod = modulation_all(cond, ada_w, ada_b).reshape(depth, rows, N_MOD, d_model)

    seg_b = _seg_ones(wb, hs_b)
    streams = {
        "prompt": dict(x=x_prompt, latent=False, **_tile_plan(x_prompt.shape[1])),
        "sample": dict(x=x_sample, latent=True, **_tile_plan(x_sample.shape[1])),
    }
    new_states = {}

    for l in range(depth):
        w1a, w1b = ffn_w1[l, 0].astype(BF16), ffn_w1[l, 1].astype(BF16)
        w2a, w2b = ffn_w2[l, 0].astype(BF16), ffn_w2[l, 1].astype(BF16)
        fin = final_norm if l == depth - 1 else None
        if l % 2 == 0:
            e = l // 2
            w_in = w_in_even[e].astype(BF16)
            w_out = w_out_even[e].astype(BF16)
            splits = (2 * h_a * dk_a, wa, wa, 3 * wb + lora_tot)
            w2p = jnp.stack([_pad_rows(rwkv_w2[e, d], 0, lora_tot) for d in range(2)]).astype(BF16)
            a2p = jnp.stack([_pad_rows(rwkv_a2[e, d], lora_w, lora_tot) for d in range(2)]).astype(BF16)
            g2p = _pad_rows(rwkv_g2[e], lora_w + lora_a, lora_tot).astype(BF16)
        else:
            o = l // 2
            wi = w_in_odd[o]
            g0 = 3 * wc
            p_c = g0 + 4 * h_c + wc
            gates_w = jnp.zeros((d_model, LANES), F32).at[:, :4 * h_c].set(wi[:, g0:g0 + 4 * h_c])
            w_in = jnp.concatenate([wi[:, :g0], wi[:, g0 + 4 * h_c:p_c], wi[:, p_c:p_c + wd],
                                    wi[:, p_c + wd:], gates_w], axis=1).astype(BF16)
            w_out = w_out_odd[o].astype(BF16)
            splits = (3 * wc, wc, wd, 2 * kvw, LANES)
            bias_row = jnp.zeros((1, LANES), F32).at[0, :2 * h_c].set(mlstm_i_bias[o].reshape(-1))
            bias_row = bias_row.at[0, 2 * h_c:4 * h_c].set(mlstm_f_bias[o].reshape(-1))

        for name, st in streams.items():
            x, latent, tm = st["x"], st["latent"], st["tm"]
            b, seq, _ = x.shape
            m = mod[l, :n_dec] if latent else mod[l, n_dec:n_dec + 1]
            emit = not latent
            x1, parts = dense_pre(x, m, norm_g[l, 0], norm_g[l, 1], w1a, w2a, w_in, splits, tm)
            if l % 2 == 0:
                qk_a, v_a, g_a, p_b = parts
                ret = retention_scan(qk_a, v_a, ret_log_decay[e], state_ret[:, e] if latent else None,
                                     h_a, dk_a, dv_a, latent, emit)
                stack, g_b, bonus = rwkv_prep(
                    p_b, rwkv_mu[e], rwkv_w0[e], w2p, rwkv_a0[e], a2p, g2p, rwkv_k_k[e], rwkv_k_a[e],
                    rwkv_r_k[e].reshape(-1), seg_b, wb, st["tp"])
                _, mixed = _chain_groups(b, h_b)
                scan = rwkv_scan(to_chains(stack, h_b),
                                 _state_to_chains(state_rwkv[:, e]) if latent else None, hs_b, mixed, emit)
                y = from_chains(scan[0], scan[1], b, h_b, hs_b)
                if emit:
                    new_states.setdefault("ret", []).append(ret[2])
                    new_states.setdefault("rwkv", []).append(_state_from_chains(scan[2], b, h_b))
                x = _dense_post(_post_even_kernel, "dense_post_even", x1,
                                [ret[0], ret[1], g_a, y, bonus, g_b], m,
                                [_row(ret_gn_w[e]), _row(rwkv_ln_w[e]), _row(rwkv_ln_b[e]), seg_b,
                                 _row(norm_g[l, 2]), w_out, w1b, w2b],
                                fin, tm, d_ff=w2b.shape[0], dv=dv_a, hs=hs_b)
            else:
                qkv_c, og, q_d, kv_d, gates = parts
                ml = mlstm_scan(qkv_c, gates, bias_row,
                                state_mlstm_c[:, o] if latent else None,
                                state_mlstm_n[:, o] if latent else None,
                                state_mlstm_m[:, o] if latent else None, h_c, dh_c, emit)
                qn, kn, vn = attn_prep(q_d, kv_d, attn_qk_norm[o], hq_d, hkv_d, hd_d, latent, st["tp"])
                out_d = attention(qn, kn, vn, cache_k if latent else None, cache_v if latent else None, o,
                                  hq_d, hkv_d, hd_d, st["tq"], st["nsub"])
                if emit:
                    new_states.setdefault("mc", []).append(ml[2])
                    new_states.setdefault("mn", []).append(ml[3].reshape(b, 2, h_c, dh_c))
                    new_states.setdefault("mm", []).append(ml[4][:, :, 0].reshape(b, 2, h_c))
                    new_states.setdefault("k", []).append(kn)
                    new_states.setdefault("v", []).append(vn)
                x = _dense_post(_post_odd_kernel, "dense_post_odd", x1, [ml[0], ml[1], og, out_d], m,
                                [_row(mlstm_norm_w[o]), _row(norm_g[l, 2]), w_out, w1b, w2b],
                                fin, tm, d_ff=w2b.shape[0], dh=dh_c)
            st["x"] = x

    stack = lambda key: jnp.stack(new_states[key], axis=1)
    return (streams["prompt"]["x"], streams["sample"]["x"], stack("ret"), stack("rwkv"), stack("mc"),
            stack("mn"), stack("mm"), stack("k"), stack("v"))
```

```python
import functools

import jax
import jax.numpy as jnp
from jax import lax
from jax.experimental import pallas as pl
from jax.experimental.pallas import tpu as pltpu

F32 = jnp.float32
BF16 = jnp.bfloat16

GRID_W = 64
CHUNK = 128
ROPE_BASE = 10000.0
RMS_EPS = 1e-6
LOG2E = 1.4426950408889634
N_MOD = 9
LANES = 128
SUBLANES = 8
VMEM_LIMIT = 56 * 1024 * 1024


def _cparams(n_axes):
    return pltpu.CompilerParams(dimension_semantics=("arbitrary",) * n_axes, vmem_limit_bytes=VMEM_LIMIT)


def _bdot(a, b):
    return jnp.dot(a.astype(BF16), b.astype(BF16), preferred_element_type=F32)


def _bdot_nt(a, b):
    return lax.dot_general(a.astype(BF16), b.astype(BF16), (((1,), (1,)), ((), ())),
                           preferred_element_type=F32)


def _bdot_tn(a, b):
    return lax.dot_general(a.astype(BF16), b.astype(BF16), (((0,), (0,)), ((), ())),
                           preferred_element_type=F32)


def _split3(x):
    hi = x.astype(BF16)
    r1 = x - hi.astype(F32)
    mid = r1.astype(BF16)
    lo = (r1 - mid.astype(F32)).astype(BF16)
    return hi, mid, lo


def _dot_exact_rhs(x, e):
    e = e.astype(BF16)
    hi, mid, lo = _split3(x)
    return (jnp.dot(hi, e, preferred_element_type=F32) + jnp.dot(mid, e, preferred_element_type=F32)
            + jnp.dot(lo, e, preferred_element_type=F32))


def _dot_exact_lhs(e, x):
    e = e.astype(BF16)
    hi, mid, lo = _split3(x)
    return (jnp.dot(e, hi, preferred_element_type=F32) + jnp.dot(e, mid, preferred_element_type=F32)
            + jnp.dot(e, lo, preferred_element_type=F32))


def _sigmoid(x):
    return 1.0 / (1.0 + jnp.exp(-x))


def _silu(x):
    return x * _sigmoid(x)


def _softplus(x):
    return jnp.maximum(x, 0.0) + jnp.log(1.0 + jnp.exp(-jnp.abs(x)))


def _rms(x, g):
    return x * lax.rsqrt(jnp.mean(x * x, axis=-1, keepdims=True) + RMS_EPS) * g


def _seg_ones(n, seg):
    r = lax.broadcasted_iota(jnp.int32, (n, n), 0) // seg
    c = lax.broadcasted_iota(jnp.int32, (n, n), 1) // seg
    return (r == c).astype(BF16)


def _full_spec(arr):
    nd = arr.ndim
    return pl.BlockSpec(arr.shape, lambda *_: (0,) * nd, pipeline_mode=pl.Buffered(1))


def _tok_spec(tm, w):
    return pl.BlockSpec((None, tm, w), lambda b, i: (b, i, 0))


def _row(a):
    return a.reshape(1, -1)


def _mod_kernel(c_ref, w_ref, b_ref, o_ref):
    o_ref[...] = _bdot(_silu(c_ref[...]), w_ref[...]) + b_ref[...]


def modulation_all(cond, ada_w, ada_b, tn=1024):
    depth, d, n = ada_w.shape
    rows = cond.shape[0]
    return pl.pallas_call(
        _mod_kernel,
        grid=(depth, n // tn),
        in_specs=[pl.BlockSpec((rows, d), lambda l, j: (0, 0)),
                  pl.BlockSpec((None, d, tn), lambda l, j: (l, 0, j)),
                  pl.BlockSpec((None, 1, tn), lambda l, j: (l, 0, j))],
        out_specs=pl.BlockSpec((None, rows, tn), lambda l, j: (l, 0, j)),
        out_shape=jax.ShapeDtypeStruct((depth, rows, n), F32),
        compiler_params=_cparams(2),
        name="adaln_modulation",
    )(cond, ada_w, ada_b.reshape(depth, 1, n))


def _ff_chunk(d_ff):
    return 256 if d_ff % 256 == 0 else d_ff


def _swiglu(h_bf, w1_ref, w2_ref, d_ff):
    fc = _ff_chunk(d_ff)
    acc = None
    for c in range(d_ff // fc):
        gate = jnp.dot(h_bf, w1_ref[:, c * fc:(c + 1) * fc], preferred_element_type=F32)
        up = jnp.dot(h_bf, w1_ref[:, d_ff + c * fc:d_ff + (c + 1) * fc], preferred_element_type=F32)
        a = (_silu(gate) * up).astype(BF16)
        part = jnp.dot(a, w2_ref[c * fc:(c + 1) * fc, :], preferred_element_type=F32)
        acc = part if acc is None else acc + part
    return acc


def _fold_rows(x, mod):
    if mod.shape[0] == 1:
        return x.reshape(1, -1, x.shape[-1])
    return x


def _mod_specs(mod, ks):
    bc, _, d = mod.shape
    mod4 = mod.reshape(bc, N_MOD, 1, d)

    def spec(k):
        if bc == 1:
            return pl.BlockSpec((None, None, 1, d), lambda b, i: (0, k, 0, 0))
        return pl.BlockSpec((None, None, 1, d), lambda b, i: (b, k, 0, 0))

    return [mod4] * len(ks), [spec(k) for k in ks]


def _pre_kernel(x_ref, sh1_ref, sc1_ref, gt1_ref, sh2_ref, sc2_ref, g1_ref, g2_ref,
                w1_ref, w2_ref, win_ref, x_out_ref, *p_refs, d_ff, splits):
    x = x_ref[...]
    h = _rms(x, g1_ref[...]) * (1.0 + sc1_ref[...]) + sh1_ref[...]
    x1 = x + 0.5 * gt1_ref[...] * _swiglu(h.astype(BF16), w1_ref, w2_ref, d_ff)
    x_out_ref[...] = x1
    h2 = (_rms(x1, g2_ref[...]) * (1.0 + sc2_ref[...]) + sh2_ref[...]).astype(BF16)
    off = 0
    for ref, wdt in zip(p_refs, splits):
        ref[...] = jnp.dot(h2, win_ref[:, off:off + wdt], preferred_element_type=F32)
        off += wdt


def dense_pre(x, mod, g1, g2, w1, w2, w_in, splits, tm):
    shape = x.shape
    x = _fold_rows(x, mod)
    b, l, d = x.shape
    tm = min(l, tm)
    margs, mspecs = _mod_specs(mod, (0, 1, 2, 3, 4))
    params = [_row(g1), _row(g2), w1, w2, w_in]
    outs = pl.pallas_call(
        functools.partial(_pre_kernel, d_ff=w2.shape[0], splits=tuple(splits)),
        grid=(b, l // tm),
        in_specs=[_tok_spec(tm, d)] + mspecs + [_full_spec(a) for a in params],
        out_specs=[_tok_spec(tm, d)] + [_tok_spec(tm, w) for w in splits],
        out_shape=[jax.ShapeDtypeStruct((b, l, d), F32)] + [jax.ShapeDtypeStruct((b, l, w), F32) for w in splits],
        compiler_params=_cparams(2),
        name="dense_pre",
    )(x, *margs, *params)
    outs = [o.reshape(shape[:-1] + o.shape[-1:]) for o in outs]
    return outs[0], outs[1:]


def _head_ln(x, width, eps):
    parts = []
    for h in range(x.shape[-1] // width):
        xh = x[:, h * width:(h + 1) * width]
        mu = jnp.mean(xh, axis=-1, keepdims=True)
        xc = xh - mu
        var = jnp.mean(xc * xc, axis=-1, keepdims=True)
        parts.append(xc * lax.rsqrt(var + eps))
    return jnp.concatenate(parts, axis=-1)


def _seg_ln(x, e, width, eps):
    mu = _dot_exact_rhs(x, e) * (1.0 / width)
    xc = x - mu
    var = _dot_exact_rhs(xc * xc, e) * (1.0 / width)
    return xc * lax.rsqrt(var + eps)


def _post_tail(x, mix, gt2_ref, sh3_ref, sc3_ref, gt3_ref, g3_ref, w1_ref, w2_ref, fg_ref, o_ref, d_ff):
    x2 = x + gt2_ref[...] * mix
    h3 = (_rms(x2, g3_ref[...]) * (1.0 + sc3_ref[...]) + sh3_ref[...]).astype(BF16)
    y = x2 + 0.5 * gt3_ref[...] * _swiglu(h3, w1_ref, w2_ref, d_ff)
    if fg_ref is not None:
        y = _rms(y, fg_ref[...])
    o_ref[...] = y


def _post_even_kernel(x_ref, of_ref, ob_ref, ga_ref, y_ref, bon_ref, gb_ref,
                      gt2_ref, sh3_ref, sc3_ref, gt3_ref,
                      gnw_ref, lnw_ref, lnb_ref, seg_ref, g3_ref, wout_ref, w1_ref, w2_ref,
                      *rest, d_ff, dv, hs, final):
    fg_ref = rest[0] if final else None
    o_ref = rest[-1]
    wa = of_ref.shape[-1]
    o = _head_ln(of_ref[...] + ob_ref[...], dv, 1e-5) * gnw_ref[...]
    out_a = _silu(ga_ref[...]) * o
    y = _seg_ln(y_ref[...], seg_ref[...], hs, 64e-5) * lnw_ref[...] + lnb_ref[...]
    out_b = (y + bon_ref[...]) * gb_ref[...]
    mix = _bdot(out_a, wout_ref[0:wa, :]) + _bdot(out_b, wout_ref[wa:, :])
    _post_tail(x_ref[...], mix, gt2_ref, sh3_ref, sc3_ref, gt3_ref, g3_ref, w1_ref, w2_ref, fg_ref, o_ref, d_ff)


def _post_odd_kernel(x_ref, hf_ref, hb_ref, og_ref, od_ref,
                     gt2_ref, sh3_ref, sc3_ref, gt3_ref,
                     nw_ref, g3_ref, wout_ref, w1_ref, w2_ref, *rest, d_ff, dh, final):
    fg_ref = rest[0] if final else None
    o_ref = rest[-1]
    wc = hf_ref.shape[-1]
    h = _head_ln(hf_ref[...] + hb_ref[...], dh, 1e-5) * nw_ref[...]
    out_c = _sigmoid(og_ref[...]) * h
    mix = _bdot(out_c, wout_ref[0:wc, :]) + _bdot(od_ref[...], wout_ref[wc:, :])
    _post_tail(x_ref[...], mix, gt2_ref, sh3_ref, sc3_ref, gt3_ref, g3_ref, w1_ref, w2_ref, fg_ref, o_ref, d_ff)


def _dense_post(kernel_fn, name, x, toks, mod, params, final_g, tm, **kw):
    shape = x.shape
    x = _fold_rows(x, mod)
    toks = [_fold_rows(t, mod) for t in toks]
    b, l, d = x.shape
    tm = min(l, tm)
    margs, mspecs = _mod_specs(mod, (5, 6, 7, 8))
    final = final_g is not None
    params = list(params) + ([_row(final_g)] if final else [])
    return pl.pallas_call(
        functools.partial(kernel_fn, final=final, **kw),
        grid=(b, l // tm),
        in_specs=([_tok_spec(tm, d)] + [_tok_spec(tm, a.shape[-1]) for a in toks] + mspecs
                  + [_full_spec(a) for a in params]),
        out_specs=_tok_spec(tm, d),
        out_shape=jax.ShapeDtypeStruct((b, l, d), F32),
        compiler_params=_cparams(2),
        name=name,
    )(x, *toks, *margs, *params).reshape(shape)


def _rope_tables(n, d, reps):
    rows = n // GRID_W
    row = jnp.repeat(jnp.arange(rows), GRID_W).astype(F32)
    col = (jnp.arange(rows * GRID_W) % GRID_W).astype(F32)
    nf = d // 4
    inv = ROPE_BASE ** (-jnp.arange(nf, dtype=F32) / nf)
    ang_r = row[:, None] * inv[None, :]
    ang_c = col[:, None] * inv[None, :]
    cos = jnp.concatenate([jnp.cos(ang_r), jnp.cos(ang_r), jnp.cos(ang_c), jnp.cos(ang_c)], axis=-1)
    sin = jnp.concatenate([-jnp.sin(ang_r), jnp.sin(ang_r), -jnp.sin(ang_c), jnp.sin(ang_c)], axis=-1)
    return jnp.tile(cos, (1, reps)), jnp.tile(sin, (1, reps))


def _rope(x, cos, sin, nf):
    w = x.shape[-1]
    lane = lax.broadcasted_iota(jnp.int32, x.shape, 1)
    first = (lane % (2 * nf)) < nf
    partner = jnp.where(first, pltpu.roll(x, w - nf, axis=1), pltpu.roll(x, nf, axis=1))
    return x * cos + partner * sin


def _ret_kernel(ld_ref, *refs, nr, h, dk, dv, c, nc, latent, emit_state):
    it = iter(refs)
    qkf_ref, vf_ref, qkb_ref, vb_ref = next(it), next(it), next(it), next(it)
    if latent:
        cosf_ref, sinf_ref, cosb_ref, sinb_ref, s0_ref = next(it), next(it), next(it), next(it), next(it)
    of_ref, ob_ref = next(it), next(it)
    sfin_ref = next(it) if emit_state else None
    z_scr = next(it)

    ci = pl.program_id(1)
    hk = h * dk

    @pl.when(ci == 0)
    def _():
        z_scr[...] = jnp.zeros_like(z_scr)
        if latent:
            for r in range(nr):
                for d in range(2):
                    for hh in range(h):
                        z_scr[r, d, hh, hh * dk:(hh + 1) * dk, :] = s0_ref[r, d, hh]

    ii = lax.broadcasted_iota(jnp.int32, (c, c), 0)
    jj = lax.broadcasted_iota(jnp.int32, (c, c), 1)
    icol = lax.broadcasted_iota(jnp.int32, (c, 1), 0).astype(F32)
    lane = lax.broadcasted_iota(jnp.int32, (1, hk), 1)

    for d, (qk_ref, v_ref, o_ref) in enumerate(((qkf_ref, vf_ref, of_ref), (qkb_ref, vb_ref, ob_ref))):
        diff = (ii - jj) if d == 0 else (jj - ii)
        causal = diff >= 0
        dpos = jnp.where(causal, diff, 0).astype(F32)
        qe = (icol + 1.0) if d == 0 else (c - icol)
        ke = (c - 1.0 - icol) if d == 0 else icol
        tables = []
        for hh in range(h):
            lg = ld_ref[d, hh]
            tables.append((jnp.where(causal, jnp.exp(lg * dpos), 0.0), jnp.exp(lg * qe), jnp.exp(lg * ke),
                           jnp.exp(lg * jnp.full((1, 1), c, F32))))
        for r in range(nr):
            qk = qk_ref[r]
            if latent:
                cos_ref, sin_ref = (cosf_ref, sinf_ref) if d == 0 else (cosb_ref, sinb_ref)
                qk = _rope(qk, cos_ref[...], sin_ref[...], dk // 4)
            q = qk[:, :hk]
            k = qk[:, hk:] * (dk ** -0.5)
            v = v_ref[r]
            outs = []
            for hh in range(h):
                d_intra, q_dec, k_dec, c_dec = tables[hh]
                msk = (lane // dk) == hh
                qh = jnp.where(msk, q, 0.0)
                kh = jnp.where(msk, k, 0.0)
                vh = v[:, hh * dv:(hh + 1) * dv]
                att = _bdot_nt(qh, kh) * d_intra
                z = z_scr[r, d, hh]
                outs.append(_bdot(att, vh) + _bdot(qh * q_dec, z))
                z_scr[r, d, hh] = z * c_dec + _bdot_tn(kh * k_dec, vh)
            o_ref[r] = jnp.concatenate(outs, axis=-1)

    if emit_state:
        @pl.when(ci == nc - 1)
        def _():
            for r in range(nr):
                for d in range(2):
                    for hh in range(h):
                        sfin_ref[r, d, hh] = z_scr[r, d, hh, hh * dk:(hh + 1) * dk, :]


def retention_scan(qk, v, log_decay, s0, h, dk, dv, latent, emit_state, nr=4):
    b, l, _ = qk.shape
    c = CHUNK
    nc = l // c
    hk = h * dk
    fwd = lambda w: pl.BlockSpec((nr, c, w), lambda bb, ci: (bb, ci, 0))
    bwd = lambda w: pl.BlockSpec((nr, c, w), lambda bb, ci: (bb, nc - 1 - ci, 0))
    args = [log_decay, qk, v, qk, v]
    specs = [pl.BlockSpec(memory_space=pltpu.SMEM), fwd(2 * hk), fwd(h * dv), bwd(2 * hk), bwd(h * dv)]
    if latent:
        cos, sin = _rope_tables(l, dk, 2 * h)
        args += [cos, sin, cos, sin, s0]
        tf = pl.BlockSpec((c, 2 * hk), lambda bb, ci: (ci, 0))
        tb = pl.BlockSpec((c, 2 * hk), lambda bb, ci: (nc - 1 - ci, 0))
        specs += [tf, tf, tb, tb, pl.BlockSpec((nr, 2, h, dk, dv), lambda bb, ci: (bb, 0, 0, 0, 0))]
    out_shape = [jax.ShapeDtypeStruct((b, l, h * dv), F32)] * 2
    out_specs = [fwd(h * dv), bwd(h * dv)]
    if emit_state:
        out_shape.append(jax.ShapeDtypeStruct((b, 2, h, dk, dv), F32))
        out_specs.append(pl.BlockSpec((nr, 2, h, dk, dv), lambda bb, ci: (bb, 0, 0, 0, 0)))
    return pl.pallas_call(
        functools.partial(_ret_kernel, nr=nr, h=h, dk=dk, dv=dv, c=c, nc=nc, latent=latent,
                          emit_state=emit_state),
        grid=(b // nr, nc),
        in_specs=specs,
        out_specs=out_specs,
        out_shape=out_shape,
        scratch_shapes=[pltpu.VMEM((nr, 2, h, hk, dv), F32)],
        compiler_params=_cparams(2),
        name="retention_scan",
    )(*args)


def _rwkv_prep_kernel(p_ref, prev_ref, next_ref, mu_ref, w0_ref, a0_ref, w2_ref, a2_ref, g2_ref,
                      kk_ref, ka_ref, rk_ref, seg_ref, s_out, g_out, bon_out, *, tm, nt, wb):
    i = pl.program_id(1)
    p = p_ref[...]
    row = lax.broadcasted_iota(jnp.int32, p.shape, 0)
    prow = jnp.where(i == 0, 0.0, prev_ref[SUBLANES - 1:SUBLANES, :])
    nrow = jnp.where(i == nt - 1, 0.0, next_ref[0:1, :])
    prev = jnp.where(row == 0, prow, pltpu.roll(p, 1, axis=0))
    nxt = jnp.where(row == tm - 1, nrow, pltpu.roll(p, tm - 1, axis=0))
    ps = p + mu_ref[...] * (0.5 * (prev + nxt) - p)
    r = ps[:, 0:wb]
    k = ps[:, wb:2 * wb]
    v = ps[:, 2 * wb:3 * wb]
    lora = ps[:, 3 * wb:]
    tl = jnp.tanh(lora)
    seg = seg_ref[...]
    kk = k * kk_ref[...]
    nrm = jnp.sqrt(_dot_exact_rhs(kk * kk, seg))
    kk = kk / jnp.maximum(nrm, 1e-12)
    s_out[0] = r
    s_out[1] = v
    s_out[2] = -kk
    g_out[...] = _bdot(_sigmoid(lora), g2_ref[...])
    bon_out[...] = _dot_exact_rhs(r * k * rk_ref[...], seg) * v
    for d in range(2):
        w_log = -_softplus(-(w0_ref[d:d + 1, :] + _bdot(tl, w2_ref[d]))) - 0.5
        s_out[3 + 3 * d] = jnp.exp(-jnp.exp(w_log))
        a = _sigmoid(a0_ref[d:d + 1, :] + _bdot(lora, a2_ref[d]))
        s_out[4 + 3 * d] = k * (1.0 + (a - 1.0) * ka_ref[...])
        s_out[5 + 3 * d] = kk * a


def rwkv_prep(p, mu, w0, w2p, a0, a2p, g2p, k_k, k_a, r_k, seg, wb, tm):
    b, l, pw = p.shape
    nt = l // tm
    r8 = tm // SUBLANES
    nb8 = l // SUBLANES
    params = [_row(mu), w0, a0, w2p, a2p, g2p, _row(k_k), _row(k_a), _row(r_k), seg]
    return pl.pallas_call(
        functools.partial(_rwkv_prep_kernel, tm=tm, nt=nt, wb=wb),
        grid=(b, nt),
        in_specs=[_tok_spec(tm, pw),
                  pl.BlockSpec((None, SUBLANES, pw), lambda bb, i: (bb, jnp.maximum(i * r8 - 1, 0), 0)),
                  pl.BlockSpec((None, SUBLANES, pw), lambda bb, i: (bb, jnp.minimum((i + 1) * r8, nb8 - 1), 0))]
                 + [_full_spec(a) for a in params],
        out_specs=[pl.BlockSpec((9, None, tm, wb), lambda bb, i: (0, bb, i, 0))] + [_tok_spec(tm, wb)] * 2,
        out_shape=[jax.ShapeDtypeStruct((9, b, l, wb), F32)] + [jax.ShapeDtypeStruct((b, l, wb), F32)] * 2,
        compiler_params=_cparams(2),
        name="rwkv_prep",
    )(p, p, p, *params)


ROW_PITCH = 72
CHAIN_TILE = 128
N_CHAIN_ARRAYS = 6


def _chain_groups(b, h):
    per_dir = b * h
    if 2 * per_dir == LANES:
        return 1, True
    assert per_dir % LANES == 0
    return 2 * per_dir // LANES, False


def _to_chains_kernel(lo_ref, hi_ref, o_ref, t_scr, *, n, h, nb, tt):
    for half, ref in enumerate((lo_ref, hi_ref)):
        for bb in range(nb):
            xt = ref[bb].T
            for hh in range(h):
                t_scr[half * nb + bb, hh * ROW_PITCH:hh * ROW_PITCH + n, :] = xt[hh * n:(hh + 1) * n]
    for c in range(n):
        q = jnp.concatenate([t_scr[k, pl.ds(c, h, stride=ROW_PITCH), :] for k in range(2 * nb)], axis=0)
        o_ref[pl.ds(c, tt, stride=ROW_PITCH), :] = q.T
    pad = jnp.zeros((ROW_PITCH - n, LANES), F32)
    for t in range(tt):
        o_ref[t * ROW_PITCH + n:(t + 1) * ROW_PITCH, :] = pad


def to_chains(stack, h):
    _, b, l, w = stack.shape
    n = w // h
    nb = LANES // (2 * h)
    tt = CHAIN_TILE
    ng, mixed = _chain_groups(b, h)
    src1 = lambda a: jnp.where(a < 3, a, a + 3)
    if mixed:
        lo_map = lambda a, g, ti: (a, 0, ti, 0)
        hi_map = lambda a, g, ti: (src1(a), 0, ti, 0)
    else:
        src = lambda a, g: jnp.where(g < ng // 2, a, src1(a))
        blocks_per_dir = ng // 2
        lo_map = lambda a, g, ti: (src(a, g), 2 * (g % blocks_per_dir), ti, 0)
        hi_map = lambda a, g, ti: (src(a, g), 2 * (g % blocks_per_dir) + 1, ti, 0)
    out = pl.pallas_call(
        functools.partial(_to_chains_kernel, n=n, h=h, nb=nb, tt=tt),
        grid=(N_CHAIN_ARRAYS, ng, l // tt),
        in_specs=[pl.BlockSpec((None, nb, tt, w), lo_map), pl.BlockSpec((None, nb, tt, w), hi_map)],
        out_specs=pl.BlockSpec((None, None, tt * ROW_PITCH, LANES), lambda a, g, ti: (a, g, ti, 0)),
        out_shape=jax.ShapeDtypeStruct((N_CHAIN_ARRAYS, ng, l * ROW_PITCH, LANES), F32),
        scratch_shapes=[pltpu.VMEM((2 * nb, h * ROW_PITCH, tt), F32)],
        compiler_params=_cparams(3),
        name="to_chains",
    )(stack, stack)
    return out.reshape(N_CHAIN_ARRAYS, ng, l, ROW_PITCH, LANES)


def _from_chains_kernel(a_ref, b_ref, o_ref, t_scr, *, n, h, nb, tt, mixed):
    lane = lax.broadcasted_iota(jnp.int32, (tt, LANES), 1)
    nk = LANES // h
    for c in range(n):
        va = a_ref[pl.ds(c, tt, stride=ROW_PITCH), :]
        vb = b_ref[pl.ds(c, tt, stride=ROW_PITCH), :]
        tiles = [jnp.where(lane >= LANES // 2, vb, va)] if mixed else [va, vb]
        for idx, v in enumerate(tiles):
            vt = v.T
            for k in range(nk):
                t_scr[idx * nk + k, pl.ds(c, h, stride=ROW_PITCH), :] = vt[k * h:(k + 1) * h]
    for bb in range(nb):
        k0, k1 = (bb, nb + bb) if mixed else (bb, nk + bb)
        parts = [t_scr[k0, hh * ROW_PITCH:hh * ROW_PITCH + n, :] + t_scr[k1, hh * ROW_PITCH:hh * ROW_PITCH + n, :]
                 for hh in range(h)]
        o_ref[bb] = jnp.concatenate(parts, axis=0).T


def from_chains(yf, yb, b, h, n):
    ng, l, _, _ = yf.shape
    _, mixed = _chain_groups(b, h)
    tt = CHAIN_TILE
    nk = LANES // h
    nb = nk // 2 if mixed else nk
    yf2 = yf.reshape(ng, l * ROW_PITCH, LANES)
    yb2 = yb.reshape(ng, l * ROW_PITCH, LANES)
    if mixed:
        a_map = lambda gb, ti: (0, ti, 0)
        b_map = a_map
    else:
        a_map = lambda gb, ti: (gb, ti, 0)
        b_map = lambda gb, ti: (ng // 2 + gb, ti, 0)
    spec = lambda m: pl.BlockSpec((None, tt * ROW_PITCH, LANES), m)
    return pl.pallas_call(
        functools.partial(_from_chains_kernel, n=n, h=h, nb=nb, tt=tt, mixed=mixed),
        grid=(b // nb, l // tt),
        in_specs=[spec(a_map), spec(b_map)],
        out_specs=pl.BlockSpec((nb, tt, h * n), lambda gb, ti: (gb, ti, 0)),
        out_shape=jax.ShapeDtypeStruct((b, l, h * n), F32),
        scratch_shapes=[pltpu.VMEM(((1 if mixed else 2) * nk, h * ROW_PITCH, tt), F32)],
        compiler_params=_cparams(2),
        name="from_chains",
    )(yf2, yb2)


def _rwkv_scan_kernel(*refs, n, tt, nt, ng, mixed, has_state, emit_state):
    it = iter(refs)
    fw = [next(it) for _ in range(N_CHAIN_ARRAYS)]
    bw = [next(it) for _ in range(N_CHAIN_ARRAYS)]
    a_next_ref, a_prev_ref = next(it), next(it)
    s0_ref = next(it) if has_state else None
    yf_ref, yb_ref = next(it), next(it)
    sfin_ref = next(it) if emit_state else None
    s_scr, sa_scr, gam_scr = next(it), next(it), next(it)
    r_buf, v_buf, a_buf, an_buf, k_buf, b_buf = (next(it) for _ in range(N_CHAIN_ARRAYS))
    g = pl.program_id(0)
    ti = pl.program_id(1)

    lane = lax.broadcasted_iota(jnp.int32, (n, LANES), 1)
    split = LANES // 2 if mixed else jnp.where(g < ng // 2, LANES, 0)
    is_bwd = lane >= split

    def sel(idx, s):
        return jnp.where(is_bwd, bw[idx][tt - 1 - s, 0:n, :], fw[idx][s, 0:n, :])

    for s in range(tt):
        a_buf[s] = sel(2, s)
    a_after = jnp.where(is_bwd, a_prev_ref[0, 0:n, :], a_next_ref[0, 0:n, :])
    a_buf[tt] = jnp.where(ti == nt - 1, 0.0, a_after)
    gam = jnp.ones((n, LANES), F32)
    for s in range(tt):
        gam = gam * sel(3, s)
        inv = 1.0 / gam
        r_buf[s] = sel(0, s) * gam
        v_buf[s] = sel(1, s)
        k_buf[s] = sel(4, s) * inv
        b_buf[s] = sel(5, s) * inv
        an_buf[s] = a_buf[s + 1] * gam
    gam_scr[...] = gam

    @pl.when(ti == 0)
    def _():
        if has_state:
            acc = jnp.zeros(sa_scr.shape, F32)
            for j in range(n):
                sj = s0_ref[j]
                s_scr[j] = sj
                acc = acc + sj * a_buf[0, j:j + 1, :]
            sa_scr[...] = acc
        else:
            s_scr[...] = jnp.zeros_like(s_scr)
            sa_scr[...] = jnp.zeros_like(sa_scr)

    pad = jnp.zeros((ROW_PITCH - n, LANES), F32)

    def step(t, sa):
        vt = v_buf[t]
        y = jnp.zeros_like(sa)
        san = jnp.zeros_like(sa)
        for j in range(n):
            new = s_scr[j] + (sa * b_buf[t, j:j + 1, :] + vt * k_buf[t, j:j + 1, :])
            s_scr[j] = new
            y = y + new * r_buf[t, j:j + 1, :]
            san = san + new * an_buf[t, j:j + 1, :]
        yf_ref[t, 0:n, :] = y
        yf_ref[t, n:, :] = pad
        yb_ref[tt - 1 - t, 0:n, :] = y
        yb_ref[tt - 1 - t, n:, :] = pad
        return san

    sa_scr[...] = lax.fori_loop(0, tt, step, sa_scr[...])
    for j in range(n):
        s_scr[j] = s_scr[j] * gam_scr[j:j + 1, :]

    if emit_state:
        @pl.when(ti == nt - 1)
        def _():
            sfin_ref[...] = s_scr[...]


def rwkv_scan(chains, s0, n, mixed, emit_state, tt=32):
    _, ng, l, rp, lanes = chains.shape
    nt = l // tt
    has_state = s0 is not None

    def seq(a, rev):
        if rev:
            return pl.BlockSpec((None, None, tt, rp, lanes), lambda g, ti: (a, g, nt - 1 - ti, 0, 0))
        return pl.BlockSpec((None, None, tt, rp, lanes), lambda g, ti: (a, g, ti, 0, 0))

    one = lambda m: pl.BlockSpec((None, None, 1, rp, lanes), m)
    st = pl.BlockSpec((None, n, n, lanes), lambda g, ti: (g, 0, 0, 0))
    yspec_f = pl.BlockSpec((None, tt, rp, lanes), lambda g, ti: (g, ti, 0, 0))
    yspec_b = pl.BlockSpec((None, tt, rp, lanes), lambda g, ti: (g, nt - 1 - ti, 0, 0))
    args = [chains] * (2 * N_CHAIN_ARRAYS + 2)
    specs = ([seq(a, False) for a in range(N_CHAIN_ARRAYS)] + [seq(a, True) for a in range(N_CHAIN_ARRAYS)]
             + [one(lambda g, ti: (2, g, jnp.minimum((ti + 1) * tt, l - 1), 0, 0)),
                one(lambda g, ti: (2, g, jnp.maximum((nt - 1 - ti) * tt - 1, 0), 0, 0))])
    if has_state:
        args.append(s0)
        specs.append(st)
    out_shape = [jax.ShapeDtypeStruct((ng, l, rp, lanes), F32)] * 2
    out_specs = [yspec_f, yspec_b]
    if emit_state:
        out_shape.append(jax.ShapeDtypeStruct((ng, n, n, lanes), F32))
        out_specs.append(st)
    seq_buf = pltpu.VMEM((tt, n, lanes), F32)
    return pl.pallas_call(
        functools.partial(_rwkv_scan_kernel, n=n, tt=tt, nt=nt, ng=ng, mixed=mixed,
                          has_state=has_state, emit_state=emit_state),
        grid=(ng, nt),
        in_specs=specs,
        out_specs=out_specs,
        out_shape=out_shape,
        scratch_shapes=[pltpu.VMEM((n, n, lanes), F32), pltpu.VMEM((n, lanes), F32), pltpu.VMEM((n, lanes), F32),
                        seq_buf, seq_buf, pltpu.VMEM((tt + 1, n, lanes), F32), seq_buf, seq_buf, seq_buf],
        compiler_params=_cparams(2),
        name="rwkv_scan",
    )(*args)


def _state_to_chains(s):
    b, _, h, n, _ = s.shape
    ng = (2 * b * h) // LANES
    return s.transpose(4, 3, 1, 0, 2).reshape(n, n, ng, LANES).transpose(2, 0, 1, 3)


def _state_from_chains(s, b, h):
    ng, n, _, _ = s.shape
    return s.transpose(1, 2, 0, 3).reshape(n, n, 2, b, h).transpose(3, 2, 4, 1, 0)


MLSTM_BIAS_ARG = 4


def _mlstm_kernel(*refs, nr, nc, emit_state, **kw):
    ci = pl.program_id(1)
    rows = [[ref if i == MLSTM_BIAS_ARG else ref.at[r] for i, ref in enumerate(refs)] for r in range(nr)]

    @pl.when(ci == 0)
    def _():
        for row in rows:
            _mlstm_row(*row, phase="load", emit_state=emit_state, **kw)

    for row in rows:
        _mlstm_row(*row, phase="chunk", emit_state=emit_state, **kw)

    if emit_state:
        @pl.when(ci == nc - 1)
        def _():
            for row in rows:
                _mlstm_row(*row, phase="store", emit_state=emit_state, **kw)


def _mlstm_row(*refs, phase, h, dh, c, has_state, emit_state):
    it = iter(refs)
    qkvf_ref, gf_ref, qkvb_ref, gb_ref, bias_ref = (next(it) for _ in range(5))
    if has_state:
        c0_ref, n0_ref, m0_ref = next(it), next(it), next(it)
    hf_ref, hb_ref = next(it), next(it)
    if emit_state:
        cfin_ref, nfin_ref, mfin_ref = next(it), next(it), next(it)
    cma_scr, m_scr = next(it), next(it)

    if phase == "load":
        if has_state:
            row0 = lax.broadcasted_iota(jnp.int32, (dh, dh), 0) == 0
            for s in range(2 * h):
                cma_scr[s, :, 0:dh] = c0_ref[s // h, s % h]
                cma_scr[s, :, dh:] = jnp.where(row0, n0_ref[s:s + 1, :], 0.0).T
            m_scr[...] = m0_ref[...]
        else:
            cma_scr[...] = jnp.zeros_like(cma_scr)
            m_scr[...] = jnp.zeros_like(m_scr)
        return
    if phase == "store":
        for s in range(2 * h):
            cfin_ref[s // h, s % h] = cma_scr[s, :, 0:dh]
            nfin_ref[s:s + 1, :] = cma_scr[s, :, dh:].T[0:1, :]
        mfin_ref[...] = m_scr[...]
        return

    ii = lax.broadcasted_iota(jnp.int32, (c, c), 0)
    jj = lax.broadcasted_iota(jnp.int32, (c, c), 1)
    gl_lane = lax.broadcasted_iota(jnp.int32, (c, LANES), 1)
    is_fg = (gl_lane >= 2 * h) & (gl_lane < 4 * h)
    ones_col = (lax.broadcasted_iota(jnp.int32, (c, dh), 1) == 0).astype(F32)

    for d, (qkv_ref, g_ref, o_ref) in enumerate(((qkvf_ref, gf_ref, hf_ref), (qkvb_ref, gb_ref, hb_ref))):
        causal = (ii >= jj) if d == 0 else (ii <= jj)
        gx = g_ref[...] + bias_ref[...]
        gl = jnp.where(is_fg, -_softplus(-gx), gx)
        bcum = _dot_exact_lhs(causal.astype(BF16), gl)
        glt = gl.T
        brow = _dot_exact_rhs(glt, (~causal).astype(BF16) + (ii == jj).astype(BF16))
        qkv = qkv_ref[...]
        kt_all = (qkv[:, h * dh:2 * h * dh] * (dh ** -0.5)).T
        end = c - 1 if d == 0 else 0
        outs = []
        for hh in range(h):
            ig_c = d * h + hh
            fg_c = 2 * h + d * h + hh
            sr = d * h + hh
            q = qkv[:, hh * dh:(hh + 1) * dh]
            k = qkv[:, (h + hh) * dh:(h + hh + 1) * dh] * (dh ** -0.5)
            v_aug = jnp.concatenate([qkv[:, (2 * h + hh) * dh:(2 * h + hh + 1) * dh], ones_col], axis=-1)
            kt = kt_all[hh * dh:(hh + 1) * dh, :]
            b_col = bcum[:, fg_c:fg_c + 1]
            b_row = brow[fg_c:fg_c + 1, :]
            c_row = glt[ig_c:ig_c + 1, :] - b_row
            m = m_scr[sr:sr + 1, 0:1]
            cma = cma_scr[sr]
            dmat = jnp.where(causal, c_row, -jnp.inf)
            g_col = jnp.maximum(m, jnp.max(dmat, axis=-1, keepdims=True))
            s = _bdot_nt(q, k) * jnp.exp(dmat - g_col)
            nd = _bdot(s, v_aug) + jnp.exp(m - g_col) * _bdot(q, cma)
            den = nd[:, dh:dh + 1]
            outs.append(nd[:, 0:dh] / jnp.maximum(jnp.abs(den), jnp.exp(-(b_col + g_col))))
            bl = b_row[:, end:end + 1]
            wl_row = bl + c_row
            m_new = jnp.maximum(bl + m, jnp.max(wl_row, axis=-1, keepdims=True))
            cma_scr[sr] = jnp.exp(bl + m - m_new) * cma + _bdot(kt * jnp.exp(wl_row - m_new), v_aug)
            m_scr[sr:sr + 1, :] = jnp.broadcast_to(m_new, (1, LANES))
        o_ref[...] = jnp.concatenate(outs, axis=-1)


def mlstm_scan(qkv, gates, bias_row, c0, n0, m0, h, dh, emit_state, nr=4):
    b, l, _ = qkv.shape
    c = CHUNK
    nc = l // c
    has_state = c0 is not None
    w = qkv.shape[-1]
    fwd = lambda ww: pl.BlockSpec((nr, c, ww), lambda bb, ci: (bb, ci, 0))
    bwd = lambda ww: pl.BlockSpec((nr, c, ww), lambda bb, ci: (bb, nc - 1 - ci, 0))
    cspec = pl.BlockSpec((nr, 2, h, dh, dh), lambda bb, ci: (bb, 0, 0, 0, 0))
    rspec = pl.BlockSpec((nr, 2 * h, LANES), lambda bb, ci: (bb, 0, 0))
    args = [qkv, gates, qkv, gates, bias_row]
    specs = [fwd(w), fwd(LANES), bwd(w), bwd(LANES), _full_spec(bias_row)]
    if has_state:
        args += [c0, n0.reshape(b, 2 * h, dh), jnp.broadcast_to(m0.reshape(b, 2 * h, 1), (b, 2 * h, LANES))]
        specs += [cspec, rspec, rspec]
    out_shape = [jax.ShapeDtypeStruct((b, l, h * dh), F32)] * 2
    out_specs = [fwd(h * dh), bwd(h * dh)]
    if emit_state:
        out_shape += [jax.ShapeDtypeStruct((b, 2, h, dh, dh), F32),
                      jax.ShapeDtypeStruct((b, 2 * h, dh), F32), jax.ShapeDtypeStruct((b, 2 * h, LANES), F32)]
        out_specs += [cspec, rspec, rspec]
    return pl.pallas_call(
        functools.partial(_mlstm_kernel, nr=nr, nc=nc, emit_state=emit_state, h=h, dh=dh, c=c,
                          has_state=has_state),
        grid=(b // nr, nc),
        in_specs=specs,
        out_specs=out_specs,
        out_shape=out_shape,
        scratch_shapes=[pltpu.VMEM((nr, 2 * h, dh, 2 * dh), F32), pltpu.VMEM((nr, 2 * h, LANES), F32)],
        compiler_params=_cparams(2),
        name="mlstm_scan",
    )(*args)


def _attn_prep_kernel(*refs, hq, hkv, hd, latent):
    it = iter(refs)
    q_ref, kv_ref, gq_ref, gk_ref, segq_ref, segk_ref = (next(it) for _ in range(6))
    if latent:
        cq_ref, sq_ref, ck_ref, sk_ref = (next(it) for _ in range(4))
    qo_ref, ko_ref, vo_ref = next(it), next(it), next(it)
    q = q_ref[...]
    kv = kv_ref[...]
    kw = hkv * hd
    k = kv[:, :kw]
    v = kv[:, kw:]
    qn = q * lax.rsqrt(_dot_exact_rhs(q * q, segq_ref[...]) * (1.0 / hd) + RMS_EPS) * gq_ref[...]
    kn = k * lax.rsqrt(_dot_exact_rhs(k * k, segk_ref[...]) * (1.0 / hd) + RMS_EPS) * gk_ref[...]
    if latent:
        qn = _rope(qn, cq_ref[...], sq_ref[...], hd // 4)
        kn = _rope(kn, ck_ref[...], sk_ref[...], hd // 4)
    qo_ref[...] = qn * (hd ** -0.5 * LOG2E)
    for j in range(hkv):
        ko_ref[j] = kn[:, j * hd:(j + 1) * hd]
        vo_ref[j] = v[:, j * hd:(j + 1) * hd]


def attn_prep(q, kv, qk_gain, hq, hkv, hd, latent, tm):
    b, l, wq = q.shape
    wkv = kv.shape[-1]
    kw = hkv * hd
    params = [jnp.tile(qk_gain[0], hq).reshape(1, wq), jnp.tile(qk_gain[1], hkv).reshape(1, kw),
              _seg_ones(wq, hd), _seg_ones(kw, hd)]
    args = [q, kv] + params
    specs = [_tok_spec(tm, wq), _tok_spec(tm, wkv)] + [_full_spec(a) for a in params]
    if latent:
        cq, sq = _rope_tables(l, hd, hq)
        ck, sk = _rope_tables(l, hd, hkv)
        args += [cq, sq, ck, sk]
        specs += [pl.BlockSpec((tm, wq), lambda bb, i: (i, 0))] * 2 + [pl.BlockSpec((tm, kw), lambda bb, i: (i, 0))] * 2
    kvspec = pl.BlockSpec((None, hkv, tm, hd), lambda bb, i: (bb, 0, i, 0))
    return pl.pallas_call(
        functools.partial(_attn_prep_kernel, hq=hq, hkv=hkv, hd=hd, latent=latent),
        grid=(b, l // tm),
        in_specs=specs,
        out_specs=[_tok_spec(tm, wq), kvspec, kvspec],
        out_shape=[jax.ShapeDtypeStruct((b, l, wq), F32)] + [jax.ShapeDtypeStruct((b, hkv, l, hd), F32)] * 2,
        compiler_params=_cparams(2),
        name="attn_prep",
    )(*args)


def _attn_kernel(*refs, g, hd, tq, nsub, kb, has_ctx):
    it = iter(refs)
    q_ref, k_ref, v_ref = next(it), next(it), next(it)
    if has_ctx:
        ck_ref, cv_ref = next(it), next(it)
    o_ref = next(it)
    s_scr = next(it)
    sources =([(ck_ref, cv_ref)] if has_ctx else []) + [(k_ref, v_ref)]
    maxes = []
    for u in range(nsub):
        q = q_ref[u * tq:(u + 1) * tq, :]
        qs = jnp.concatenate([q[:, i * hd:(i + 1) * hd] for i in range(g)], axis=0).astype(BF16)
        m = None
        row = 0
        for kr, _ in sources:
            for j in range(kr.shape[0] // kb):
                st = _bdot_nt(kr[j * kb:(j + 1) * kb, :], qs)
                s_scr[u, row:row + kb, :] = st
                bm = jnp.max(st, axis=0, keepdims=True)
                m = bm if m is None else jnp.maximum(m, bm)
                row += kb
        maxes.append(m)
    for u in range(nsub):
        acc = jnp.zeros((hd, g * tq), F32)
        den = jnp.zeros((1, g * tq), F32)
        row = 0
        for _, vr in sources:
            for j in range(vr.shape[0] // kb):
                p = jnp.exp2(s_scr[u, row:row + kb, :] - maxes[u])
                den = den + jnp.sum(p, axis=0, keepdims=True)
                acc = acc + _bdot_tn(vr[j * kb:(j + 1) * kb, :], p)
                row += kb
        ot = acc / den
        o = jnp.concatenate([ot, jnp.zeros((LANES - hd, g * tq), F32)], axis=0).T
        for i in range(g):
            o_ref[u * tq:(u + 1) * tq, i * hd:(i + 1) * hd] = o[i * tq:(i + 1) * tq, 0:hd]


def attention(q, k, v, ctx_k, ctx_v, layer_o, hq, hkv, hd, tq, nsub):
    b, l, wq = q.shape
    g = hq // hkv
    has_ctx = ctx_k is not None
    qspec = pl.BlockSpec((None, nsub * tq, g * hd), lambda bb, j, i: (bb, i, j))
    kvspec = pl.BlockSpec((None, None, l, hd), lambda bb, j, i: (bb, j, 0, 0))
    args = [q, k, v]
    specs = [qspec, kvspec, kvspec]
    n_keys = l
    if has_ctx:
        s = ctx_k.shape[3]
        n_keys += s
        cspec = pl.BlockSpec((None, None, None, s, hd), lambda bb, j, i: (bb, layer_o, j, 0, 0))
        args += [ctx_k, ctx_v]
        specs += [cspec, cspec]
    kb = min(256, l)
    return pl.pallas_call(
        functools.partial(_attn_kernel, g=g, hd=hd, tq=tq, nsub=nsub, kb=kb, has_ctx=has_ctx),
        grid=(b, hkv, l // (nsub * tq)),
        in_specs=specs,
        out_specs=qspec,
        out_shape=jax.ShapeDtypeStruct((b, l, wq), F32),
        scratch_shapes=[pltpu.VMEM((nsub, n_keys, g * tq), F32)],
        compiler_params=_cparams(3),
        name="gqa_attention",
    )(*args)


DENSE_TILE = 512
PREP_TILE = 512
QUERY_TILE = 128
QUERY_SUBTILES = 2


def _tile_plan(seq_len):
    return dict(tm=DENSE_TILE, tp=min(seq_len, PREP_TILE), tq=QUERY_TILE, nsub=QUERY_SUBTILES)


def _pad_rows(w, lo, total):
    return jnp.zeros((total, w.shape[-1]), w.dtype).at[lo:lo + w.shape[0]].set(w)


def kernel(x_prompt, x_sample, state_ret, state_rwkv, state_mlstm_c, state_mlstm_n, state_mlstm_m, cache_k, cache_v, c, c_ctx, ada_w, ada_b, norm_g, ffn_w1, ffn_w2, w_in_even, w_out_even, ret_log_decay, ret_gn_w, rwkv_mu, rwkv_w0, rwkv_w2, rwkv_a0, rwkv_a2, rwkv_g2, rwkv_k_k, rwkv_k_a, rwkv_r_k, rwkv_ln_w, rwkv_ln_b, w_in_odd, w_out_odd, mlstm_i_bias, mlstm_f_bias, mlstm_norm_w, attn_qk_norm, final_norm):
    depth = ada_w.shape[0]
    d_model = x_prompt.shape[-1]
    h_a, dk_a, dv_a = state_ret.shape[3:]
    h_b, hs_b = state_rwkv.shape[3:5]
    h_c, dh_c = state_mlstm_c.shape[3:5]
    hkv_d, hd_d = cache_k.shape[2], cache_k.shape[4]
    wa, wb, wc = h_a * dv_a, h_b * hs_b, h_c * dh_c
    wd = w_out_odd.shape[1] - wc
    hq_d = wd // hd_d
    kvw = hkv_d * hd_d
    n_dec = c.shape[0]
    lora_w, lora_a, lora_g = rwkv_w2.shape[2], rwkv_a2.shape[2], rwkv_g2.shape[1]
    lora_tot = lora_w + lora_a + lora_g

    rows = -(-(n_dec + 1) // SUBLANES) * SUBLANES
    cond = jnp.zeros((rows, d_model), F32).at[:n_dec].set(c).at[n_dec].set(c_ctx)
    mod = modulation_all(cond, ada_w, ada_b).reshape(depth, rows, N_MOD, d_model)

    seg_b = _seg_ones(wb, hs_b)
    streams = {
        "prompt": dict(x=x_prompt, latent=False, **_tile_plan(x_prompt.shape[1])),
        "sample": dict(x=x_sample, latent=True, **_tile_plan(x_sample.shape[1])),
    }
    new_states = {}

    for l in range(depth):
        w1a, w1b = ffn_w1[l, 0].astype(BF16), ffn_w1[l, 1].astype(BF16)
        w2a, w2b = ffn_w2[l, 0].astype(BF16), ffn_w2[l, 1].astype(BF16)
        fin = final_norm if l == depth - 1 else None
        if l % 2 == 0:
            e = l // 2
            w_in = w_in_even[e].astype(BF16)
            w_out = w_out_even[e].astype(BF16)
            splits = (2 * h_a * dk_a, wa, wa, 3 * wb + lora_tot)
            w2p = jnp.stack([_pad_rows(rwkv_w2[e, d], 0, lora_tot) for d in range(2)]).astype(BF16)
            a2p = jnp.stack([_pad_rows(rwkv_a2[e, d], lora_w, lora_tot) for d in range(2)]).astype(BF16)
            g2p = _pad_rows(rwkv_g2[e], lora_w + lora_a, lora_tot).astype(BF16)
        else:
            o = l // 2
            wi = w_in_odd[o]
            g0 = 3 * wc
            p_c = g0 + 4 * h_c + wc
            gates_w = jnp.zeros((d_model, LANES), F32).at[:, :4 * h_c].set(wi[:, g0:g0 + 4 * h_c])
            w_in = jnp.concatenate([wi[:, :g0], wi[:, g0 + 4 * h_c:p_c], wi[:, p_c:p_c + wd],
                                    wi[:, p_c + wd:], gates_w], axis=1).astype(BF16)
            w_out = w_out_odd[o].astype(BF16)
            splits = (3 * wc, wc, wd, 2 * kvw, LANES)
            bias_row = jnp.zeros((1, LANES), F32).at[0, :2 * h_c].set(mlstm_i_bias[o].reshape(-1))
            bias_row = bias_row.at[0, 2 * h_c:4 * h_c].set(mlstm_f_bias[o].reshape(-1))

        for name, st in streams.items():
            x, latent, tm = st["x"], st["latent"], st["tm"]
            b, seq, _ = x.shape
            m = mod[l, :n_dec] if latent else mod[l, n_dec:n_dec + 1]
            emit = not latent
            x1, parts = dense_pre(x, m, norm_g[l, 0], norm_g[l, 1], w1a, w2a, w_in, splits, tm)
            if l % 2 == 0:
                qk_a, v_a, g_a, p_b = parts
                ret = retention_scan(qk_a, v_a, ret_log_decay[e], state_ret[:, e] if latent else None,
                                     h_a, dk_a, dv_a, latent, emit)
                stack, g_b, bonus = rwkv_prep(
                    p_b, rwkv_mu[e], rwkv_w0[e], w2p, rwkv_a0[e], a2p, g2p, rwkv_k_k[e], rwkv_k_a[e],
                    rwkv_r_k[e].reshape(-1), seg_b, wb, st["tp"])
                _, mixed = _chain_groups(b, h_b)
                scan = rwkv_scan(to_chains(stack, h_b),
                                 _state_to_chains(state_rwkv[:, e]) if latent else None, hs_b, mixed, emit)
                y = from_chains(scan[0], scan[1], b, h_b, hs_b)
                if emit:
                    new_states.setdefault("ret", []).append(ret[2])
                    new_states.setdefault("rwkv", []).append(_state_from_chains(scan[2], b, h_b))
                x = _dense_post(_post_even_kernel, "dense_post_even", x1,
                                [ret[0], ret[1], g_a, y, bonus, g_b], m,
                                [_row(ret_gn_w[e]), _row(rwkv_ln_w[e]), _row(rwkv_ln_b[e]), seg_b,
                                 _row(norm_g[l, 2]), w_out, w1b, w2b],
                                fin, tm, d_ff=w2b.shape[0], dv=dv_a, hs=hs_b)
            else:
                qkv_c, og, q_d, kv_d, gates = parts
                ml = mlstm_scan(qkv_c, gates, bias_row,
                                state_mlstm_c[:, o] if latent else None,
                                state_mlstm_n[:, o] if latent else None,
                                state_mlstm_m[:, o] if latent else None, h_c, dh_c, emit)
                qn, kn, vn = attn_prep(q_d, kv_d, attn_qk_norm[o], hq_d, hkv_d, hd_d, latent, st["tp"])
                out_d = attention(qn, kn, vn, cache_k if latent else None, cache_v if latent else None, o,
                                  hq_d, hkv_d, hd_d, st["tq"], st["nsub"])
                if emit:
                    new_states.setdefault("mc", []).append(ml[2])
                    new_states.setdefault("mn", []).append(ml[3].reshape(b, 2, h_c, dh_c))
                    new_states.setdefault("mm", []).append(ml[4][:, :, 0].reshape(b, 2, h_c))
                    new_states.setdefault("k", []).append(kn)
                    new_states.setdefault("v", []).append(vn)
                x = _dense_post(_post_odd_kernel, "dense_post_odd", x1, [ml[0], ml[1], og, out_d], m,
                                [_row(mlstm_norm_w[o]), _row(norm_g[l, 2]), w_out, w1b, w2b],
                                fin, tm, d_ff=w2b.shape[0], dh=dh_c)
            st["x"] = x

    stack = lambda key: jnp.stack(new_states[key], axis=1)
    return (streams["prompt"]["x"], streams["sample"]["x"], stack("ret"), stack("rwkv"), stack("mc"),
            stack("mn"), stack("mm"), stack("k"), stack("v"))
```

```python
import functools

import jax
import jax.numpy as jnp
from jax import lax
from jax.experimental import pallas as pl
from jax.experimental.pallas import tpu as pltpu

F32 = jnp.float32
BF16 = jnp.bfloat16

GRID_W = 64
CHUNK = 128
ROPE_BASE = 10000.0
RMS_EPS = 1e-6
LOG2E = 1.4426950408889634
N_MOD = 9
LANES = 128
SUBLANES = 8
VMEM_LIMIT = 56 * 1024 * 1024


def _cparams(n_axes):
    return pltpu.CompilerParams(dimension_semantics=("arbitrary",) * n_axes, vmem_limit_bytes=VMEM_LIMIT)


def _bdot(a, b):
    return jnp.dot(a.astype(BF16), b.astype(BF16), preferred_element_type=F32)


def _bdot_nt(a, b):
    return lax.dot_general(a.astype(BF16), b.astype(BF16), (((1,), (1,)), ((), ())),
                           preferred_element_type=F32)


def _bdot_tn(a, b):
    return lax.dot_general(a.astype(BF16), b.astype(BF16), (((0,), (0,)), ((), ())),
                           preferred_element_type=F32)


def _split3(x):
    hi = x.astype(BF16)
    r1 = x - hi.astype(F32)
    mid = r1.astype(BF16)
    lo = (r1 - mid.astype(F32)).astype(BF16)
    return hi, mid, lo


def _dot_exact_rhs(x, e, pieces=3):
    e = e.astype(BF16)
    hi, mid, lo = _split3(x)
    out = jnp.dot(hi, e, preferred_element_type=F32) + jnp.dot(mid, e, preferred_element_type=F32)
    if pieces == 3:
        out = out + jnp.dot(lo, e, preferred_element_type=F32)
    return out


def _dot_exact_lhs(e, x):
    e = e.astype(BF16)
    hi, mid, lo = _split3(x)
    return (jnp.dot(e, hi, preferred_element_type=F32) + jnp.dot(e, mid, preferred_element_type=F32)
            + jnp.dot(e, lo, preferred_element_type=F32))


def _sigmoid(x):
    return 1.0 / (1.0 + jnp.exp(-x))


def _silu(x):
    return x * _sigmoid(x)


def _softplus(x):
    return jnp.maximum(x, 0.0) + jnp.log(1.0 + jnp.exp(-jnp.abs(x)))


def _rms(x, g):
    return x * lax.rsqrt(jnp.mean(x * x, axis=-1, keepdims=True) + RMS_EPS) * g


def _seg_ones(n, seg):
    r = lax.broadcasted_iota(jnp.int32, (n, n), 0) // seg
    c = lax.broadcasted_iota(jnp.int32, (n, n), 1) // seg
    return (r == c).astype(BF16)


def _full_spec(arr):
    nd = arr.ndim
    return pl.BlockSpec(arr.shape, lambda *_: (0,) * nd, pipeline_mode=pl.Buffered(1))


def _tok_spec(tm, w):
    return pl.BlockSpec((None, tm, w), lambda b, i: (b, i, 0))


def _row(a):
    return a.reshape(1, -1)


def _mod_kernel(c_ref, w_ref, b_ref, o_ref):
    o_ref[...] = _bdot(_silu(c_ref[...]), w_ref[...]) + b_ref[...]


def modulation_all(cond, ada_w, ada_b, tn=1024):
    depth, d, n = ada_w.shape
    rows = cond.shape[0]
    return pl.pallas_call(
        _mod_kernel,
        grid=(depth, n // tn),
        in_specs=[pl.BlockSpec((rows, d), lambda l, j: (0, 0)),
                  pl.BlockSpec((None, d, tn), lambda l, j: (l, 0, j)),
                  pl.BlockSpec((None, 1, tn), lambda l, j: (l, 0, j))],
        out_specs=pl.BlockSpec((None, rows, tn), lambda l, j: (l, 0, j)),
        out_shape=jax.ShapeDtypeStruct((depth, rows, n), F32),
        compiler_params=_cparams(2),
        name="adaln_modulation",
    )(cond, ada_w, ada_b.reshape(depth, 1, n))


def _ff_chunk(d_ff):
    return 256 if d_ff % 256 == 0 else d_ff


def _swiglu(h_bf, w1_ref, w2_ref, d_ff):
    fc = _ff_chunk(d_ff)
    acc = None
    for c in range(d_ff // fc):
        gate = jnp.dot(h_bf, w1_ref[:, c * fc:(c + 1) * fc], preferred_element_type=F32)
        up = jnp.dot(h_bf, w1_ref[:, d_ff + c * fc:d_ff + (c + 1) * fc], preferred_element_type=F32)
        a = (_silu(gate) * up).astype(BF16)
        part = jnp.dot(a, w2_ref[c * fc:(c + 1) * fc, :], preferred_element_type=F32)
        acc = part if acc is None else acc + part
    return acc


def _fold_rows(x, mod):
    if mod.shape[0] == 1:
        return x.reshape(1, -1, x.shape[-1])
    return x


def _mod_specs(mod, ks):
    bc, _, d = mod.shape
    mod4 = mod.reshape(bc, N_MOD, 1, d)

    def spec(k):
        if bc == 1:
            return pl.BlockSpec((None, None, 1, d), lambda b, i: (0, k, 0, 0))
        return pl.BlockSpec((None, None, 1, d), lambda b, i: (b, k, 0, 0))

    return [mod4] * len(ks), [spec(k) for k in ks]


def _pre_kernel(x_ref, sh1_ref, sc1_ref, gt1_ref, sh2_ref, sc2_ref, g1_ref, g2_ref,
                w1_ref, w2_ref, win_ref, x_out_ref, *p_refs, d_ff, splits):
    x = x_ref[...]
    h = _rms(x, g1_ref[...]) * (1.0 + sc1_ref[...]) + sh1_ref[...]
    x1 = x + 0.5 * gt1_ref[...] * _swiglu(h.astype(BF16), w1_ref, w2_ref, d_ff)
    x_out_ref[...] = x1
    h2 = (_rms(x1, g2_ref[...]) * (1.0 + sc2_ref[...]) + sh2_ref[...]).astype(BF16)
    off = 0
    for ref, wdt in zip(p_refs, splits):
        ref[...] = jnp.dot(h2, win_ref[:, off:off + wdt], preferred_element_type=F32)
        off += wdt


def dense_pre(x, mod, g1, g2, w1, w2, w_in, splits, tm):
    shape = x.shape
    x = _fold_rows(x, mod)
    b, l, d = x.shape
    tm = min(l, tm)
    margs, mspecs = _mod_specs(mod, (0, 1, 2, 3, 4))
    params = [_row(g1), _row(g2), w1, w2, w_in]
    outs = pl.pallas_call(
        functools.partial(_pre_kernel, d_ff=w2.shape[0], splits=tuple(splits)),
        grid=(b, l // tm),
        in_specs=[_tok_spec(tm, d)] + mspecs + [_full_spec(a) for a in params],
        out_specs=[_tok_spec(tm, d)] + [_tok_spec(tm, w) for w in splits],
        out_shape=[jax.ShapeDtypeStruct((b, l, d), F32)] + [jax.ShapeDtypeStruct((b, l, w), F32) for w in splits],
        compiler_params=_cparams(2),
        name="dense_pre",
    )(x, *margs, *params)
    outs = [o.reshape(shape[:-1] + o.shape[-1:]) for o in outs]
    return outs[0], outs[1:]


def _head_ln(x, width, eps):
    parts = []
    for h in range(x.shape[-1] // width):
        xh = x[:, h * width:(h + 1) * width]
        mu = jnp.mean(xh, axis=-1, keepdims=True)
        xc = xh - mu
        var = jnp.mean(xc * xc, axis=-1, keepdims=True)
        parts.append(xc * lax.rsqrt(var + eps))
    return jnp.concatenate(parts, axis=-1)


def _seg_ln(x, e, width, eps):
    mu = _dot_exact_rhs(x, e, pieces=2) * (1.0 / width)
    xc = x - mu
    var = _dot_exact_rhs(xc * xc, e, pieces=2) * (1.0 / width)
    return xc * lax.rsqrt(var + eps)


def _post_tail(x, mix, gt2_ref, sh3_ref, sc3_ref, gt3_ref, g3_ref, w1_ref, w2_ref, fg_ref, o_ref, d_ff):
    x2 = x + gt2_ref[...] * mix
    h3 = (_rms(x2, g3_ref[...]) * (1.0 + sc3_ref[...]) + sh3_ref[...]).astype(BF16)
    y = x2 + 0.5 * gt3_ref[...] * _swiglu(h3, w1_ref, w2_ref, d_ff)
    if fg_ref is not None:
        y = _rms(y, fg_ref[...])
    o_ref[...] = y


def _post_even_kernel(x_ref, of_ref, ob_ref, ga_ref, y_ref, bon_ref, gb_ref,
                      gt2_ref, sh3_ref, sc3_ref, gt3_ref,
                      gnw_ref, lnw_ref, lnb_ref, seg_ref, g3_ref, wout_ref, w1_ref, w2_ref,
                      *rest, d_ff, dv, hs, final):
    fg_ref = rest[0] if final else None
    o_ref = rest[-1]
    wa = of_ref.shape[-1]
    o = _head_ln(of_ref[...] + ob_ref[...], dv, 1e-5) * gnw_ref[...]
    out_a = _silu(ga_ref[...]) * o
    y = _seg_ln(y_ref[...], seg_ref[...], hs, 64e-5) * lnw_ref[...] + lnb_ref[...]
    out_b = (y + bon_ref[...]) * gb_ref[...]
    mix = _bdot(out_a, wout_ref[0:wa, :]) + _bdot(out_b, wout_ref[wa:, :])
    _post_tail(x_ref[...], mix, gt2_ref, sh3_ref, sc3_ref, gt3_ref, g3_ref, w1_ref, w2_ref, fg_ref, o_ref, d_ff)


def _post_odd_kernel(x_ref, hf_ref, hb_ref, og_ref, od_ref,
                     gt2_ref, sh3_ref, sc3_ref, gt3_ref,
                     nw_ref, g3_ref, wout_ref, w1_ref, w2_ref, *rest, d_ff, dh, final):
    fg_ref = rest[0] if final else None
    o_ref = rest[-1]
    wc = hf_ref.shape[-1]
    h = _head_ln(hf_ref[...] + hb_ref[...], dh, 1e-5) * nw_ref[...]
    out_c = _sigmoid(og_ref[...]) * h
    mix = _bdot(out_c, wout_ref[0:wc, :]) + _bdot(od_ref[...], wout_ref[wc:, :])
    _post_tail(x_ref[...], mix, gt2_ref, sh3_ref, sc3_ref, gt3_ref, g3_ref, w1_ref, w2_ref, fg_ref, o_ref, d_ff)


def _dense_post(kernel_fn, name, x, toks, mod, params, final_g, tm, **kw):
    shape = x.shape
    x = _fold_rows(x, mod)
    toks = [_fold_rows(t, mod) for t in toks]
    b, l, d = x.shape
    tm = min(l, tm)
    margs, mspecs = _mod_specs(mod, (5, 6, 7, 8))
    final = final_g is not None
    params = list(params) + ([_row(final_g)] if final else [])
    return pl.pallas_call(
        functools.partial(kernel_fn, final=final, **kw),
        grid=(b, l // tm),
        in_specs=([_tok_spec(tm, d)] + [_tok_spec(tm, a.shape[-1]) for a in toks] + mspecs
                  + [_full_spec(a) for a in params]),
        out_specs=_tok_spec(tm, d),
        out_shape=jax.ShapeDtypeStruct((b, l, d), F32),
        compiler_params=_cparams(2),
        name=name,
    )(x, *toks, *margs, *params).reshape(shape)


def _rope_tables(n, d, reps):
    rows = n // GRID_W
    row = jnp.repeat(jnp.arange(rows), GRID_W).astype(F32)
    col = (jnp.arange(rows * GRID_W) % GRID_W).astype(F32)
    nf = d // 4
    inv = ROPE_BASE ** (-jnp.arange(nf, dtype=F32) / nf)
    ang_r = row[:, None] * inv[None, :]
    ang_c = col[:, None] * inv[None, :]
    cos = jnp.concatenate([jnp.cos(ang_r), jnp.cos(ang_r), jnp.cos(ang_c), jnp.cos(ang_c)], axis=-1)
    sin = jnp.concatenate([-jnp.sin(ang_r), jnp.sin(ang_r), -jnp.sin(ang_c), jnp.sin(ang_c)], axis=-1)
    return jnp.tile(cos, (1, reps)), jnp.tile(sin, (1, reps))


def _rope(x, cos, sin, nf):
    w = x.shape[-1]
    lane = lax.broadcasted_iota(jnp.int32, x.shape, 1)
    first = (lane % (2 * nf)) < nf
    partner = jnp.where(first, pltpu.roll(x, w - nf, axis=1), pltpu.roll(x, nf, axis=1))
    return x * cos + partner * sin


def _ret_kernel(ld_ref, *refs, nr, h, dk, dv, c, nc, latent, emit_state):
    it = iter(refs)
    qkf_ref, vf_ref, qkb_ref, vb_ref = next(it), next(it), next(it), next(it)
    if latent:
        cosf_ref, sinf_ref, cosb_ref, sinb_ref, s0_ref = next(it), next(it), next(it), next(it), next(it)
    of_ref, ob_ref = next(it), next(it)
    sfin_ref = next(it) if emit_state else None
    z_scr = next(it)

    ci = pl.program_id(1)
    hk = h * dk

    @pl.when(ci == 0)
    def _():
        z_scr[...] = jnp.zeros_like(z_scr)
        if latent:
            for r in range(nr):
                for d in range(2):
                    for hh in range(h):
                        z_scr[r, d, hh, hh * dk:(hh + 1) * dk, :] = s0_ref[r, d, hh]

    ii = lax.broadcasted_iota(jnp.int32, (c, c), 0)
    jj = lax.broadcasted_iota(jnp.int32, (c, c), 1)
    icol = lax.broadcasted_iota(jnp.int32, (c, 1), 0).astype(F32)
    lane = lax.broadcasted_iota(jnp.int32, (1, hk), 1)

    for d, (qk_ref, v_ref, o_ref) in enumerate(((qkf_ref, vf_ref, of_ref), (qkb_ref, vb_ref, ob_ref))):
        diff = (ii - jj) if d == 0 else (jj - ii)
        causal = diff >= 0
        dpos = jnp.where(causal, diff, 0).astype(F32)
        qe = (icol + 1.0) if d == 0 else (c - icol)
        ke = (c - 1.0 - icol) if d == 0 else icol
        tables = []
        for hh in range(h):
            lg = ld_ref[d, hh]
            tables.append((jnp.where(causal, jnp.exp(lg * dpos), 0.0), jnp.exp(lg * qe), jnp.exp(lg * ke),
                           jnp.exp(lg * jnp.full((1, 1), c, F32))))
        for r in range(nr):
            qk = qk_ref[r]
            if latent:
                cos_ref, sin_ref = (cosf_ref, sinf_ref) if d == 0 else (cosb_ref, sinb_ref)
                qk = _rope(qk, cos_ref[...], sin_ref[...], dk // 4)
            q = qk[:, :hk]
            k = qk[:, hk:] * (dk ** -0.5)
            v = v_ref[r]
            outs = []
            for hh in range(h):
                d_intra, q_dec, k_dec, c_dec = tables[hh]
                msk = (lane // dk) == hh
                qh = jnp.where(msk, q, 0.0)
                kh = jnp.where(msk, k, 0.0)
                vh = v[:, hh * dv:(hh + 1) * dv]
                att = _bdot_nt(qh, kh) * d_intra
                z = z_scr[r, d, hh]
                outs.append(_bdot(att, vh) + _bdot(qh * q_dec, z))
                z_scr[r, d, hh] = z * c_dec + _bdot_tn(kh * k_dec, vh)
            o_ref[r] = jnp.concatenate(outs, axis=-1)

    if emit_state:
        @pl.when(ci == nc - 1)
        def _():
            for r in range(nr):
                for d in range(2):
                    for hh in range(h):
                        sfin_ref[r, d, hh] = z_scr[r, d, hh, hh * dk:(hh + 1) * dk, :]


def retention_scan(qk, v, log_decay, s0, h, dk, dv, latent, emit_state, nr=8):
    b, l, _ = qk.shape
    c = CHUNK
    nc = l // c
    hk = h * dk
    fwd = lambda w: pl.BlockSpec((nr, c, w), lambda bb, ci: (bb, ci, 0))
    bwd = lambda w: pl.BlockSpec((nr, c, w), lambda bb, ci: (bb, nc - 1 - ci, 0))
    args = [log_decay, qk, v, qk, v]
    specs = [pl.BlockSpec(memory_space=pltpu.SMEM), fwd(2 * hk), fwd(h * dv), bwd(2 * hk), bwd(h * dv)]
    if latent:
        cos, sin = _rope_tables(l, dk, 2 * h)
        args += [cos, sin, cos, sin, s0]
        tf = pl.BlockSpec((c, 2 * hk), lambda bb, ci: (ci, 0))
        tb = pl.BlockSpec((c, 2 * hk), lambda bb, ci: (nc - 1 - ci, 0))
        specs += [tf, tf, tb, tb, pl.BlockSpec((nr, 2, h, dk, dv), lambda bb, ci: (bb, 0, 0, 0, 0))]
    out_shape = [jax.ShapeDtypeStruct((b, l, h * dv), F32)] * 2
    out_specs = [fwd(h * dv), bwd(h * dv)]
    if emit_state:
        out_shape.append(jax.ShapeDtypeStruct((b, 2, h, dk, dv), F32))
        out_specs.append(pl.BlockSpec((nr, 2, h, dk, dv), lambda bb, ci: (bb, 0, 0, 0, 0)))
    return pl.pallas_call(
        functools.partial(_ret_kernel, nr=nr, h=h, dk=dk, dv=dv, c=c, nc=nc, latent=latent,
                          emit_state=emit_state),
        grid=(b // nr, nc),
        in_specs=specs,
        out_specs=out_specs,
        out_shape=out_shape,
        scratch_shapes=[pltpu.VMEM((nr, 2, h, hk, dv), F32)],
        compiler_params=_cparams(2),
        name="retention_scan",
    )(*args)


def _rwkv_prep_kernel(p_ref, prev_ref, next_ref, mu_ref, w0_ref, a0_ref, w2_ref, a2_ref, g2_ref,
                      kk_ref, ka_ref, rk_ref, seg_ref, s_out, g_out, bon_out, *, tm, nt, wb):
    i = pl.program_id(1)
    p = p_ref[...]
    row = lax.broadcasted_iota(jnp.int32, p.shape, 0)
    prow = jnp.where(i == 0, 0.0, prev_ref[SUBLANES - 1:SUBLANES, :])
    nrow = jnp.where(i == nt - 1, 0.0, next_ref[0:1, :])
    prev = jnp.where(row == 0, prow, pltpu.roll(p, 1, axis=0))
    nxt = jnp.where(row == tm - 1, nrow, pltpu.roll(p, tm - 1, axis=0))
    ps = p + mu_ref[...] * (0.5 * (prev + nxt) - p)
    r = ps[:, 0:wb]
    k = ps[:, wb:2 * wb]
    v = ps[:, 2 * wb:3 * wb]
    lora = ps[:, 3 * wb:]
    tl = jnp.tanh(lora)
    seg = seg_ref[...]
    kk = k * kk_ref[...]
    nrm = jnp.sqrt(_dot_exact_rhs(kk * kk, seg, pieces=2))
    kk = kk / jnp.maximum(nrm, 1e-12)
    s_out[0] = r
    s_out[1] = v
    s_out[2] = -kk
    g_out[...] = _bdot(_sigmoid(lora), g2_ref[...])
    bon_out[...] = _dot_exact_rhs(r * k * rk_ref[...], seg) * v
    for d in range(2):
        w_log = -_softplus(-(w0_ref[d:d + 1, :] + _bdot(tl, w2_ref[d]))) - 0.5
        s_out[3 + 3 * d] = jnp.exp(-jnp.exp(w_log))
        a = _sigmoid(a0_ref[d:d + 1, :] + _bdot(lora, a2_ref[d]))
        s_out[4 + 3 * d] = k * (1.0 + (a - 1.0) * ka_ref[...])
        s_out[5 + 3 * d] = kk * a


def rwkv_prep(p, mu, w0, w2p, a0, a2p, g2p, k_k, k_a, r_k, seg, wb, tm):
    b, l, pw = p.shape
    nt = l // tm
    r8 = tm // SUBLANES
    nb8 = l // SUBLANES
    params = [_row(mu), w0, a0, w2p, a2p, g2p, _row(k_k), _row(k_a), _row(r_k), seg]
    return pl.pallas_call(
        functools.partial(_rwkv_prep_kernel, tm=tm, nt=nt, wb=wb),
        grid=(b, nt),
        in_specs=[_tok_spec(tm, pw),
                  pl.BlockSpec((None, SUBLANES, pw), lambda bb, i: (bb, jnp.maximum(i * r8 - 1, 0), 0)),
                  pl.BlockSpec((None, SUBLANES, pw), lambda bb, i: (bb, jnp.minimum((i + 1) * r8, nb8 - 1), 0))]
                 + [_full_spec(a) for a in params],
        out_specs=[pl.BlockSpec((9, None, tm, wb), lambda bb, i: (0, bb, i, 0))] + [_tok_spec(tm, wb)] * 2,
        out_shape=[jax.ShapeDtypeStruct((9, b, l, wb), F32)] + [jax.ShapeDtypeStruct((b, l, wb), F32)] * 2,
        compiler_params=_cparams(2),
        name="rwkv_prep",
    )(p, p, p, *params)


ROW_PITCH = 72
CHAIN_TILE = 128
N_CHAIN_ARRAYS = 6


def _chain_groups(b, h):
    per_dir = b * h
    if 2 * per_dir == LANES:
        return 1, True
    assert per_dir % LANES == 0
    return 2 * per_dir // LANES, False


def _to_chains_kernel(lo_ref, hi_ref, o_ref, t_scr, *, n, h, nb, tt):
    for half, ref in enumerate((lo_ref, hi_ref)):
        for bb in range(nb):
            xt = ref[bb].T
            for hh in range(h):
                t_scr[half * nb + bb, hh * ROW_PITCH:hh * ROW_PITCH + n, :] = xt[hh * n:(hh + 1) * n]
    for c in range(n):
        q = jnp.concatenate([t_scr[k, pl.ds(c, h, stride=ROW_PITCH), :] for k in range(2 * nb)], axis=0)
        o_ref[pl.ds(c, tt, stride=ROW_PITCH), :] = q.T
    pad = jnp.zeros((ROW_PITCH - n, LANES), F32)
    for t in range(tt):
        o_ref[t * ROW_PITCH + n:(t + 1) * ROW_PITCH, :] = pad


def to_chains(stack, h):
    _, b, l, w = stack.shape
    n = w // h
    nb = LANES // (2 * h)
    tt = CHAIN_TILE
    ng, mixed = _chain_groups(b, h)
    src1 = lambda a: jnp.where(a < 3, a, a + 3)
    if mixed:
        lo_map = lambda a, g, ti: (a, 0, ti, 0)
        hi_map = lambda a, g, ti: (src1(a), 0, ti, 0)
    else:
        src = lambda a, g: jnp.where(g < ng // 2, a, src1(a))
        blocks_per_dir = ng // 2
        lo_map = lambda a, g, ti: (src(a, g), 2 * (g % blocks_per_dir), ti, 0)
        hi_map = lambda a, g, ti: (src(a, g), 2 * (g % blocks_per_dir) + 1, ti, 0)
    out = pl.pallas_call(
        functools.partial(_to_chains_kernel, n=n, h=h, nb=nb, tt=tt),
        grid=(N_CHAIN_ARRAYS, ng, l // tt),
        in_specs=[pl.BlockSpec((None, nb, tt, w), lo_map), pl.BlockSpec((None, nb, tt, w), hi_map)],
        out_specs=pl.BlockSpec((None, None, tt * ROW_PITCH, LANES), lambda a, g, ti: (a, g, ti, 0)),
        out_shape=jax.ShapeDtypeStruct((N_CHAIN_ARRAYS, ng, l * ROW_PITCH, LANES), F32),
        scratch_shapes=[pltpu.VMEM((2 * nb, h * ROW_PITCH, tt), F32)],
        compiler_params=_cparams(3),
        name="to_chains",
    )(stack, stack)
    return out.reshape(N_CHAIN_ARRAYS, ng, l, ROW_PITCH, LANES)


def _from_chains_kernel(a_ref, b_ref, o_ref, t_scr, *, n, h, nb, tt, mixed):
    lane = lax.broadcasted_iota(jnp.int32, (tt, LANES), 1)
    nk = LANES // h
    for c in range(n):
        va = a_ref[pl.ds(c, tt, stride=ROW_PITCH), :]
        vb = b_ref[pl.ds(c, tt, stride=ROW_PITCH), :]
        tiles = [jnp.where(lane >= LANES // 2, vb, va)] if mixed else [va, vb]
        for idx, v in enumerate(tiles):
            vt = v.T
            for k in range(nk):
                t_scr[idx * nk + k, pl.ds(c, h, stride=ROW_PITCH), :] = vt[k * h:(k + 1) * h]
    for bb in range(nb):
        k0, k1 = (bb, nb + bb) if mixed else (bb, nk + bb)
        parts = [t_scr[k0, hh * ROW_PITCH:hh * ROW_PITCH + n, :] + t_scr[k1, hh * ROW_PITCH:hh * ROW_PITCH + n, :]
                 for hh in range(h)]
        o_ref[bb] = jnp.concatenate(parts, axis=0).T


def from_chains(yf, yb, b, h, n):
    ng, l, _, _ = yf.shape
    _, mixed = _chain_groups(b, h)
    tt = CHAIN_TILE
    nk = LANES // h
    nb = nk // 2 if mixed else nk
    yf2 = yf.reshape(ng, l * ROW_PITCH, LANES)
    yb2 = yb.reshape(ng, l * ROW_PITCH, LANES)
    if mixed:
        a_map = lambda gb, ti: (0, ti, 0)
        b_map = a_map
    else:
        a_map = lambda gb, ti: (gb, ti, 0)
        b_map = lambda gb, ti: (ng // 2 + gb, ti, 0)
    spec = lambda m: pl.BlockSpec((None, tt * ROW_PITCH, LANES), m)
    return pl.pallas_call(
        functools.partial(_from_chains_kernel, n=n, h=h, nb=nb, tt=tt, mixed=mixed),
        grid=(b // nb, l // tt),
        in_specs=[spec(a_map), spec(b_map)],
        out_specs=pl.BlockSpec((nb, tt, h * n), lambda gb, ti: (gb, ti, 0)),
        out_shape=jax.ShapeDtypeStruct((b, l, h * n), F32),
        scratch_shapes=[pltpu.VMEM(((1 if mixed else 2) * nk, h * ROW_PITCH, tt), F32)],
        compiler_params=_cparams(2),
        name="from_chains",
    )(yf2, yb2)


def _rwkv_scan_kernel(*refs, n, tt, nt, ng, mixed, has_state, emit_state):
    it = iter(refs)
    fw = [next(it) for _ in range(N_CHAIN_ARRAYS)]
    bw = [next(it) for _ in range(N_CHAIN_ARRAYS)]
    a_next_ref, a_prev_ref = next(it), next(it)
    s0_ref = next(it) if has_state else None
    yf_ref, yb_ref = next(it), next(it)
    sfin_ref = next(it) if emit_state else None
    s_scr, sa_scr, gam_scr = next(it), next(it), next(it)
    r_buf, v_buf, a_buf, an_buf, k_buf, b_buf = (next(it) for _ in range(N_CHAIN_ARRAYS))
    g = pl.program_id(0)
    ti = pl.program_id(1)

    lane = lax.broadcasted_iota(jnp.int32, (n, LANES), 1)
    split = LANES // 2 if mixed else jnp.where(g < ng // 2, LANES, 0)
    is_bwd = lane >= split

    def sel(idx, s):
        return jnp.where(is_bwd, bw[idx][tt - 1 - s, 0:n, :], fw[idx][s, 0:n, :])

    for s in range(tt):
        a_buf[s] = sel(2, s)
    a_after = jnp.where(is_bwd, a_prev_ref[0, 0:n, :], a_next_ref[0, 0:n, :])
    a_buf[tt] = jnp.where(ti == nt - 1, 0.0, a_after)
    gam = jnp.ones((n, LANES), F32)
    for s in range(tt):
        gam = gam * sel(3, s)
        inv = 1.0 / gam
        r_buf[s] = sel(0, s) * gam
        v_buf[s] = sel(1, s)
        k_buf[s] = sel(4, s) * inv
        b_buf[s] = sel(5, s) * inv
        an_buf[s] = a_buf[s + 1] * gam
    gam_scr[...] = gam

    @pl.when(ti == 0)
    def _():
        if has_state:
            acc = jnp.zeros(sa_scr.shape, F32)
            for j in range(n):
                sj = s0_ref[j]
                s_scr[j] = sj
                acc = acc + sj * a_buf[0, j:j + 1, :]
            sa_scr[...] = acc
        else:
            s_scr[...] = jnp.zeros_like(s_scr)
            sa_scr[...] = jnp.zeros_like(sa_scr)

    pad = jnp.zeros((ROW_PITCH - n, LANES), F32)

    def step(t, sa):
        vt = v_buf[t]
        y = jnp.zeros_like(sa)
        san = jnp.zeros_like(sa)
        for j in range(n):
            new = s_scr[j] + (sa * b_buf[t, j:j + 1, :] + vt * k_buf[t, j:j + 1, :])
            s_scr[j] = new
            y = y + new * r_buf[t, j:j + 1, :]
            san = san + new * an_buf[t, j:j + 1, :]
        yf_ref[t, 0:n, :] = y
        yf_ref[t, n:, :] = pad
        yb_ref[tt - 1 - t, 0:n, :] = y
        yb_ref[tt - 1 - t, n:, :] = pad
        return san

    sa_scr[...] = lax.fori_loop(0, tt, step, sa_scr[...])
    for j in range(n):
        s_scr[j] = s_scr[j] * gam_scr[j:j + 1, :]

    if emit_state:
        @pl.when(ti == nt - 1)
        def _():
            sfin_ref[...] = s_scr[...]


def rwkv_scan(chains, s0, n, mixed, emit_state, tt=32):
    _, ng, l, rp, lanes = chains.shape
    nt = l // tt
    has_state = s0 is not None

    def seq(a, rev):
        if rev:
            return pl.BlockSpec((None, None, tt, rp, lanes), lambda g, ti: (a, g, nt - 1 - ti, 0, 0))
        return pl.BlockSpec((None, None, tt, rp, lanes), lambda g, ti: (a, g, ti, 0, 0))

    one = lambda m: pl.BlockSpec((None, None, 1, rp, lanes), m)
    st = pl.BlockSpec((None, n, n, lanes), lambda g, ti: (g, 0, 0, 0))
    yspec_f = pl.BlockSpec((None, tt, rp, lanes), lambda g, ti: (g, ti, 0, 0))
    yspec_b = pl.BlockSpec((None, tt, rp, lanes), lambda g, ti: (g, nt - 1 - ti, 0, 0))
    args = [chains] * (2 * N_CHAIN_ARRAYS + 2)
    specs = ([seq(a, False) for a in range(N_CHAIN_ARRAYS)] + [seq(a, True) for a in range(N_CHAIN_ARRAYS)]
             + [one(lambda g, ti: (2, g, jnp.minimum((ti + 1) * tt, l - 1), 0, 0)),
                one(lambda g, ti: (2, g, jnp.maximum((nt - 1 - ti) * tt - 1, 0), 0, 0))])
    if has_state:
        args.append(s0)
        specs.append(st)
    out_shape = [jax.ShapeDtypeStruct((ng, l, rp, lanes), F32)] * 2
    out_specs = [yspec_f, yspec_b]
    if emit_state:
        out_shape.append(jax.ShapeDtypeStruct((ng, n, n, lanes), F32))
        out_specs.append(st)
    seq_buf = pltpu.VMEM((tt, n, lanes), F32)
    return pl.pallas_call(
        functools.partial(_rwkv_scan_kernel, n=n, tt=tt, nt=nt, ng=ng, mixed=mixed,
                          has_state=has_state, emit_state=emit_state),
        grid=(ng, nt),
        in_specs=specs,
        out_specs=out_specs,
        out_shape=out_shape,
        scratch_shapes=[pltpu.VMEM((n, n, lanes), F32), pltpu.VMEM((n, lanes), F32), pltpu.VMEM((n, lanes), F32),
                        seq_buf, seq_buf, pltpu.VMEM((tt + 1, n, lanes), F32), seq_buf, seq_buf, seq_buf],
        compiler_params=_cparams(2),
        name="rwkv_scan",
    )(*args)


def _state_to_chains(s):
    b, _, h, n, _ = s.shape
    ng = (2 * b * h) // LANES
    return s.transpose(4, 3, 1, 0, 2).reshape(n, n, ng, LANES).transpose(2, 0, 1, 3)


def _state_from_chains(s, b, h):
    ng, n, _, _ = s.shape
    return s.transpose(1, 2, 0, 3).reshape(n, n, 2, b, h).transpose(3, 2, 4, 1, 0)


MLSTM_BIAS_ARG = 4


def _mlstm_kernel(*refs, nr, nc, emit_state, **kw):
    ci = pl.program_id(1)
    rows = [[ref if i == MLSTM_BIAS_ARG else ref.at[r] for i, ref in enumerate(refs)] for r in range(nr)]

    @pl.when(ci == 0)
    def _():
        for row in rows:
            _mlstm_row(*row, phase="load", emit_state=emit_state, **kw)

    for row in rows:
        _mlstm_row(*row, phase="chunk", emit_state=emit_state, **kw)

    if emit_state:
        @pl.when(ci == nc - 1)
        def _():
            for row in rows:
                _mlstm_row(*row, phase="store", emit_state=emit_state, **kw)


def _mlstm_row(*refs, phase, h, dh, c, has_state, emit_state):
    it = iter(refs)
    qkvf_ref, gf_ref, qkvb_ref, gb_ref, bias_ref = (next(it) for _ in range(5))
    if has_state:
        c0_ref, n0_ref, m0_ref = next(it), next(it), next(it)
    hf_ref, hb_ref = next(it), next(it)
    if emit_state:
        cfin_ref, nfin_ref, mfin_ref = next(it), next(it), next(it)
    cma_scr, m_scr = next(it), next(it)

    if phase == "load":
        if has_state:
            row0 = lax.broadcasted_iota(jnp.int32, (dh, dh), 0) == 0
            for s in range(2 * h):
                cma_scr[s, :, 0:dh] = c0_ref[s // h, s % h]
                cma_scr[s, :, dh:] = jnp.where(row0, n0_ref[s:s + 1, :], 0.0).T
            m_scr[...] = m0_ref[...]
        else:
            cma_scr[...] = jnp.zeros_like(cma_scr)
            m_scr[...] = jnp.zeros_like(m_scr)
        return
    if phase == "store":
        for s in range(2 * h):
            cfin_ref[s // h, s % h] = cma_scr[s, :, 0:dh]
            nfin_ref[s:s + 1, :] = cma_scr[s, :, dh:].T[0:1, :]
        mfin_ref[...] = m_scr[...]
        return

    ii = lax.broadcasted_iota(jnp.int32, (c, c), 0)
    jj = lax.broadcasted_iota(jnp.int32, (c, c), 1)
    gl_lane = lax.broadcasted_iota(jnp.int32, (c, LANES), 1)
    is_fg = (gl_lane >= 2 * h) & (gl_lane < 4 * h)
    ones_col = (lax.broadcasted_iota(jnp.int32, (c, dh), 1) == 0).astype(F32)

    for d, (qkv_ref, g_ref, o_ref) in enumerate(((qkvf_ref, gf_ref, hf_ref), (qkvb_ref, gb_ref, hb_ref))):
        causal = (ii >= jj) if d == 0 else (ii <= jj)
        gx = g_ref[...] + bias_ref[...]
        gl = jnp.where(is_fg, -_softplus(-gx), gx)
        bcum = _dot_exact_lhs(causal.astype(BF16), gl)
        glt = gl.T
        brow = _dot_exact_rhs(glt, (~causal).astype(BF16) + (ii == jj).astype(BF16))
        qkv = qkv_ref[...]
        kt_all = (qkv[:, h * dh:2 * h * dh] * (dh ** -0.5)).T
        end = c - 1 if d == 0 else 0
        outs = []
        for hh in range(h):
            ig_c = d * h + hh
            fg_c = 2 * h + d * h + hh
            sr = d * h + hh
            q = qkv[:, hh * dh:(hh + 1) * dh]
            k = qkv[:, (h + hh) * dh:(h + hh + 1) * dh] * (dh ** -0.5)
            v_aug = jnp.concatenate([qkv[:, (2 * h + hh) * dh:(2 * h + hh + 1) * dh], ones_col], axis=-1)
            kt = kt_all[hh * dh:(hh + 1) * dh, :]
            b_col = bcum[:, fg_c:fg_c + 1]
            b_row = brow[fg_c:fg_c + 1, :]
            c_row = glt[ig_c:ig_c + 1, :] - b_row
            m = m_scr[sr:sr + 1, 0:1]
            cma = cma_scr[sr]
            dmat = jnp.where(causal, c_row, -jnp.inf)
            g_col = jnp.maximum(m, jnp.max(dmat, axis=-1, keepdims=True))
            s = _bdot_nt(q, k) * jnp.exp(dmat - g_col)
            nd = _bdot(s, v_aug) + jnp.exp(m - g_col) * _bdot(q, cma)
            den = nd[:, dh:dh + 1]
            outs.append(nd[:, 0:dh] / jnp.maximum(jnp.abs(den), jnp.exp(-(b_col + g_col))))
            bl = b_row[:, end:end + 1]
            wl_row = bl + c_row
            m_new = jnp.maximum(bl + m, jnp.max(wl_row, axis=-1, keepdims=True))
            cma_scr[sr] = jnp.exp(bl + m - m_new) * cma + _bdot(kt * jnp.exp(wl_row - m_new), v_aug)
            m_scr[sr:sr + 1, :] = jnp.broadcast_to(m_new, (1, LANES))
        o_ref[...] = jnp.concatenate(outs, axis=-1)


def mlstm_scan(qkv, gates, bias_row, c0, n0, m0, h, dh, emit_state, nr=8):
    b, l, _ = qkv.shape
    c = CHUNK
    nc = l // c
    has_state = c0 is not None
    w = qkv.shape[-1]
    fwd = lambda ww: pl.BlockSpec((nr, c, ww), lambda bb, ci: (bb, ci, 0))
    bwd = lambda ww: pl.BlockSpec((nr, c, ww), lambda bb, ci: (bb, nc - 1 - ci, 0))
    cspec = pl.BlockSpec((nr, 2, h, dh, dh), lambda bb, ci: (bb, 0, 0, 0, 0))
    rspec = pl.BlockSpec((nr, 2 * h, LANES), lambda bb, ci: (bb, 0, 0))
    args = [qkv, gates, qkv, gates, bias_row]
    specs = [fwd(w), fwd(LANES), bwd(w), bwd(LANES), _full_spec(bias_row)]
    if has_state:
        args += [c0, n0.reshape(b, 2 * h, dh), jnp.broadcast_to(m0.reshape(b, 2 * h, 1), (b, 2 * h, LANES))]
        specs += [cspec, rspec, rspec]
    out_shape = [jax.ShapeDtypeStruct((b, l, h * dh), F32)] * 2
    out_specs = [fwd(h * dh), bwd(h * dh)]
    if emit_state:
        out_shape += [jax.ShapeDtypeStruct((b, 2, h, dh, dh), F32),
                      jax.ShapeDtypeStruct((b, 2 * h, dh), F32), jax.ShapeDtypeStruct((b, 2 * h, LANES), F32)]
        out_specs += [cspec, rspec, rspec]
    return pl.pallas_call(
        functools.partial(_mlstm_kernel, nr=nr, nc=nc, emit_state=emit_state, h=h, dh=dh, c=c,
                          has_state=has_state),
        grid=(b // nr, nc),
        in_specs=specs,
        out_specs=out_specs,
        out_shape=out_shape,
        scratch_shapes=[pltpu.VMEM((nr, 2 * h, dh, 2 * dh), F32), pltpu.VMEM((nr, 2 * h, LANES), F32)],
        compiler_params=_cparams(2),
        name="mlstm_scan",
    )(*args)


def _attn_prep_kernel(*refs, hq, hkv, hd, latent):
    it = iter(refs)
    q_ref, kv_ref, gq_ref, gk_ref, segq_ref, segk_ref = (next(it) for _ in range(6))
    if latent:
        cq_ref, sq_ref, ck_ref, sk_ref = (next(it) for _ in range(4))
    qo_ref, ko_ref, vo_ref = next(it), next(it), next(it)
    q = q_ref[...]
    kv = kv_ref[...]
    kw = hkv * hd
    k = kv[:, :kw]
    v = kv[:, kw:]
    qn = q * lax.rsqrt(_dot_exact_rhs(q * q, segq_ref[...], pieces=2) * (1.0 / hd) + RMS_EPS) * gq_ref[...]
    kn = k * lax.rsqrt(_dot_exact_rhs(k * k, segk_ref[...], pieces=2) * (1.0 / hd) + RMS_EPS) * gk_ref[...]
    if latent:
        qn = _rope(qn, cq_ref[...], sq_ref[...], hd // 4)
        kn = _rope(kn, ck_ref[...], sk_ref[...], hd // 4)
    qo_ref[...] = qn * (hd ** -0.5 * LOG2E)
    for j in range(hkv):
        ko_ref[j] = kn[:, j * hd:(j + 1) * hd]
        vo_ref[j] = v[:, j * hd:(j + 1) * hd]


def attn_prep(q, kv, qk_gain, hq, hkv, hd, latent, tm):
    b, l, wq = q.shape
    wkv = kv.shape[-1]
    kw = hkv * hd
    params = [jnp.tile(qk_gain[0], hq).reshape(1, wq), jnp.tile(qk_gain[1], hkv).reshape(1, kw),
              _seg_ones(wq, hd), _seg_ones(kw, hd)]
    args = [q, kv] + params
    specs = [_tok_spec(tm, wq), _tok_spec(tm, wkv)] + [_full_spec(a) for a in params]
    if latent:
        cq, sq = _rope_tables(l, hd, hq)
        ck, sk = _rope_tables(l, hd, hkv)
        args += [cq, sq, ck, sk]
        specs += [pl.BlockSpec((tm, wq), lambda bb, i: (i, 0))] * 2 + [pl.BlockSpec((tm, kw), lambda bb, i: (i, 0))] * 2
    kvspec = pl.BlockSpec((None, hkv, tm, hd), lambda bb, i: (bb, 0, i, 0))
    return pl.pallas_call(
        functools.partial(_attn_prep_kernel, hq=hq, hkv=hkv, hd=hd, latent=latent),
        grid=(b, l // tm),
        in_specs=specs,
        out_specs=[_tok_spec(tm, wq), kvspec, kvspec],
        out_shape=[jax.ShapeDtypeStruct((b, l, wq), F32)] + [jax.ShapeDtypeStruct((b, hkv, l, hd), F32)] * 2,
        compiler_params=_cparams(2),
        name="attn_prep",
    )(*args)


def _attn_kernel(*refs, g, hd, tq, nsub, kb, has_ctx):
    it = iter(refs)
    q_ref, k_ref, v_ref = next(it), next(it), next(it)
    if has_ctx:
        ck_ref, cv_ref = next(it), next(it)
    o_ref = next(it)
    s_scr = next(it)
    sources =([(ck_ref, cv_ref)] if has_ctx else []) + [(k_ref, v_ref)]
    maxes = []
    for u in range(nsub):
        q = q_ref[u * tq:(u + 1) * tq, :]
        qs = jnp.concatenate([q[:, i * hd:(i + 1) * hd] for i in range(g)], axis=0).astype(BF16)
        m = None
        row = 0
        for kr, _ in sources:
            for j in range(kr.shape[0] // kb):
                st = _bdot_nt(kr[j * kb:(j + 1) * kb, :], qs)
                s_scr[u, row:row + kb, :] = st
                bm = jnp.max(st, axis=0, keepdims=True)
                m = bm if m is None else jnp.maximum(m, bm)
                row += kb
        maxes.append(m)
    for u in range(nsub):
        acc = jnp.zeros((hd, g * tq), F32)
        den = jnp.zeros((1, g * tq), F32)
        row = 0
        for _, vr in sources:
            for j in range(vr.shape[0] // kb):
                p = jnp.exp2(s_scr[u, row:row + kb, :] - maxes[u])
                den = den + jnp.sum(p, axis=0, keepdims=True)
                acc = acc + _bdot_tn(vr[j * kb:(j + 1) * kb, :], p)
                row += kb
        ot = acc / den
        o = jnp.concatenate([ot, jnp.zeros((LANES - hd, g * tq), F32)], axis=0).T
        for i in range(g):
            o_ref[u * tq:(u + 1) * tq, i * hd:(i + 1) * hd] = o[i * tq:(i + 1) * tq, 0:hd]


def attention(q, k, v, ctx_k, ctx_v, layer_o, hq, hkv, hd, tq, nsub):
    b, l, wq = q.shape
    g = hq // hkv
    has_ctx = ctx_k is not None
    qspec = pl.BlockSpec((None, nsub * tq, g * hd), lambda bb, j, i: (bb, i, j))
    kvspec = pl.BlockSpec((None, None, l, hd), lambda bb, j, i: (bb, j, 0, 0))
    args = [q, k, v]
    specs = [qspec, kvspec, kvspec]
    n_keys = l
    if has_ctx:
        s = ctx_k.shape[3]
        n_keys += s
        cspec = pl.BlockSpec((None, None, None, s, hd), lambda bb, j, i: (bb, layer_o, j, 0, 0))
        args += [ctx_k, ctx_v]
        specs += [cspec, cspec]
    kb = min(256, l)
    return pl.pallas_call(
        functools.partial(_attn_kernel, g=g, hd=hd, tq=tq, nsub=nsub, kb=kb, has_ctx=has_ctx),
        grid=(b, hkv, l // (nsub * tq)),
        in_specs=specs,
        out_specs=qspec,
        out_shape=jax.ShapeDtypeStruct((b, l, wq), F32),
        scratch_shapes=[pltpu.VMEM((nsub, n_keys, g * tq), F32)],
        compiler_params=_cparams(3),
        name="gqa_attention",
    )(*args)


DENSE_TILE = 512
PREP_TILE = 512
QUERY_TILE = 128
QUERY_SUBTILES = 2


def _tile_plan(seq_len):
    return dict(tm=DENSE_TILE, tp=min(seq_len, PREP_TILE), tq=QUERY_TILE, nsub=QUERY_SUBTILES)


def _pad_rows(w, lo, total):
    return jnp.zeros((total, w.shape[-1]), w.dtype).at[lo:lo + w.shape[0]].set(w)


def kernel(x_prompt, x_sample, state_ret, state_rwkv, state_mlstm_c, state_mlstm_n, state_mlstm_m, cache_k, cache_v, c, c_ctx, ada_w, ada_b, norm_g, ffn_w1, ffn_w2, w_in_even, w_out_even, ret_log_decay, ret_gn_w, rwkv_mu, rwkv_w0, rwkv_w2, rwkv_a0, rwkv_a2, rwkv_g2, rwkv_k_k, rwkv_k_a, rwkv_r_k, rwkv_ln_w, rwkv_ln_b, w_in_odd, w_out_odd, mlstm_i_bias, mlstm_f_bias, mlstm_norm_w, attn_qk_norm, final_norm):
    depth = ada_w.shape[0]
    d_model = x_prompt.shape[-1]
    h_a, dk_a, dv_a = state_ret.shape[3:]
    h_b, hs_b = state_rwkv.shape[3:5]
    h_c, dh_c = state_mlstm_c.shape[3:5]
    hkv_d, hd_d = cache_k.shape[2], cache_k.shape[4]
    wa, wb, wc = h_a * dv_a, h_b * hs_b, h_c * dh_c
    wd = w_out_odd.shape[1] - wc
    hq_d = wd // hd_d
    kvw = hkv_d * hd_d
    n_dec = c.shape[0]
    lora_w, lora_a, lora_g = rwkv_w2.shape[2], rwkv_a2.shape[2], rwkv_g2.shape[1]
    lora_tot = lora_w + lora_a + lora_g

    rows = -(-(n_dec + 1) // SUBLANES) * SUBLANES
    cond = jnp.zeros((rows, d_model), F32).at[:n_dec].set(c).at[n_dec].set(c_ctx)
    mod = modulation_all(cond, ada_w, ada_b).reshape(depth, rows, N_MOD, d_model)

    seg_b = _seg_ones(wb, hs_b)
    streams = {
        "prompt": dict(x=x_prompt, latent=False, **_tile_plan(x_prompt.shape[1])),
        "sample": dict(x=x_sample, latent=True, **_tile_plan(x_sample.shape[1])),
    }
    new_states = {}

    for l in range(depth):
        w1a, w1b = ffn_w1[l, 0].astype(BF16), ffn_w1[l, 1].astype(BF16)
        w2a, w2b = ffn_w2[l, 0].astype(BF16), ffn_w2[l, 1].astype(BF16)
        fin = final_norm if l == depth - 1 else None
        if l % 2 == 0:
            e = l // 2
            w_in = w_in_even[e].astype(BF16)
            w_out = w_out_even[e].astype(BF16)
            splits = (2 * h_a * dk_a, wa, wa, 3 * wb + lora_tot)
            w2p = jnp.stack([_pad_rows(rwkv_w2[e, d], 0, lora_tot) for d in range(2)]).astype(BF16)
            a2p = jnp.stack([_pad_rows(rwkv_a2[e, d], lora_w, lora_tot) for d in range(2)]).astype(BF16)
            g2p = _pad_rows(rwkv_g2[e], lora_w + lora_a, lora_tot).astype(BF16)
        else:
            o = l // 2
            wi = w_in_odd[o]
            g0 = 3 * wc
            p_c = g0 + 4 * h_c + wc
            gates_w = jnp.zeros((d_model, LANES), F32).at[:, :4 * h_c].set(wi[:, g0:g0 + 4 * h_c])
            w_in = jnp.concatenate([wi[:, :g0], wi[:, g0 + 4 * h_c:p_c], wi[:, p_c:p_c + wd],
                                    wi[:, p_c + wd:], gates_w], axis=1).astype(BF16)
            w_out = w_out_odd[o].astype(BF16)
            splits = (3 * wc, wc, wd, 2 * kvw, LANES)
            bias_row = jnp.zeros((1, LANES), F32).at[0, :2 * h_c].set(mlstm_i_bias[o].reshape(-1))
            bias_row = bias_row.at[0, 2 * h_c:4 * h_c].set(mlstm_f_bias[o].reshape(-1))

        for name, st in streams.items():
            x, latent, tm = st["x"], st["latent"], st["tm"]
            b, seq, _ = x.shape
            m = mod[l, :n_dec] if latent else mod[l, n_dec:n_dec + 1]
            emit = not latent
            x1, parts = dense_pre(x, m, norm_g[l, 0], norm_g[l, 1], w1a, w2a, w_in, splits, tm)
            if l % 2 == 0:
                qk_a, v_a, g_a, p_b = parts
                ret = retention_scan(qk_a, v_a, ret_log_decay[e], state_ret[:, e] if latent else None,
                                     h_a, dk_a, dv_a, latent, emit)
                stack, g_b, bonus = rwkv_prep(
                    p_b, rwkv_mu[e], rwkv_w0[e], w2p, rwkv_a0[e], a2p, g2p, rwkv_k_k[e], rwkv_k_a[e],
                    rwkv_r_k[e].reshape(-1), seg_b, wb, st["tp"])
                _, mixed = _chain_groups(b, h_b)
                scan = rwkv_scan(to_chains(stack, h_b),
                                 _state_to_chains(state_rwkv[:, e]) if latent else None, hs_b, mixed, emit)
                y = from_chains(scan[0], scan[1], b, h_b, hs_b)
                if emit:
                    new_states.setdefault("ret", []).append(ret[2])
                    new_states.setdefault("rwkv", []).append(_state_from_chains(scan[2], b, h_b))
                x = _dense_post(_post_even_kernel, "dense_post_even", x1,
                                [ret[0], ret[1], g_a, y, bonus, g_b], m,
                                [_row(ret_gn_w[e]), _row(rwkv_ln_w[e]), _row(rwkv_ln_b[e]), seg_b,
                                 _row(norm_g[l, 2]), w_out, w1b, w2b],
                                fin, tm, d_ff=w2b.shape[0], dv=dv_a, hs=hs_b)
            else:
                qkv_c, og, q_d, kv_d, gates = parts
                ml = mlstm_scan(qkv_c, gates, bias_row,
                                state_mlstm_c[:, o] if latent else None,
                                state_mlstm_n[:, o] if latent else None,
                                state_mlstm_m[:, o] if latent else None, h_c, dh_c, emit)
                qn, kn, vn = attn_prep(q_d, kv_d, attn_qk_norm[o], hq_d, hkv_d, hd_d, latent, st["tp"])
                out_d = attention(qn, kn, vn, cache_k if latent else None, cache_v if latent else None, o,
                                  hq_d, hkv_d, hd_d, st["tq"], st["nsub"])
                if emit:
                    new_states.setdefault("mc", []).append(ml[2])
                    new_states.setdefault("mn", []).append(ml[3].reshape(b, 2, h_c, dh_c))
                    new_states.setdefault("mm", []).append(ml[4][:, :, 0].reshape(b, 2, h_c))
                    new_states.setdefault("k", []).append(kn)
                    new_states.setdefault("v", []).append(vn)
                x = _dense_post(_post_odd_kernel, "dense_post_odd", x1, [ml[0], ml[1], og, out_d], m,
                                [_row(mlstm_norm_w[o]), _row(norm_g[l, 2]), w_out, w1b, w2b],
                                fin, tm, d_ff=w2b.shape[0], dh=dh_c)
            st["x"] = x

    stack = lambda key: jnp.stack(new_states[key], axis=1)
    return (streams["prompt"]["x"], streams["sample"]["x"], stack("ret"), stack("rwkv"), stack("mc"),
            stack("mn"), stack("mm"), stack("k"), stack("v"))
```
